```python
import math
import jax, jax.numpy as jnp
from jax import lax
import numpy as np

D_MODEL = 1024
BATCH = 2
SEQ = 8192
DEPTH = 4
DEC_BATCH = 128
DEC_SEQ = 1
PAST_LEN = 2048
PAGE_SIZE = 128

HEAD_DIM = 64
SSM_WIDTH = D_MODEL // 4
SSM_GROUP = 16
SSM_GROUPS = SSM_WIDTH // SSM_GROUP
SSM_STATE = 64
SB_HEADS = 4
SB_WIDTH = SB_HEADS * HEAD_DIM
NSA_HEADS = 8
NSA_KV_HEADS = 2
NSA_REP = NSA_HEADS // NSA_KV_HEADS
NSA_WIDTH = NSA_HEADS * HEAD_DIM
NSA_KV_WIDTH = NSA_KV_HEADS * HEAD_DIM
CMP_LEN = 32
CMP_STRIDE = 16
SEL_LEN = 64
SEL_TOPK = 16
WINDOW = 512
N_BRANCH = 3
MIX_WIDTH = SSM_WIDTH + SB_WIDTH + NSA_WIDTH
IN_SPLITS = (SSM_WIDTH, SB_WIDTH, SB_WIDTH, SB_WIDTH, NSA_WIDTH, 6 * NSA_KV_WIDTH, 3 * NSA_HEADS, N_BRANCH * D_MODEL)
IN_WIDTH = SSM_WIDTH + 3 * SB_WIDTH + NSA_WIDTH + 6 * NSA_KV_WIDTH + 3 * NSA_HEADS + N_BRANCH * D_MODEL
Q_BLOCK = 128
ROPE_THETA = 10000.0
MOE_GROUPS = 4
EXPERTS_PER_GROUP = 4
N_EXPERTS = MOE_GROUPS * EXPERTS_PER_GROUP
EXPERT_FF = D_MODEL // 4
MOE_TOPK = 2
EPS = 1e-6
NEG = -1e30
FORCE = 1e9

kernel_name = "hybrid_s5_stickbreak_nsa_hmoe_step"

F32 = jnp.float32


def rms_norm(x, g):
    xf = x.astype(F32)
    y = xf * lax.rsqrt(jnp.mean(xf * xf, axis=-1, keepdims=True) + EPS)
    return (y * g.astype(F32)).astype(x.dtype)


def rope(x, pos):
    half = HEAD_DIM // 2
    freqs = ROPE_THETA ** (-jnp.arange(half, dtype=F32) / half)
    ang = pos.astype(F32)[:, None] * freqs
    cos, sin = jnp.cos(ang)[:, None, :], jnp.sin(ang)[:, None, :]
    xf = x.astype(F32)
    x1, x2 = xf[..., :half], xf[..., half:]
    return jnp.concatenate([x1 * cos - x2 * sin, x1 * sin + x2 * cos], axis=-1).astype(x.dtype)


def ada_mod(c, w, b):
    m = jax.nn.silu(c) @ w + b
    return [t[:, None, :] for t in jnp.split(m, 6, axis=-1)]


def split_cols(p):
    out, start = [], 0
    for w in IN_SPLITS:
        out.append(p[..., start:start + w])
        start += w
    return out


def mixer_inputs(x, mods, g_norm, w_in_l):
    h = rms_norm(x, g_norm) * (1 + mods[1]) + mods[0]
    return split_cols(h @ w_in_l)


def to_blocks(a):
    b, t = a.shape[:2]
    return a.reshape(b, t // Q_BLOCK, Q_BLOCK, *a.shape[2:]).swapaxes(0, 1)


def from_blocks(o):
    nb, b = o.shape[:2]
    return o.swapaxes(0, 1).reshape(b, nb * Q_BLOCK, *o.shape[3:])


def _cplx_affine(e1, e2):
    a1r, a1i, b1r, b1i = e1
    a2r, a2i, b2r, b2i = e2
    return (a2r * a1r - a2i * a1i, a2r * a1i + a2i * a1r,
            a2r * b1r - a2i * b1i + b2r, a2r * b1i + a2i * b1r + b2i)


def s5_branch(u, s0_re, s0_im, a_re, a_im, log_dt, b_re, b_im, c_re, c_im, d, w_glu, b_glu):
    bsz, t = u.shape[:2]
    a_re, a_im = a_re.astype(F32), a_im.astype(F32)
    dt = jnp.exp(log_dt.astype(F32))[:, None]
    mag = jnp.exp(a_re * dt)
    lb_re, lb_im = mag * jnp.cos(a_im * dt), mag * jnp.sin(a_im * dt)
    den = a_re * a_re + a_im * a_im
    co_re = ((lb_re - 1.0) * a_re + lb_im * a_im) / den
    co_im = (lb_im * a_re - (lb_re - 1.0) * a_im) / den
    b_re, b_im = b_re.astype(F32), b_im.astype(F32)
    bb_re = co_re[..., None] * b_re - co_im[..., None] * b_im
    bb_im = co_re[..., None] * b_im + co_im[..., None] * b_re
    ug = u.astype(F32).reshape(bsz, t, SSM_GROUPS, SSM_GROUP)
    bu_re = jnp.einsum('btgp,gnp->btgn', ug, bb_re)
    bu_im = jnp.einsum('btgp,gnp->btgn', ug, bb_im)
    s0r, s0i = s0_re.astype(F32), s0_im.astype(F32)
    bu_re = bu_re.at[:, 0].add(lb_re * s0r - lb_im * s0i)
    bu_im = bu_im.at[:, 0].add(lb_re * s0i + lb_im * s0r)
    la_re = jnp.broadcast_to(lb_re, bu_re.shape)
    la_im = jnp.broadcast_to(lb_im, bu_im.shape)
    _, _, s_re, s_im = lax.associative_scan(_cplx_affine, (la_re, la_im, bu_re, bu_im), axis=1)
    y = (jnp.einsum('btgn,gpn->btgp', s_re, c_re.astype(F32))
         - jnp.einsum('btgn,gpn->btgp', s_im, c_im.astype(F32))
         + d.astype(F32) * ug)
    y = jax.nn.gelu(y.reshape(bsz, t, SSM_WIDTH))
    y = y * jax.nn.sigmoid(y @ w_glu.astype(F32) + b_glu.astype(F32))
    return y.astype(u.dtype), s_re[:, -1], s_im[:, -1]


def sb_core(q, k, v, qpos, kpos):
    z = jnp.einsum('bqhd,bshd->bhqs', q.astype(F32), k.astype(F32)) / math.sqrt(HEAD_DIM)
    mask = kpos[None, :] < qpos[:, None]
    log_1m = jnp.where(mask, jax.nn.log_sigmoid(-z), 0.0)
    between = lax.cumsum(log_1m, axis=3, reverse=True) - log_1m
    w = jnp.where(mask, jnp.exp(jax.nn.log_sigmoid(z) + between), 0.0)
    return jnp.einsum('bhqs,bshd->bqhd', w, v.astype(F32)).astype(v.dtype)


def sb_prompt(q, k, v):
    bsz, t = q.shape[:2]
    kpos = jnp.arange(t)

    def block(args):
        qi, i = args
        return sb_core(qi, k, v, i * Q_BLOCK + jnp.arange(Q_BLOCK), kpos)

    o = lax.map(block, (to_blocks(q), jnp.arange(t // Q_BLOCK)))
    return from_blocks(o).reshape(bsz, t, SB_WIDTH)


def compress(rows, pe, w):
    bsz, length = rows.shape[:2]
    r = CMP_LEN // CMP_STRIDE
    n_chunks = length // CMP_STRIDE
    nc = n_chunks - r + 1
    chunks = rows[:, :n_chunks * CMP_STRIDE].reshape(bsz, n_chunks, CMP_STRIDE, NSA_KV_HEADS, HEAD_DIM)
    wj = w.reshape(r, CMP_STRIDE, HEAD_DIM, HEAD_DIM)
    pj = pe.reshape(r, CMP_STRIDE, HEAD_DIM)
    out = 0.0
    for j in range(r):
        out = out + jnp.einsum('bnsgd,sde->bnge', chunks[:, j:j + nc], wj[j]) + jnp.einsum('sd,sde->e', pj[j], wj[j])
    cend = jnp.arange(nc) * CMP_STRIDE + CMP_LEN - 1
    return out, cend


def cmp_to_sel(nc, ns):
    cs = np.arange(nc) * CMP_STRIDE
    ss = np.arange(ns) * SEL_LEN
    ov = np.clip(np.minimum(cs[:, None] + CMP_LEN, ss[None, :] + SEL_LEN) - np.maximum(cs[:, None], ss[None, :]), 0, None)
    return jnp.asarray(ov / CMP_LEN, dtype=F32)


def pad_blocks(a):
    pad = (-a.shape[1]) % SEL_LEN
    return jnp.pad(a, ((0, 0), (0, pad), (0, 0), (0, 0)))


def nsa_prepare(k_cmp, v_cmp, k_sel, v_sel, w_cmp, pe_cmp):
    kc, cend = compress(k_cmp, pe_cmp[0], w_cmp[0])
    vc, _ = compress(v_cmp, pe_cmp[1], w_cmp[1])
    ks, vs = pad_blocks(k_sel), pad_blocks(v_sel)
    cmap = cmp_to_sel(kc.shape[1], ks.shape[1] // SEL_LEN)
    return kc, vc, cend, cmap, ks, vs


def nsa_core(q_rope, q_nope, gates, qpos, kc, vc, cend, cmap, ks, vs, kw, vw, kwpos):
    bsz, nq = q_rope.shape[:2]
    scale = 1.0 / math.sqrt(HEAD_DIM)
    qr = q_rope.astype(F32).reshape(bsz, nq, NSA_KV_HEADS, NSA_REP, HEAD_DIM) * scale
    qn = q_nope.astype(F32).reshape(bsz, nq, NSA_KV_HEADS, NSA_REP, HEAD_DIM) * scale
    mc = cend[None, :] <= qpos[:, None]
    sc = jnp.einsum('bqgrd,bcgd->bgrqc', qn, kc.astype(F32))
    pc = jax.nn.softmax(jnp.where(mc, sc, NEG), axis=-1) * mc
    o_cmp = jnp.einsum('bgrqc,bcgd->bqgrd', pc, vc.astype(F32))
    ns = cmap.shape[1]
    topk = min(SEL_TOPK, ns)
    imp = jnp.einsum('bgrqc,cn->bgqn', pc, cmap)
    blk = jnp.arange(ns)[None, :]
    cur = (qpos // SEL_LEN)[:, None]
    imp = jnp.where(blk == cur, FORCE, jnp.where(blk < cur, imp, -1.0))
    _, idx = lax.top_k(imp, topk)
    ksb = ks.reshape(bsz, ns, SEL_LEN, NSA_KV_HEADS, HEAD_DIM).transpose(0, 3, 1, 2, 4)
    vsb = vs.reshape(bsz, ns, SEL_LEN, NSA_KV_HEADS, HEAD_DIM).transpose(0, 3, 1, 2, 4)
    bi = jnp.arange(bsz)[:, None, None, None]
    gi = jnp.arange(NSA_KV_HEADS)[None, :, None, None]
    kg, vg = ksb[bi, gi, idx], vsb[bi, gi, idx]
    spos = idx[..., None] * SEL_LEN + jnp.arange(SEL_LEN)
    ms = spos <= qpos[None, None, :, None, None]
    ss = jnp.einsum('bqgrd,bgqkld->bgrqkl', qr, kg.astype(F32))
    ss = jnp.where(ms[:, :, None], ss, NEG).reshape(bsz, NSA_KV_HEADS, NSA_REP, nq, topk * SEL_LEN)
    ps = jax.nn.softmax(ss, axis=-1).reshape(bsz, NSA_KV_HEADS, NSA_REP, nq, topk, SEL_LEN)
    o_sel = jnp.einsum('bgrqkl,bgqkld->bqgrd', ps, vg.astype(F32))
    mw = (kwpos[None, :] <= qpos[:, None]) & (kwpos[None, :] > qpos[:, None] - WINDOW) & (kwpos[None, :] >= 0)
    sw = jnp.einsum('bqgrd,bwgd->bgrqw', qr, kw.astype(F32))
    pw = jax.nn.softmax(jnp.where(mw, sw, NEG), axis=-1)
    o_win = jnp.einsum('bgrqw,bwgd->bqgrd', pw, vw.astype(F32))
    g = jax.nn.sigmoid(gates.astype(F32)).reshape(bsz, nq, NSA_KV_HEADS, NSA_REP, 3)
    o = g[..., 0:1] * o_cmp + g[..., 1:2] * o_sel + g[..., 2:3] * o_win
    return o.reshape(bsz, nq, NSA_WIDTH)


def nsa_prompt(q, gates, k_cmp, v_cmp, k_sel, v_sel, k_win, v_win, w_cmp, pe_cmp):
    t = q.shape[1]
    kc, vc, cend, cmap, ks, vs = nsa_prepare(k_cmp, v_cmp, k_sel, v_sel, w_cmp, pe_cmp)
    pad = ((0, 0), (WINDOW, 0), (0, 0), (0, 0))
    kwp, vwp = jnp.pad(k_win, pad), jnp.pad(v_win, pad)
    q_rope = rope(q, jnp.arange(t))

    def block(args):
        qr, qn, g, i = args
        start = i * Q_BLOCK
        qpos = start + jnp.arange(Q_BLOCK)
        kw = lax.dynamic_slice_in_dim(kwp, start, WINDOW + Q_BLOCK, axis=1)
        vw = lax.dynamic_slice_in_dim(vwp, start, WINDOW + Q_BLOCK, axis=1)
        kwpos = start - WINDOW + jnp.arange(WINDOW + Q_BLOCK)
        return nsa_core(qr, qn, g, qpos, kc, vc, cend, cmap, ks, vs, kw, vw, kwpos)

    o = lax.map(block, (to_blocks(q_rope), to_blocks(q), to_blocks(gates), jnp.arange(t // Q_BLOCK)))
    return from_blocks(o)


def merge_branches(y_a, y_b, y_c, gate_logits, w_branch_l, w_out_l, dtype):
    g = jax.nn.sigmoid(gate_logits.astype(F32)).astype(dtype)
    g = g.reshape(*gate_logits.shape[:-1], N_BRANCH, D_MODEL)
    br_a = y_a.astype(dtype) @ w_branch_l[:SSM_WIDTH]
    br_b = y_b.astype(dtype) @ w_branch_l[SSM_WIDTH:SSM_WIDTH + SB_WIDTH]
    br_c = y_c.astype(dtype) @ w_branch_l[SSM_WIDTH + SB_WIDTH:]
    merged = g[..., 0, :] * br_a + g[..., 1, :] * br_b + g[..., 2, :] * br_c
    return (merged @ w_out_l).astype(dtype)


def hier_moe(h, w_grp, b_grp, w_rt, b_rt, w_g, w_u, w_d):
    n = h.shape[0]
    g_logits = (h @ w_grp).astype(F32) + b_grp.astype(F32)
    p_grp = jax.nn.softmax(g_logits, axis=-1)
    g_sel = jnp.argmax(g_logits, axis=-1)
    e_logits = ((h @ w_rt).astype(F32) + b_rt.astype(F32)).reshape(n, MOE_GROUPS, EXPERTS_PER_GROUP)
    e_in = jnp.take_along_axis(e_logits, g_sel[:, None, None], axis=1)[:, 0]
    p_top, i_top = lax.top_k(jax.nn.softmax(e_in, axis=-1), MOE_TOPK)
    wts = p_top / jnp.sum(p_top, axis=-1, keepdims=True) * jnp.take_along_axis(p_grp, g_sel[:, None], axis=1)
    eid = g_sel[:, None] * EXPERTS_PER_GROUP + i_top
    comb = jnp.einsum('nk,nke->ne', wts, jax.nn.one_hot(eid, N_EXPERTS, dtype=F32))
    hid = jax.nn.silu(jnp.einsum('nd,edf->nef', h, w_g)) * jnp.einsum('nd,edf->nef', h, w_u)
    return jnp.einsum('nef,efd->nd', hid * comb[:, :, None].astype(hid.dtype), w_d)


def ffn_sublayer(x, mods, g_norm, w_grp, b_grp, w_rt, b_rt, w_g, w_u, w_d):
    h = rms_norm(x, g_norm) * (1 + mods[4]) + mods[3]
    out = hier_moe(h.reshape(-1, D_MODEL), w_grp, b_grp, w_rt, b_rt, w_g, w_u, w_d).reshape(x.shape)
    return x + mods[5] * out.astype(x.dtype)


def setup_inputs(seed: int = 0) -> dict:
    key = jax.random.key(seed)
    ks = iter(jax.random.split(key, 40))

    def nrm(shape, s):
        return jax.random.normal(next(ks), shape, F32) * s

    n_pages = PAST_LEN // PAGE_SIZE
    n_used = DEC_BATCH * n_pages
    n_pool = n_used + n_used // 4
    win_len = min(WINDOW, PAST_LEN)
    a_im0 = jnp.pi * jnp.arange(SSM_STATE, dtype=F32)
    return {
        "x_prompt": nrm((BATCH, SEQ, D_MODEL), 1.0),
        "x_sample": nrm((DEC_BATCH, DEC_SEQ, D_MODEL), 1.0),
        "cache_sb": nrm((DEPTH, n_pool, PAGE_SIZE, 2, SB_HEADS, HEAD_DIM), 1.0),
        "cache_nsa": nrm((DEPTH, n_pool, PAGE_SIZE, 4, NSA_KV_HEADS, HEAD_DIM), 1.0),
        "state_win": nrm((DEPTH, DEC_BATCH, win_len, 2, NSA_KV_HEADS, HEAD_DIM), 1.0),
        "state_ssm": nrm((DEPTH, DEC_BATCH, 2, SSM_GROUPS, SSM_STATE), 0.5),
        "page_table": jax.random.permutation(next(ks), n_pool)[:n_used].reshape(DEC_BATCH, n_pages).astype(jnp.int32),
        "c_prompt": nrm((BATCH, D_MODEL), 1.0),
        "c_sample": nrm((DEC_BATCH, D_MODEL), 1.0),
        "w_ada": nrm((DEPTH, D_MODEL, 6 * D_MODEL), 0.5 * D_MODEL ** -0.5),
        "b_ada": nrm((DEPTH, 6 * D_MODEL), 0.02),
        "norm_mix": 1.0 + nrm((DEPTH, D_MODEL), 0.05),
        "norm_ffn": 1.0 + nrm((DEPTH, D_MODEL), 0.05),
        "w_in": nrm((DEPTH, D_MODEL, IN_WIDTH), D_MODEL ** -0.5),
        "ssm_a_re": -0.5 + nrm((DEPTH, SSM_GROUPS, SSM_STATE), 0.01),
        "ssm_a_im": a_im0 + nrm((DEPTH, SSM_GROUPS, SSM_STATE), 0.01),
        "ssm_log_dt": jax.random.uniform(next(ks), (DEPTH, SSM_GROUPS), F32, math.log(0.001), math.log(0.1)),
        "ssm_b_re": nrm((DEPTH, SSM_GROUPS, SSM_STATE, SSM_GROUP), (2 * SSM_GROUP) ** -0.5),
        "ssm_b_im": nrm((DEPTH, SSM_GROUPS, SSM_STATE, SSM_GROUP), (2 * SSM_GROUP) ** -0.5),
        "ssm_c_re": nrm((DEPTH, SSM_GROUPS, SSM_GROUP, SSM_STATE), SSM_STATE ** -0.5),
        "ssm_c_im": nrm((DEPTH, SSM_GROUPS, SSM_GROUP, SSM_STATE), SSM_STATE ** -0.5),
        "ssm_d": nrm((DEPTH, SSM_GROUPS, SSM_GROUP), 1.0),
        "w_glu": nrm((DEPTH, SSM_WIDTH, SSM_WIDTH), SSM_WIDTH ** -0.5),
        "b_glu": nrm((DEPTH, SSM_WIDTH), 0.02),
        "nsa_w_cmp": nrm((DEPTH, 2, CMP_LEN, HEAD_DIM, HEAD_DIM), (CMP_LEN * HEAD_DIM) ** -0.5),
        "nsa_pe_cmp": nrm((DEPTH, 2, CMP_LEN, HEAD_DIM), 0.1),
        "w_branch": nrm((DEPTH, MIX_WIDTH, D_MODEL), SSM_WIDTH ** -0.5),
        "w_out": nrm((DEPTH, D_MODEL, D_MODEL), D_MODEL ** -0.5),
        "w_grp": nrm((DEPTH, D_MODEL, MOE_GROUPS), D_MODEL ** -0.5),
        "b_grp": nrm((DEPTH, MOE_GROUPS), 0.01),
        "w_rt": nrm((DEPTH, D_MODEL, N_EXPERTS), D_MODEL ** -0.5),
        "b_rt": nrm((DEPTH, N_EXPERTS), 0.01),
        "w_e_gate": nrm((DEPTH, N_EXPERTS, D_MODEL, EXPERT_FF), D_MODEL ** -0.5),
        "w_e_up": nrm((DEPTH, N_EXPERTS, D_MODEL, EXPERT_FF), D_MODEL ** -0.5),
        "w_e_down": nrm((DEPTH, N_EXPERTS, EXPERT_FF, D_MODEL), EXPERT_FF ** -0.5),
        "final_norm": 1.0 + nrm((D_MODEL,), 0.05),
    }


def reference(x_prompt, x_sample, cache_sb, cache_nsa, state_win, state_ssm, page_table, c_prompt, c_sample,
              w_ada, b_ada, norm_mix, norm_ffn, w_in, ssm_a_re, ssm_a_im, ssm_log_dt, ssm_b_re, ssm_b_im,
              ssm_c_re, ssm_c_im, ssm_d, w_glu, b_glu, nsa_w_cmp, nsa_pe_cmp, w_branch, w_out,
              w_grp, b_grp, w_rt, b_rt, w_e_gate, w_e_up, w_e_down, final_norm):
    xp, xs = x_prompt, x_sample
    bp, t = xp.shape[:2]
    bd, s = xs.shape[:2]
    pos_p = jnp.arange(t)
    pos_s = PAST_LEN + jnp.arange(s)
    win_len = state_win.shape[2]
    sb_p, sb_s, nsa_p, nsa_s, win_p, win_s, ssm_p, ssm_s = [], [], [], [], [], [], [], []
    for l in range(DEPTH):
        ssm_l = (ssm_a_re[l], ssm_a_im[l], ssm_log_dt[l], ssm_b_re[l], ssm_b_im[l], ssm_c_re[l], ssm_c_im[l],
                 ssm_d[l], w_glu[l], b_glu[l])
        moe_l = (w_grp[l], b_grp[l], w_rt[l], b_rt[l], w_e_gate[l], w_e_up[l], w_e_down[l])

        mp = ada_mod(c_prompt, w_ada[l], b_ada[l])
        u, sq, sk, sv, nq, nkv, ng, mg = mixer_inputs(xp, mp, norm_mix[l], w_in[l])
        s0 = jnp.zeros((bp, SSM_GROUPS, SSM_STATE), F32)
        y_a, sr, si = s5_branch(u, s0, s0, *ssm_l)
        k_sb = sk.reshape(bp, t, SB_HEADS, HEAD_DIM)
        v_sb = sv.reshape(bp, t, SB_HEADS, HEAD_DIM)
        y_b = sb_prompt(sq.reshape(bp, t, SB_HEADS, HEAD_DIM), k_sb, v_sb)
        kv6 = nkv.reshape(bp, t, 6, NSA_KV_HEADS, HEAD_DIM)
        k_sel = rope(kv6[:, :, 2], pos_p)
        k_win = rope(kv6[:, :, 4], pos_p)
        y_c = nsa_prompt(nq.reshape(bp, t, NSA_HEADS, HEAD_DIM), ng.reshape(bp, t, NSA_HEADS, 3),
                         kv6[:, :, 0], kv6[:, :, 1], k_sel, kv6[:, :, 3], k_win, kv6[:, :, 5],
                         nsa_w_cmp[l], nsa_pe_cmp[l])
        xp = xp + mp[2] * merge_branches(y_a, y_b, y_c, mg, w_branch[l], w_out[l], xp.dtype)
        xp = ffn_sublayer(xp, mp, norm_ffn[l], *moe_l)
        sb_p.append(jnp.stack([k_sb, v_sb], axis=2))
        nsa_p.append(jnp.stack([kv6[:, :, 0], kv6[:, :, 1], k_sel, kv6[:, :, 3]], axis=2))
        win_p.append(jnp.stack([k_win, kv6[:, :, 5]], axis=2)[:, t - min(WINDOW, t):])
        ssm_p.append(jnp.stack([sr, si], axis=1))

        ms = ada_mod(c_sample, w_ada[l], b_ada[l])
        u, sq, sk, sv, nq, nkv, ng, mg = mixer_inputs(xs, ms, norm_mix[l], w_in[l])
        y_a, sr, si = s5_branch(u, state_ssm[l][:, 0], state_ssm[l][:, 1], *ssm_l)
        new_sb = jnp.stack([sk.reshape(bd, s, SB_HEADS, HEAD_DIM), sv.reshape(bd, s, SB_HEADS, HEAD_DIM)], axis=2)
        past_sb = cache_sb[l, page_table].reshape(bd, PAST_LEN, 2, SB_HEADS, HEAD_DIM)
        kv_sb = jnp.concatenate([past_sb, new_sb.astype(past_sb.dtype)], axis=1)
        y_b = sb_core(sq.reshape(bd, s, SB_HEADS, HEAD_DIM), kv_sb[:, :, 0], kv_sb[:, :, 1],
                      pos_s, jnp.arange(PAST_LEN + s)).reshape(bd, s, SB_WIDTH)
        kv6 = nkv.reshape(bd, s, 6, NSA_KV_HEADS, HEAD_DIM)
        k_sel = rope(kv6[:, :, 2], pos_s)
        k_win = rope(kv6[:, :, 4], pos_s)
        new_nsa = jnp.stack([kv6[:, :, 0], kv6[:, :, 1], k_sel, kv6[:, :, 3]], axis=2)
        past_nsa = cache_nsa[l, page_table].reshape(bd, PAST_LEN, 4, NSA_KV_HEADS, HEAD_DIM)
        kv4 = jnp.concatenate([past_nsa, new_nsa.astype(past_nsa.dtype)], axis=1)
        kc, vc, cend, cmap, ksel, vsel = nsa_prepare(kv4[:, :, 0], kv4[:, :, 1], kv4[:, :, 2], kv4[:, :, 3],
                                                     nsa_w_cmp[l], nsa_pe_cmp[l])
        win = jnp.concatenate([state_win[l], jnp.stack([k_win, kv6[:, :, 5]], axis=2).astype(state_win.dtype)], axis=1)
        kwpos = PAST_LEN - win_len + jnp.arange(win_len + s)
        q_s = nq.reshape(bd, s, NSA_HEADS, HEAD_DIM)
        y_c = nsa_core(rope(q_s, pos_s), q_s, ng.reshape(bd, s, NSA_HEADS, 3), pos_s, kc, vc, cend, cmap,
                       ksel, vsel, win[:, :, 0], win[:, :, 1], kwpos)
        xs = xs + ms[2] * merge_branches(y_a, y_b, y_c, mg, w_branch[l], w_out[l], xs.dtype)
        xs = ffn_sublayer(xs, ms, norm_ffn[l], *moe_l)
        sb_s.append(new_sb)
        nsa_s.append(new_nsa)
        win_s.append(win[:, s:])
        ssm_s.append(jnp.stack([sr, si], axis=1))

    y_prompt = rms_norm(xp, final_norm)
    y_sample = rms_norm(xs, final_norm)
    return (y_prompt, y_sample,
            jnp.stack(sb_p), jnp.stack(sb_s),
            jnp.stack(nsa_p), jnp.stack(nsa_s),
            jnp.stack(win_p), jnp.stack(win_s),
            jnp.stack(ssm_p), jnp.stack(ssm_s))
```

```python
import functools
import math

import jax
import jax.numpy as jnp
import numpy as np
from jax import lax
from jax.experimental import pallas as pl
from jax.experimental.pallas import tpu as pltpu

F32 = jnp.float32
BF16 = jnp.bfloat16

D_MODEL = 1024
DEPTH = 4
PAST_LEN = 2048
HEAD_DIM = 64
SSM_WIDTH = 256
SSM_GROUP = 16
SSM_GROUPS = 16
SSM_STATE = 64
SB_HEADS = 4
SB_WIDTH = 256
NSA_HEADS = 8
NSA_KV_HEADS = 2
NSA_REP = 4
NSA_WIDTH = 512
NSA_KV_WIDTH = 128
CMP_LEN = 32
CMP_STRIDE = 16
SEL_LEN = 64
SEL_TOPK = 16
WINDOW = 512
N_BRANCH = 3
Q_BLOCK = 128
ROPE_THETA = 10000.0
MOE_GROUPS = 4
EXPERTS_PER_GROUP = 4
N_EXPERTS = 16
EXPERT_FF = 256
EPS = 1e-6
NEG = -1e30
FORCE = 1e9

LANES = 128
VMEM_LIMIT = 56 * 1024 * 1024

_C_U = 0
_C_SQ = 256
_C_SKV = 512
_C_NQ = 1024
_C_NKV = 1536
_C_NG = 2304
_C_MG = 2432
_C_END = 5504


def _cparams(sem):
    return pltpu.CompilerParams(dimension_semantics=sem, vmem_limit_bytes=VMEM_LIMIT)


def _rope_slab(v, cos, sin_signed):
    lane = lax.broadcasted_iota(jnp.int32, v.shape, 1)
    first = (lane % HEAD_DIM) < (HEAD_DIM // 2)
    swapped = jnp.where(first, pltpu.roll(v, LANES - HEAD_DIM // 2, 1), pltpu.roll(v, HEAD_DIM // 2, 1))
    return v * cos + swapped * sin_signed


def _inproj_kernel(x_ref, shift_ref, scale_ref, gn_ref, cos_ref, sin_ref, w_ref,
                   u_ref, sq_ref, skv_ref, qn_ref, qr_ref, nsa_ref, win_ref, ng_ref, g_ref):
    x = x_ref[...]
    ms = jnp.mean(x * x, axis=-1, keepdims=True)
    h = x * lax.rsqrt(ms + EPS) * gn_ref[...]
    h = h * (1.0 + scale_ref[0]) + shift_ref[0]
    hb = h.astype(BF16)

    def mm(lo, hi):
        return jnp.dot(hb, w_ref[:, lo:hi], preferred_element_type=F32)

    cos = cos_ref[...]
    sin = sin_ref[...]
    u_ref[...] = mm(_C_U, _C_SQ)
    sq_ref[...] = mm(_C_SQ, _C_SKV)
    skv_ref[...] = mm(_C_SKV, _C_NQ)
    q = mm(_C_NQ, _C_NKV) * (1.0 / math.sqrt(HEAD_DIM))
    qn_ref[...] = q
    for s in range(NSA_WIDTH // LANES):
        qr_ref[:, s * LANES:(s + 1) * LANES] = _rope_slab(q[:, s * LANES:(s + 1) * LANES], cos, sin)
    kv = mm(_C_NKV, _C_NG)
    nsa_ref[:, 0:256] = kv[:, 0:256]
    nsa_ref[:, 256:384] = _rope_slab(kv[:, 256:384], cos, sin)
    nsa_ref[:, 384:512] = kv[:, 384:512]
    win_ref[:, 0:128] = _rope_slab(kv[:, 512:640], cos, sin)
    win_ref[:, 128:256] = kv[:, 640:768]
    ng_ref[...] = jax.nn.sigmoid(mm(_C_NG, _C_MG))
    g_ref[...] = jax.nn.sigmoid(mm(_C_MG, _C_END))


def _inproj(x, shift, scale, gn, cos, sin, w, tm, mod_map, pos_map):
    n = x.shape[0]
    mrows = shift.shape[1]
    row = lambda width: pl.BlockSpec((tm, width), lambda i: (i, 0))
    widths = (256, 256, 512, 512, 512, 512, 256, 128, 3072)
    return pl.pallas_call(
        _inproj_kernel,
        grid=(n // tm,),
        in_specs=[
            row(D_MODEL),
            pl.BlockSpec((1, mrows, D_MODEL), lambda i: (mod_map(i), 0, 0)),
            pl.BlockSpec((1, mrows, D_MODEL), lambda i: (mod_map(i), 0, 0)),
            pl.BlockSpec((1, D_MODEL), lambda i: (0, 0)),
            pl.BlockSpec((tm, LANES), lambda i: (pos_map(i), 0)),
            pl.BlockSpec((tm, LANES), lambda i: (pos_map(i), 0)),
            pl.BlockSpec((D_MODEL, _C_END), lambda i: (0, 0)),
        ],
        out_specs=[row(wd) for wd in widths],
        out_shape=[jax.ShapeDtypeStruct((n, wd), F32) for wd in widths],
        compiler_params=_cparams(("parallel",)),
        name="inproj",
    )(x, shift, scale, gn, cos, sin, w)


def _merge_kernel(x_ref, gate_ref, ya_ref, yb_ref, yc_ref, g_ref, wb_ref, wo_ref, o_ref):
    def mm(a, w):
        return jnp.dot(a.astype(BF16), w, preferred_element_type=F32)

    br_a = mm(ya_ref[...], wb_ref[0:SSM_WIDTH, :])
    br_b = mm(yb_ref[...], wb_ref[SSM_WIDTH:SSM_WIDTH + SB_WIDTH, :])
    br_c = mm(yc_ref[...], wb_ref[SSM_WIDTH + SB_WIDTH:, :])
    merged = (g_ref[:, 0:D_MODEL] * br_a + g_ref[:, D_MODEL:2 * D_MODEL] * br_b
              + g_ref[:, 2 * D_MODEL:] * br_c)
    out = mm(merged, wo_ref[...])
    o_ref[...] = x_ref[...] + gate_ref[0] * out


def _merge(x, gate, ya, yb, yc, g, wb, wo, tm, mod_map):
    n = x.shape[0]
    mrows = gate.shape[1]
    row = lambda width: pl.BlockSpec((tm, width), lambda i: (i, 0))
    return pl.pallas_call(
        _merge_kernel,
        grid=(n // tm,),
        in_specs=[
            row(D_MODEL),
            pl.BlockSpec((1, mrows, D_MODEL), lambda i: (mod_map(i), 0, 0)),
            row(SSM_WIDTH), row(SB_WIDTH), row(NSA_WIDTH), row(N_BRANCH * D_MODEL),
            pl.BlockSpec((D_MODEL, D_MODEL), lambda i: (0, 0)),
            pl.BlockSpec((D_MODEL, D_MODEL), lambda i: (0, 0)),
        ],
        out_specs=row(D_MODEL),
        out_shape=jax.ShapeDtypeStruct((n, D_MODEL), F32),
        compiler_params=_cparams(("parallel",)),
        name="merge",
    )(x, gate, ya, yb, yc, g, wb, wo)


def _route(logits):
    lane = lax.broadcasted_iota(jnp.int32, logits.shape, 1)
    big = jnp.int32(1 << 20)
    is_g = lane < MOE_GROUPS
    gl = jnp.where(is_g, logits, NEG)
    gmax = jnp.max(gl, axis=-1, keepdims=True)
    p_sel = 1.0 / jnp.sum(jnp.where(is_g, jnp.exp(gl - gmax), 0.0), axis=-1, keepdims=True)
    g_sel = jnp.min(jnp.where(is_g & (gl == gmax), lane, big), axis=-1, keepdims=True)
    e_idx = lane - MOE_GROUPS
    in_grp = (e_idx >= g_sel * EXPERTS_PER_GROUP) & (e_idx < (g_sel + 1) * EXPERTS_PER_GROUP)
    el = jnp.where(in_grp, logits, NEG)
    m1 = jnp.max(el, axis=-1, keepdims=True)
    i1 = jnp.min(jnp.where(in_grp & (el == m1), lane, big), axis=-1, keepdims=True)
    el2 = jnp.where(lane == i1, NEG, el)
    m2 = jnp.max(el2, axis=-1, keepdims=True)
    i2 = jnp.min(jnp.where(in_grp & (lane != i1) & (el2 == m2), lane, big), axis=-1, keepdims=True)
    e2 = jnp.exp(m2 - m1)
    w1 = p_sel / (1.0 + e2)
    w2 = p_sel * e2 / (1.0 + e2)
    return jnp.where(lane == i1, w1, jnp.where(lane == i2, w2, 0.0))


def _ffn_kernel(x_ref, shift_ref, scale_ref, gate_ref, gn_ref, wr_ref, br_ref, wg_ref, wu_ref, wd_ref,
                o_ref, h_scr, comb_scr, acc_scr):
    e = pl.program_id(1)

    @pl.when(e == 0)
    def _():
        x = x_ref[...]
        ms = jnp.mean(x * x, axis=-1, keepdims=True)
        h = x * lax.rsqrt(ms + EPS) * gn_ref[...]
        h = h * (1.0 + scale_ref[0]) + shift_ref[0]
        logits = jnp.dot(h, wr_ref[...], preferred_element_type=F32,
                         precision=lax.Precision.HIGHEST) + br_ref[...]
        comb_scr[...] = _route(logits)
        h_scr[...] = h.astype(BF16)
        acc_scr[...] = jnp.zeros_like(acc_scr)

    hb = h_scr[...]
    a = jnp.dot(hb, wg_ref[0], preferred_element_type=F32)
    b = jnp.dot(hb, wu_ref[0], preferred_element_type=F32)
    comb = comb_scr[...]
    lane = lax.broadcasted_iota(jnp.int32, comb.shape, 1)
    c = jnp.sum(jnp.where(lane == e + MOE_GROUPS, comb, 0.0), axis=-1, keepdims=True)
    hid = (a * jax.nn.sigmoid(a)) * b * c
    acc_scr[...] += jnp.dot(hid.astype(BF16), wd_ref[0], preferred_element_type=F32)

    @pl.when(e == N_EXPERTS - 1)
    def _():
        o_ref[...] = x_ref[...] + gate_ref[0] * acc_scr[...]


def _ffn(x, shift, scale, gate, gn, wr, br, wg, wu, wd, tm, mod_map):
    n = x.shape[0]
    mrows = shift.shape[1]
    mod = pl.BlockSpec((1, mrows, D_MODEL), lambda i, e: (mod_map(i), 0, 0))
    return pl.pallas_call(
        _ffn_kernel,
        grid=(n // tm, N_EXPERTS),
        in_specs=[
            pl.BlockSpec((tm, D_MODEL), lambda i, e: (i, 0)),
            mod, mod, mod,
            pl.BlockSpec((1, D_MODEL), lambda i, e: (0, 0)),
            pl.BlockSpec((D_MODEL, LANES), lambda i, e: (0, 0)),
            pl.BlockSpec((1, LANES), lambda i, e: (0, 0)),
            pl.BlockSpec((1, D_MODEL, EXPERT_FF), lambda i, e: (e, 0, 0)),
            pl.BlockSpec((1, D_MODEL, EXPERT_FF), lambda i, e: (e, 0, 0)),
            pl.BlockSpec((1, EXPERT_FF, D_MODEL), lambda i, e: (e, 0, 0)),
        ],
        out_specs=pl.BlockSpec((tm, D_MODEL), lambda i, e: (i, 0)),
        out_shape=jax.ShapeDtypeStruct((n, D_MODEL), F32),
        scratch_shapes=[pltpu.VMEM((tm, D_MODEL), BF16), pltpu.VMEM((tm, LANES), F32),
                        pltpu.VMEM((tm, D_MODEL), F32)],
        compiler_params=_cparams(("parallel", "arbitrary")),
        name="ffn_moe",
    )(x, shift, scale, gate, gn, wr, br, wg, wu, wd)


def _final_norm_kernel(x_ref, g_ref, o_ref):
    x = x_ref[...]
    o_ref[...] = x * lax.rsqrt(jnp.mean(x * x, axis=-1, keepdims=True) + EPS) * g_ref[...]


def _final_norm(x, g, tm):
    n = x.shape[0]
    return pl.pallas_call(
        _final_norm_kernel,
        grid=(n // tm,),
        in_specs=[pl.BlockSpec((tm, D_MODEL), lambda i: (i, 0)), pl.BlockSpec((1, D_MODEL), lambda i: (0, 0))],
        out_specs=pl.BlockSpec((tm, D_MODEL), lambda i: (i, 0)),
        out_shape=jax.ShapeDtypeStruct((n, D_MODEL), F32),
        compiler_params=_cparams(("parallel",)),
        name="final_norm",
    )(x, g)


def _cplx_affine(e1, e2):
    a1r, a1i, b1r, b1i = e1
    a2r, a2i, b2r, b2i = e2
    return (a2r * a1r - a2i * a1i, a2r * a1i + a2i * a1r,
            a2r * b1r - a2i * b1i + b2r, a2r * b1i + a2i * b1r + b2i)


def _s5_branch(u, s0_re, s0_im, a_re, a_im, log_dt, b_re, b_im, c_re, c_im, d, w_glu, b_glu):
    bsz, t = u.shape[:2]
    dt = jnp.exp(log_dt)[:, None]
    mag = jnp.exp(a_re * dt)
    lb_re, lb_im = mag * jnp.cos(a_im * dt), mag * jnp.sin(a_im * dt)
    den = a_re * a_re + a_im * a_im
    co_re = ((lb_re - 1.0) * a_re + lb_im * a_im) / den
    co_im = (lb_im * a_re - (lb_re - 1.0) * a_im) / den
    bb_re = co_re[..., None] * b_re - co_im[..., None] * b_im
    bb_im = co_re[..., None] * b_im + co_im[..., None] * b_re
    ug = u.reshape(bsz, t, SSM_GROUPS, SSM_GROUP)
    bu_re = jnp.einsum('btgp,gnp->btgn', ug, bb_re)
    bu_im = jnp.einsum('btgp,gnp->btgn', ug, bb_im)
    bu_re = bu_re.at[:, 0].add(lb_re * s0_re - lb_im * s0_im)
    bu_im = bu_im.at[:, 0].add(lb_re * s0_im + lb_im * s0_re)
    la_re = jnp.broadcast_to(lb_re, bu_re.shape)
    la_im = jnp.broadcast_to(lb_im, bu_im.shape)
    _, _, s_re, s_im = lax.associative_scan(_cplx_affine, (la_re, la_im, bu_re, bu_im), axis=1)
    y = (jnp.einsum('btgn,gpn->btgp', s_re, c_re) - jnp.einsum('btgn,gpn->btgp', s_im, c_im) + d * ug)
    y = jax.nn.gelu(y.reshape(bsz, t, SSM_WIDTH))
    y = y * jax.nn.sigmoid(y @ w_glu + b_glu)
    return y, s_re[:, -1], s_im[:, -1]


def _sb_core(q, k, v, qpos, kpos):
    z = jnp.einsum('bqhd,bshd->bhqs', q, k) / math.sqrt(HEAD_DIM)
    mask = kpos[None, :] < qpos[:, None]
    log_1m = jnp.where(mask, jax.nn.log_sigmoid(-z), 0.0)
    between = lax.cumsum(log_1m, axis=3, reverse=True) - log_1m
    w = jnp.where(mask, jnp.exp(jax.nn.log_sigmoid(z) + between), 0.0)
    return jnp.einsum('bhqs,bshd->bqhd', w, v)


def _to_blocks(a):
    b, t = a.shape[:2]
    return a.reshape(b, t // Q_BLOCK, Q_BLOCK, *a.shape[2:]).swapaxes(0, 1)


def _from_blocks(o):
    nb, b = o.shape[:2]
    return o.swapaxes(0, 1).reshape(b, nb * Q_BLOCK, *o.shape[3:])


def _sb_prompt(q, k, v):
    bsz, t = q.shape[:2]
    kpos = jnp.arange(t)

    def block(args):
        qi, i = args
        return _sb_core(qi, k, v, i * Q_BLOCK + jnp.arange(Q_BLOCK), kpos)

    o = lax.map(block, (_to_blocks(q), jnp.arange(t // Q_BLOCK)))
    return _from_blocks(o).reshape(bsz, t, SB_WIDTH)


def _compress(rows, pe, w):
    bsz, length = rows.shape[:2]
    r = CMP_LEN // CMP_STRIDE
    n_chunks = length // CMP_STRIDE
    nc = n_chunks - r + 1
    chunks = rows[:, :n_chunks * CMP_STRIDE].reshape(bsz, n_chunks, CMP_STRIDE, NSA_KV_HEADS, HEAD_DIM)
    wj = w.reshape(r, CMP_STRIDE, HEAD_DIM, HEAD_DIM)
    pj = pe.reshape(r, CMP_STRIDE, HEAD_DIM)
    out = 0.0
    for j in range(r):
        out = out + jnp.einsum('bnsgd,sde->bnge', chunks[:, j:j + nc], wj[j]) + jnp.einsum('sd,sde->e', pj[j], wj[j])
    cend = jnp.arange(nc) * CMP_STRIDE + CMP_LEN - 1
    return out, cend


def _cmp_to_sel(nc, ns):
    cs = np.arange(nc) * CMP_STRIDE
    ss = np.arange(ns) * SEL_LEN
    ov = np.clip(np.minimum(cs[:, None] + CMP_LEN, ss[None, :] + SEL_LEN) - np.maximum(cs[:, None], ss[None, :]), 0, None)
    return jnp.asarray(ov / CMP_LEN, dtype=F32)


def _pad_blocks(a):
    pad = (-a.shape[1]) % SEL_LEN
    return jnp.pad(a, ((0, 0), (0, pad), (0, 0), (0, 0)))


def _nsa_prepare(k_cmp, v_cmp, k_sel, v_sel, w_cmp, pe_cmp):
    kc, cend = _compress(k_cmp, pe_cmp[0], w_cmp[0])
    vc, _ = _compress(v_cmp, pe_cmp[1], w_cmp[1])
    ks, vs = _pad_blocks(k_sel), _pad_blocks(v_sel)
    cmap = _cmp_to_sel(kc.shape[1], ks.shape[1] // SEL_LEN)
    return kc, vc, cend, cmap, ks, vs


def _nsa_core(qr, qn, g, qpos, kc, vc, cend, cmap, ks, vs, kw, vw, kwpos):
    bsz, nq = qr.shape[:2]
    qr = qr.reshape(bsz, nq, NSA_KV_HEADS, NSA_REP, HEAD_DIM)
    qn = qn.reshape(bsz, nq, NSA_KV_HEADS, NSA_REP, HEAD_DIM)
    mc = cend[None, :] <= qpos[:, None]
    sc = jnp.einsum('bqgrd,bcgd->bgrqc', qn, kc)
    pc = jax.nn.softmax(jnp.where(mc, sc, NEG), axis=-1) * mc
    o_cmp = jnp.einsum('bgrqc,bcgd->bqgrd', pc, vc)
    ns = cmap.shape[1]
    topk = min(SEL_TOPK, ns)
    imp = jnp.einsum('bgrqc,cn->bgqn', pc, cmap)
    blk = jnp.arange(ns)[None, :]
    cur = (qpos // SEL_LEN)[:, None]
    imp = jnp.where(blk == cur, FORCE, jnp.where(blk < cur, imp, -1.0))
    _, idx = lax.top_k(imp, topk)
    ksb = ks.reshape(bsz, ns, SEL_LEN, NSA_KV_HEADS, HEAD_DIM).transpose(0, 3, 1, 2, 4)
    vsb = vs.reshape(bsz, ns, SEL_LEN, NSA_KV_HEADS, HEAD_DIM).transpose(0, 3, 1, 2, 4)
    bi = jnp.arange(bsz)[:, None, None, None]
    gi = jnp.arange(NSA_KV_HEADS)[None, :, None, None]
    kg, vg = ksb[bi, gi, idx], vsb[bi, gi, idx]
    spos = idx[..., None] * SEL_LEN + jnp.arange(SEL_LEN)
    ms = spos <= qpos[None, None, :, None, None]
    ss = jnp.einsum('bqgrd,bgqkld->bgrqkl', qr, kg)
    ss = jnp.where(ms[:, :, None], ss, NEG).reshape(bsz, NSA_KV_HEADS, NSA_REP, nq, topk * SEL_LEN)
    ps = jax.nn.softmax(ss, axis=-1).reshape(bsz, NSA_KV_HEADS, NSA_REP, nq, topk, SEL_LEN)
    o_sel = jnp.einsum('bgrqkl,bgqkld->bqgrd', ps, vg)
    mw = (kwpos[None, :] <= qpos[:, None]) & (kwpos[None, :] > qpos[:, None] - WINDOW) & (kwpos[None, :] >= 0)
    sw = jnp.einsum('bqgrd,bwgd->bgrqw', qr, kw)
    pw = jax.nn.softmax(jnp.where(mw, sw, NEG), axis=-1)
    o_win = jnp.einsum('bgrqw,bwgd->bqgrd', pw, vw)
    g = g.reshape(bsz, nq, NSA_KV_HEADS, NSA_REP, 3)
    o = g[..., 0:1] * o_cmp + g[..., 1:2] * o_sel + g[..., 2:3] * o_win
    return o.reshape(bsz, nq, NSA_WIDTH)


def _nsa_prompt(qr, qn, gates, k_cmp, v_cmp, k_sel, v_sel, k_win, v_win, w_cmp, pe_cmp):
    t = qr.shape[1]
    kc, vc, cend, cmap, ks, vs = _nsa_prepare(k_cmp, v_cmp, k_sel, v_sel, w_cmp, pe_cmp)
    pad = ((0, 0), (WINDOW, 0), (0, 0), (0, 0))
    kwp, vwp = jnp.pad(k_win, pad), jnp.pad(v_win, pad)

    def block(args):
        qri, qni, g, i = args
        start = i * Q_BLOCK
        qpos = start + jnp.arange(Q_BLOCK)
        kw = lax.dynamic_slice_in_dim(kwp, start, WINDOW + Q_BLOCK, axis=1)
        vw = lax.dynamic_slice_in_dim(vwp, start, WINDOW + Q_BLOCK, axis=1)
        kwpos = start - WINDOW + jnp.arange(WINDOW + Q_BLOCK)
        return _nsa_core(qri, qni, g, qpos, kc, vc, cend, cmap, ks, vs, kw, vw, kwpos)

    o = lax.map(block, (_to_blocks(qr), _to_blocks(qn), _to_blocks(gates), jnp.arange(t // Q_BLOCK)))
    return _from_blocks(o)


def _rope_tables(pos):
    half = HEAD_DIM // 2
    freqs = ROPE_THETA ** (-jnp.arange(half, dtype=F32) / half)
    ang = pos.astype(F32)[:, None] * freqs
    cos, sin = jnp.cos(ang), jnp.sin(ang)
    cos2 = jnp.concatenate([cos, cos], axis=-1)
    sin2 = jnp.concatenate([-sin, sin], axis=-1)
    return jnp.tile(cos2, (1, LANES // HEAD_DIM)), jnp.tile(sin2, (1, LANES // HEAD_DIM))


def _pack_w_in(w):
    pad = jnp.zeros((D_MODEL, LANES - 3 * NSA_HEADS), w.dtype)
    return jnp.concatenate([w[:, :2304], w[:, 2304:2328], pad, w[:, 2328:]], axis=1).astype(BF16)


def kernel(x_prompt, x_sample, cache_sb, cache_nsa, state_win, state_ssm, page_table, c_prompt, c_sample,
           w_ada, b_ada, norm_mix, norm_ffn, w_in, ssm_a_re, ssm_a_im, ssm_log_dt, ssm_b_re, ssm_b_im,
           ssm_c_re, ssm_c_im, ssm_d, w_glu, b_glu, nsa_w_cmp, nsa_pe_cmp, w_branch, w_out,
           w_grp, b_grp, w_rt, b_rt, w_e_gate, w_e_up, w_e_down, final_norm):
    bp, t = x_prompt.shape[:2]
    bd, s = x_sample.shape[:2]
    np_tok = bp * t
    win_len = state_win.shape[2]
    tm_p = 256
    tiles_pb = t // tm_p
    tm_f = 1024
    tiles_fb = t // tm_f

    xp = x_prompt.reshape(np_tok, D_MODEL)
    xs = x_sample.reshape(bd * s, D_MODEL)
    cos_p, sin_p = _rope_tables(jnp.arange(t))
    cos_s, sin_s = _rope_tables(jnp.full((bd,), PAST_LEN))
    pos_s = PAST_LEN + jnp.arange(s)

    sb_p, sb_s, nsa_p, nsa_s, win_p, win_s, ssm_p, ssm_s = [], [], [], [], [], [], [], []
    for l in range(DEPTH):
        w_in_l = _pack_w_in(w_in[l])
        wb_l = w_branch[l].astype(BF16)
        wo_l = w_out[l].astype(BF16)
        wr_l = jnp.concatenate([w_grp[l], w_rt[l], jnp.zeros((D_MODEL, LANES - 20), F32)], axis=1)
        br_l = jnp.concatenate([b_grp[l], b_rt[l], jnp.zeros((LANES - 20,), F32)])[None, :]
        wg_l, wu_l, wd_l = w_e_gate[l].astype(BF16), w_e_up[l].astype(BF16), w_e_down[l].astype(BF16)
        gn_mix = norm_mix[l][None, :]
        gn_ffn = norm_ffn[l][None, :]
        ssm_l = (ssm_a_re[l], ssm_a_im[l], ssm_log_dt[l], ssm_b_re[l], ssm_b_im[l], ssm_c_re[l], ssm_c_im[l],
                 ssm_d[l], w_glu[l], b_glu[l])

        mp = jnp.split(jax.nn.silu(c_prompt) @ w_ada[l] + b_ada[l], 6, axis=-1)
        mp = [m[:, None, :] for m in mp]
        u, sq, skv, qn, qr, nsa, win, ng, g = _inproj(
            xp, mp[0], mp[1], gn_mix, cos_p, sin_p, w_in_l, tm_p,
            lambda i: i // tiles_pb, lambda i: i % tiles_pb)
        s0 = jnp.zeros((bp, SSM_GROUPS, SSM_STATE), F32)
        y_a, sr, si = _s5_branch(u.reshape(bp, t, SSM_WIDTH), s0, s0, *ssm_l)
        skv5 = skv.reshape(bp, t, 2, SB_HEADS, HEAD_DIM)
        y_b = _sb_prompt(sq.reshape(bp, t, SB_HEADS, HEAD_DIM), skv5[:, :, 0], skv5[:, :, 1])
        nsa5 = nsa.reshape(bp, t, 4, NSA_KV_HEADS, HEAD_DIM)
        win5 = win.reshape(bp, t, 2, NSA_KV_HEADS, HEAD_DIM)
        gates = ng[:, :3 * NSA_HEADS].reshape(bp, t, NSA_HEADS, 3)
        y_c = _nsa_prompt(qr.reshape(bp, t, NSA_HEADS, HEAD_DIM), qn.reshape(bp, t, NSA_HEADS, HEAD_DIM), gates,
                          nsa5[:, :, 0], nsa5[:, :, 1], nsa5[:, :, 2], nsa5[:, :, 3], win5[:, :, 0], win5[:, :, 1],
                          nsa_w_cmp[l], nsa_pe_cmp[l])
        xp = _merge(xp, mp[2], y_a.reshape(np_tok, SSM_WIDTH), y_b.reshape(np_tok, SB_WIDTH),
                    y_c.reshape(np_tok, NSA_WIDTH), g, wb_l, wo_l, tm_p, lambda i: i // tiles_pb)
        xp = _ffn(xp, mp[3], mp[4], mp[5], gn_ffn, wr_l, br_l, wg_l, wu_l, wd_l, tm_f, lambda i: i // tiles_fb)
        sb_p.append(skv5)
        nsa_p.append(nsa5)
        win_p.append(win5[:, t - min(WINDOW, t):])
        ssm_p.append(jnp.stack([sr, si], axis=1))

        ms = jnp.split(jax.nn.silu(c_sample) @ w_ada[l] + b_ada[l], 6, axis=-1)
        ms = [m[None, :, :] for m in ms]
        u, sq, skv, qn, qr, nsa, win, ng, g = _inproj(
            xs, ms[0], ms[1], gn_mix, cos_s, sin_s, w_in_l, bd, lambda i: 0, lambda i: 0)
        y_a, sr, si = _s5_branch(u.reshape(bd, s, SSM_WIDTH), state_ssm[l][:, 0], state_ssm[l][:, 1], *ssm_l)
        new_sb = skv.reshape(bd, s, 2, SB_HEADS, HEAD_DIM)
        past_sb = cache_sb[l, page_table].reshape(bd, PAST_LEN, 2, SB_HEADS, HEAD_DIM)
        kv_sb = jnp.concatenate([past_sb, new_sb], axis=1)
        y_b = _sb_core(sq.reshape(bd, s, SB_HEADS, HEAD_DIM), kv_sb[:, :, 0], kv_sb[:, :, 1],
                       pos_s, jnp.arange(PAST_LEN + s)).reshape(bd, s, SB_WIDTH)
        new_nsa = nsa.reshape(bd, s, 4, NSA_KV_HEADS, HEAD_DIM)
        past_nsa = cache_nsa[l, page_table].reshape(bd, PAST_LEN, 4, NSA_KV_HEADS, HEAD_DIM)
        kv4 = jnp.concatenate([past_nsa, new_nsa], axis=1)
        kc, vc, cend, cmap, ksel, vsel = _nsa_prepare(kv4[:, :, 0], kv4[:, :, 1], kv4[:, :, 2], kv4[:, :, 3],
                                                      nsa_w_cmp[l], nsa_pe_cmp[l])
        new_win = win.reshape(bd, s, 2, NSA_KV_HEADS, HEAD_DIM)
        winc = jnp.concatenate([state_win[l], new_win], axis=1)
        kwpos = PAST_LEN - win_len + jnp.arange(win_len + s)
        gates = ng[:, :3 * NSA_HEADS].reshape(bd, s, NSA_HEADS, 3)
        y_c = _nsa_core(qr.reshape(bd, s, NSA_HEADS, HEAD_DIM), qn.reshape(bd, s, NSA_HEADS, HEAD_DIM), gates,
                        pos_s, kc, vc, cend, cmap, ksel, vsel, winc[:, :, 0], winc[:, :, 1], kwpos)
        xs = _merge(xs, ms[2], y_a.reshape(bd * s, SSM_WIDTH), y_b.reshape(bd * s, SB_WIDTH),
                    y_c.reshape(bd * s, NSA_WIDTH), g, wb_l, wo_l, bd, lambda i: 0)
        xs = _ffn(xs, ms[3], ms[4], ms[5], gn_ffn, wr_l, br_l, wg_l, wu_l, wd_l, bd, lambda i: 0)
        sb_s.append(new_sb)
        nsa_s.append(new_nsa)
        win_s.append(winc[:, s:])
        ssm_s.append(jnp.stack([sr, si], axis=1))

    fn = final_norm[None, :]
    y_prompt = _final_norm(xp, fn, tm_f).reshape(bp, t, D_MODEL)
    y_sample = _final_norm(xs, fn, bd).reshape(bd, s, D_MODEL)
    return (y_prompt, y_sample,
            jnp.stack(sb_p), jnp.stack(sb_s),
            jnp.stack(nsa_p), jnp.stack(nsa_s),
            jnp.stack(win_p), jnp.stack(win_s),
            jnp.stack(ssm_p), jnp.stack(ssm_s))
```

```python
import functools
import math

import jax
import jax.numpy as jnp
import numpy as np
from jax import lax
from jax.experimental import pallas as pl
from jax.experimental.pallas import tpu as pltpu

F32 = jnp.float32
BF16 = jnp.bfloat16

D_MODEL = 1024
DEPTH = 4
PAST_LEN = 2048
HEAD_DIM = 64
SSM_WIDTH = 256
SSM_GROUP = 16
SSM_GROUPS = 16
SSM_STATE = 64
SB_HEADS = 4
SB_WIDTH = 256
NSA_HEADS = 8
NSA_KV_HEADS = 2
NSA_REP = 4
NSA_WIDTH = 512
NSA_KV_WIDTH = 128
CMP_LEN = 32
CMP_STRIDE = 16
SEL_LEN = 64
SEL_TOPK = 16
WINDOW = 512
N_BRANCH = 3
Q_BLOCK = 128
ROPE_THETA = 10000.0
MOE_GROUPS = 4
EXPERTS_PER_GROUP = 4
N_EXPERTS = 16
EXPERT_FF = 256
EPS = 1e-6
NEG = -1e30
FORCE = 1e9

LANES = 128
VMEM_LIMIT = 56 * 1024 * 1024

_C_U = 0
_C_SQ = 256
_C_SKV = 512
_C_NQ = 1024
_C_NKV = 1536
_C_NG = 2304
_C_MG = 2432
_C_END = 5504


def _cparams(sem):
    return pltpu.CompilerParams(dimension_semantics=sem, vmem_limit_bytes=VMEM_LIMIT)


def _rope_slab(v, cos, sin_signed):
    lane = lax.broadcasted_iota(jnp.int32, v.shape, 1)
    first = (lane % HEAD_DIM) < (HEAD_DIM // 2)
    swapped = jnp.where(first, pltpu.roll(v, LANES - HEAD_DIM // 2, 1), pltpu.roll(v, HEAD_DIM // 2, 1))
    return v * cos + swapped * sin_signed


def _inproj_kernel(x_ref, shift_ref, scale_ref, gn_ref, cos_ref, sin_ref, w_ref,
                   u_ref, sq_ref, skv_ref, qn_ref, qr_ref, nsa_ref, win_ref, ng_ref, g_ref):
    x = x_ref[...]
    ms = jnp.mean(x * x, axis=-1, keepdims=True)
    h = x * lax.rsqrt(ms + EPS) * gn_ref[...]
    h = h * (1.0 + scale_ref[0]) + shift_ref[0]
    hb = h.astype(BF16)

    def mm(lo, hi):
        return jnp.dot(hb, w_ref[:, lo:hi], preferred_element_type=F32)

    cos = cos_ref[...]
    sin = sin_ref[...]
    u_ref[...] = mm(_C_U, _C_SQ)
    sq_ref[...] = mm(_C_SQ, _C_SKV)
    skv_ref[...] = mm(_C_SKV, _C_NQ)
    q = mm(_C_NQ, _C_NKV) * (1.0 / math.sqrt(HEAD_DIM))
    qn_ref[...] = q
    for s in range(NSA_WIDTH // LANES):
        qr_ref[:, s * LANES:(s + 1) * LANES] = _rope_slab(q[:, s * LANES:(s + 1) * LANES], cos, sin)
    kv = mm(_C_NKV, _C_NG)
    nsa_ref[:, 0:256] = kv[:, 0:256]
    nsa_ref[:, 256:384] = _rope_slab(kv[:, 256:384], cos, sin)
    nsa_ref[:, 384:512] = kv[:, 384:512]
    win_ref[:, 0:128] = _rope_slab(kv[:, 512:640], cos, sin)
    win_ref[:, 128:256] = kv[:, 640:768]
    ng_ref[...] = jax.nn.sigmoid(mm(_C_NG, _C_MG))
    g_ref[...] = jax.nn.sigmoid(mm(_C_MG, _C_END))


def _inproj(x, shift, scale, gn, cos, sin, w, tm, mod_map, pos_map):
    n = x.shape[0]
    mrows = shift.shape[1]
    row = lambda width: pl.BlockSpec((tm, width), lambda i: (i, 0))
    widths = (256, 256, 512, 512, 512, 512, 256, 128, 3072)
    return pl.pallas_call(
        _inproj_kernel,
        grid=(n // tm,),
        in_specs=[
            row(D_MODEL),
            pl.BlockSpec((1, mrows, D_MODEL), lambda i: (mod_map(i), 0, 0)),
            pl.BlockSpec((1, mrows, D_MODEL), lambda i: (mod_map(i), 0, 0)),
            pl.BlockSpec((1, D_MODEL), lambda i: (0, 0)),
            pl.BlockSpec((tm, LANES), lambda i: (pos_map(i), 0)),
            pl.BlockSpec((tm, LANES), lambda i: (pos_map(i), 0)),
            pl.BlockSpec((D_MODEL, _C_END), lambda i: (0, 0)),
        ],
        out_specs=[row(wd) for wd in widths],
        out_shape=[jax.ShapeDtypeStruct((n, wd), F32) for wd in widths],
        compiler_params=_cparams(("parallel",)),
        name="inproj",
    )(x, shift, scale, gn, cos, sin, w)


def _merge_kernel(x_ref, gate_ref, ya_ref, yb_ref, yc_ref, g_ref, wglu_ref, bglu_ref, wb_ref, wo_ref, o_ref):
    def mm(a, w):
        return jnp.dot(a.astype(BF16), w, preferred_element_type=F32)

    ya = jax.nn.gelu(ya_ref[...])
    ya = ya * jax.nn.sigmoid(mm(ya, wglu_ref[...]) + bglu_ref[...])
    br_a = mm(ya, wb_ref[0:SSM_WIDTH, :])
    br_b = mm(yb_ref[...], wb_ref[SSM_WIDTH:SSM_WIDTH + SB_WIDTH, :])
    br_c = mm(yc_ref[...], wb_ref[SSM_WIDTH + SB_WIDTH:, :])
    merged = (g_ref[:, 0:D_MODEL] * br_a + g_ref[:, D_MODEL:2 * D_MODEL] * br_b
              + g_ref[:, 2 * D_MODEL:] * br_c)
    out = mm(merged, wo_ref[...])
    o_ref[...] = x_ref[...] + gate_ref[0] * out


def _merge(x, gate, ya, yb, yc, g, wglu, bglu, wb, wo, tm, mod_map):
    n = x.shape[0]
    mrows = gate.shape[1]
    row = lambda width: pl.BlockSpec((tm, width), lambda i: (i, 0))
    return pl.pallas_call(
        _merge_kernel,
        grid=(n // tm,),
        in_specs=[
            row(D_MODEL),
            pl.BlockSpec((1, mrows, D_MODEL), lambda i: (mod_map(i), 0, 0)),
            row(SSM_WIDTH), row(SB_WIDTH), row(NSA_WIDTH), row(N_BRANCH * D_MODEL),
            pl.BlockSpec((SSM_WIDTH, SSM_WIDTH), lambda i: (0, 0)),
            pl.BlockSpec((1, SSM_WIDTH), lambda i: (0, 0)),
            pl.BlockSpec((D_MODEL, D_MODEL), lambda i: (0, 0)),
            pl.BlockSpec((D_MODEL, D_MODEL), lambda i: (0, 0)),
        ],
        out_specs=row(D_MODEL),
        out_shape=jax.ShapeDtypeStruct((n, D_MODEL), F32),
        compiler_params=_cparams(("parallel",)),
        name="merge",
    )(x, gate, ya, yb, yc, g, wglu, bglu, wb, wo)


def _route(logits):
    lane = lax.broadcasted_iota(jnp.int32, logits.shape, 1)
    big = jnp.int32(1 << 20)
    is_g = lane < MOE_GROUPS
    gl = jnp.where(is_g, logits, NEG)
    gmax = jnp.max(gl, axis=-1, keepdims=True)
    p_sel = 1.0 / jnp.sum(jnp.where(is_g, jnp.exp(gl - gmax), 0.0), axis=-1, keepdims=True)
    g_sel = jnp.min(jnp.where(is_g & (gl == gmax), lane, big), axis=-1, keepdims=True)
    e_idx = lane - MOE_GROUPS
    in_grp = (e_idx >= g_sel * EXPERTS_PER_GROUP) & (e_idx < (g_sel + 1) * EXPERTS_PER_GROUP)
    el = jnp.where(in_grp, logits, NEG)
    m1 = jnp.max(el, axis=-1, keepdims=True)
    i1 = jnp.min(jnp.where(in_grp & (el == m1), lane, big), axis=-1, keepdims=True)
    el2 = jnp.where(lane == i1, NEG, el)
    m2 = jnp.max(el2, axis=-1, keepdims=True)
    i2 = jnp.min(jnp.where(in_grp & (lane != i1) & (el2 == m2), lane, big), axis=-1, keepdims=True)
    e2 = jnp.exp(m2 - m1)
    w1 = p_sel / (1.0 + e2)
    w2 = p_sel * e2 / (1.0 + e2)
    return jnp.where(lane == i1, w1, jnp.where(lane == i2, w2, 0.0))


def _ffn_kernel(x_ref, shift_ref, scale_ref, gate_ref, gn_ref, wr_ref, br_ref, wg_ref, wu_ref, wd_ref,
                o_ref, h_scr, comb_scr, acc_scr):
    e = pl.program_id(1)

    @pl.when(e == 0)
    def _():
        x = x_ref[...]
        ms = jnp.mean(x * x, axis=-1, keepdims=True)
        h = x * lax.rsqrt(ms + EPS) * gn_ref[...]
        h = h * (1.0 + scale_ref[0]) + shift_ref[0]
        logits = jnp.dot(h, wr_ref[...], preferred_element_type=F32,
                         precision=lax.Precision.HIGHEST) + br_ref[...]
        comb_scr[...] = _route(logits)
        h_scr[...] = h.astype(BF16)
        acc_scr[...] = jnp.zeros_like(acc_scr)

    hb = h_scr[...]
    a = jnp.dot(hb, wg_ref[0], preferred_element_type=F32)
    b = jnp.dot(hb, wu_ref[0], preferred_element_type=F32)
    comb = comb_scr[...]
    lane = lax.broadcasted_iota(jnp.int32, comb.shape, 1)
    c = jnp.sum(jnp.where(lane == e + MOE_GROUPS, comb, 0.0), axis=-1, keepdims=True)
    hid = (a * jax.nn.sigmoid(a)) * b * c
    acc_scr[...] += jnp.dot(hid.astype(BF16), wd_ref[0], preferred_element_type=F32)

    @pl.when(e == N_EXPERTS - 1)
    def _():
        o_ref[...] = x_ref[...] + gate_ref[0] * acc_scr[...]


def _ffn(x, shift, scale, gate, gn, wr, br, wg, wu, wd, tm, mod_map):
    n = x.shape[0]
    mrows = shift.shape[1]
    mod = pl.BlockSpec((1, mrows, D_MODEL), lambda i, e: (mod_map(i), 0, 0))
    return pl.pallas_call(
        _ffn_kernel,
        grid=(n // tm, N_EXPERTS),
        in_specs=[
            pl.BlockSpec((tm, D_MODEL), lambda i, e: (i, 0)),
            mod, mod, mod,
            pl.BlockSpec((1, D_MODEL), lambda i, e: (0, 0)),
            pl.BlockSpec((D_MODEL, LANES), lambda i, e: (0, 0)),
            pl.BlockSpec((1, LANES), lambda i, e: (0, 0)),
            pl.BlockSpec((1, D_MODEL, EXPERT_FF), lambda i, e: (e, 0, 0)),
            pl.BlockSpec((1, D_MODEL, EXPERT_FF), lambda i, e: (e, 0, 0)),
            pl.BlockSpec((1, EXPERT_FF, D_MODEL), lambda i, e: (e, 0, 0)),
        ],
        out_specs=pl.BlockSpec((tm, D_MODEL), lambda i, e: (i, 0)),
        out_shape=jax.ShapeDtypeStruct((n, D_MODEL), F32),
        scratch_shapes=[pltpu.VMEM((tm, D_MODEL), BF16), pltpu.VMEM((tm, LANES), F32),
                        pltpu.VMEM((tm, D_MODEL), F32)],
        compiler_params=_cparams(("parallel", "arbitrary")),
        name="ffn_moe",
    )(x, shift, scale, gate, gn, wr, br, wg, wu, wd)


def _final_norm_kernel(x_ref, g_ref, o_ref):
    x = x_ref[...]
    o_ref[...] = x * lax.rsqrt(jnp.mean(x * x, axis=-1, keepdims=True) + EPS) * g_ref[...]


def _final_norm(x, g, tm):
    n = x.shape[0]
    return pl.pallas_call(
        _final_norm_kernel,
        grid=(n // tm,),
        in_specs=[pl.BlockSpec((tm, D_MODEL), lambda i: (i, 0)), pl.BlockSpec((1, D_MODEL), lambda i: (0, 0))],
        out_specs=pl.BlockSpec((tm, D_MODEL), lambda i: (i, 0)),
        out_shape=jax.ShapeDtypeStruct((n, D_MODEL), F32),
        compiler_params=_cparams(("parallel",)),
        name="final_norm",
    )(x, g)


S5_CHUNK = 64
S5_LW = S5_CHUNK * SSM_GROUP


def _s5_mats(a_re, a_im, log_dt, b_re, b_im, c_re, c_im):
    L = S5_CHUNK
    hp = lax.Precision.HIGHEST
    dt = jnp.exp(log_dt)[:, None]
    k = jnp.arange(L + 1, dtype=F32)[:, None, None]
    mag = jnp.exp(a_re * dt * k)
    ang = a_im * dt * k
    lk_re, lk_im = mag * jnp.cos(ang), mag * jnp.sin(ang)
    lb_re, lb_im = lk_re[1], lk_im[1]
    den = a_re * a_re + a_im * a_im
    co_re = ((lb_re - 1.0) * a_re + lb_im * a_im) / den
    co_im = (lb_im * a_re - (lb_re - 1.0) * a_im) / den
    bb_re = co_re[..., None] * b_re - co_im[..., None] * b_im
    bb_im = co_re[..., None] * b_im + co_im[..., None] * b_re
    cl_re = c_re[None] * lk_re[:, :, None, :] - c_im[None] * lk_im[:, :, None, :]
    cl_im = c_re[None] * lk_im[:, :, None, :] + c_im[None] * lk_re[:, :, None, :]
    kk = (jnp.einsum('kgpn,gnq->kgpq', cl_re[:L], bb_re, precision=hp)
          - jnp.einsum('kgpn,gnq->kgpq', cl_im[:L], bb_im, precision=hp))
    idx = jnp.arange(L)[None, :] - jnp.arange(L)[:, None]
    tm = jnp.where((idx >= 0)[:, :, None, None, None], kk[jnp.clip(idx, 0, L - 1)], 0.0)
    tmat = tm.transpose(2, 0, 4, 1, 3).reshape(SSM_GROUPS, S5_LW, S5_LW)
    rev = L - 1 - jnp.arange(L)
    be_re = lk_re[rev][:, :, :, None] * bb_re[None] - lk_im[rev][:, :, :, None] * bb_im[None]
    be_im = lk_re[rev][:, :, :, None] * bb_im[None] + lk_im[rev][:, :, :, None] * bb_re[None]
    bmat = jnp.concatenate([be_re.transpose(1, 0, 3, 2).reshape(SSM_GROUPS, S5_LW, SSM_STATE),
                            be_im.transpose(1, 0, 3, 2).reshape(SSM_GROUPS, S5_LW, SSM_STATE)], axis=-1)
    cm_re = cl_re[1:].transpose(1, 3, 0, 2).reshape(SSM_GROUPS, SSM_STATE, S5_LW)
    cm_im = cl_im[1:].transpose(1, 3, 0, 2).reshape(SSM_GROUPS, SSM_STATE, S5_LW)
    cmat = jnp.concatenate([cm_re, -cm_im], axis=1)
    lbl = jnp.stack([lk_re[L], lk_im[L]], axis=1)
    return tmat.astype(BF16), bmat.astype(BF16), cmat.astype(BF16), lbl


def _s5_kernel(x_ref, t_ref, b_ref, c_ref, lbl_ref, d_ref, y_ref, sf_ref, er_scr, ei_scr, pr_scr, pi_scr):
    nb, _, nch, _ = x_ref.shape
    a_re = lbl_ref[0, 0:1, :]
    a_im = lbl_ref[0, 1:2, :]
    for b in range(nb):
        x = x_ref[b, 0]
        xb = x.astype(BF16)
        e = jnp.dot(xb, b_ref[0], preferred_element_type=F32)
        er_scr[...] = e[:, :SSM_STATE]
        ei_scr[...] = e[:, SSM_STATE:]

        def step(c, carry):
            s_re, s_im = carry
            pr_scr[pl.ds(c, 1), :] = s_re
            pi_scr[pl.ds(c, 1), :] = s_im
            n_re = a_re * s_re - a_im * s_im + er_scr[pl.ds(c, 1), :]
            n_im = a_re * s_im + a_im * s_re + ei_scr[pl.ds(c, 1), :]
            return n_re, n_im

        zero = jnp.zeros((1, SSM_STATE), F32)
        s_re, s_im = lax.fori_loop(0, nch, step, (zero, zero))
        sprev = jnp.concatenate([pr_scr[...], pi_scr[...]], axis=1).astype(BF16)
        y = (jnp.dot(xb, t_ref[0], preferred_element_type=F32)
             + jnp.dot(sprev, c_ref[0], preferred_element_type=F32) + d_ref[0] * x)
        y_ref[b, 0] = y
        sf_ref[b, 0] = jnp.concatenate([s_re, s_im], axis=1)


def _s5_prompt(u, mats, d):
    tmat, bmat, cmat, lbl = mats
    bsz, t = u.shape[:2]
    nch = t // S5_CHUNK
    x = u.reshape(bsz, nch, S5_CHUNK, SSM_GROUPS, SSM_GROUP).transpose(0, 3, 1, 2, 4).reshape(
        bsz, SSM_GROUPS, nch, S5_LW)
    dt = jnp.tile(d, (1, S5_CHUNK))[:, None, :]
    y, sf = pl.pallas_call(
        _s5_kernel,
        grid=(SSM_GROUPS,),
        in_specs=[
            pl.BlockSpec((bsz, 1, nch, S5_LW), lambda g: (0, g, 0, 0)),
            pl.BlockSpec((1, S5_LW, S5_LW), lambda g: (g, 0, 0)),
            pl.BlockSpec((1, S5_LW, 2 * SSM_STATE), lambda g: (g, 0, 0)),
            pl.BlockSpec((1, 2 * SSM_STATE, S5_LW), lambda g: (g, 0, 0)),
            pl.BlockSpec((1, 2, SSM_STATE), lambda g: (g, 0, 0)),
            pl.BlockSpec((1, 1, S5_LW), lambda g: (g, 0, 0)),
        ],
        out_specs=[pl.BlockSpec((bsz, 1, nch, S5_LW), lambda g: (0, g, 0, 0)),
                   pl.BlockSpec((bsz, 1, 1, 2 * SSM_STATE), lambda g: (0, g, 0, 0))],
        out_shape=[jax.ShapeDtypeStruct((bsz, SSM_GROUPS, nch, S5_LW), F32),
                   jax.ShapeDtypeStruct((bsz, SSM_GROUPS, 1, 2 * SSM_STATE), F32)],
        scratch_shapes=[pltpu.VMEM((nch, SSM_STATE), F32)] * 4,
        compiler_params=_cparams(("arbitrary",)),
        name="s5_prompt",
    )(x, tmat, bmat, cmat, lbl, dt)
    y = y.reshape(bsz, SSM_GROUPS, nch, S5_CHUNK, SSM_GROUP).transpose(0, 2, 3, 1, 4).reshape(bsz, t, SSM_WIDTH)
    sf = sf.reshape(bsz, SSM_GROUPS, 2, SSM_STATE).transpose(0, 2, 1, 3)
    return y, sf


SB_TK = 128


def _sb_tri():
    s = np.arange(SB_TK)[:, None]
    j = np.arange(2 * SB_TK)[None, :]
    u = -((s > j) | (j >= SB_TK)).astype(np.float32)
    return jnp.asarray(np.concatenate([u, u], axis=0), dtype=BF16)


def _sb_kernel(q_ref, kt_ref, v_ref, tri_ref, o_ref, acc_scr, c_scr):
    qi = pl.program_id(1)
    tq = q_ref.shape[0]
    row = lax.broadcasted_iota(jnp.int32, (tq, SB_TK), 0)
    col = lax.broadcasted_iota(jnp.int32, (tq, SB_TK), 1)
    dmask = col < row
    tri = tri_ref[...]
    for h in range(SB_HEADS):
        qh = (q_ref[:, h * HEAD_DIM:(h + 1) * HEAD_DIM] * (1.0 / math.sqrt(HEAD_DIM))).astype(BF16)

        def tile(j, masked, h=h, qh=qh):
            z = jnp.dot(qh, kt_ref[0, h, j], preferred_element_type=F32)
            sp = jnp.maximum(z, 0.0) + jnp.log(1.0 + jnp.exp(-jnp.abs(z)))
            if masked:
                sp = jnp.where(dmask, sp, 0.0)
            hi = sp.astype(BF16)
            lo = (sp - hi.astype(F32)).astype(BF16)
            rt = jnp.dot(jnp.concatenate([hi, lo], axis=1), tri, preferred_element_type=F32)
            c = c_scr[...]
            w = jnp.exp(z - sp + rt[:, :SB_TK] + c)
            if masked:
                w = jnp.where(dmask, w, 0.0)
            vt = v_ref[0, h, pl.ds(pl.multiple_of(j * SB_TK, SB_TK), SB_TK), :]
            acc_scr[...] += jnp.dot(w.astype(BF16), vt, preferred_element_type=F32)
            c_scr[...] = c + rt[:, SB_TK:]

        acc_scr[...] = jnp.zeros_like(acc_scr)
        c_scr[...] = jnp.zeros_like(c_scr)
        tile(qi, True)

        def body(jj, carry):
            tile(qi - 1 - jj, False)
            return carry

        lax.fori_loop(0, qi, body, 0)
        o_ref[:, h * HEAD_DIM:(h + 1) * HEAD_DIM] = acc_scr[...]


def _sb_prompt(sq, skv, bsz, t):
    nt = t // SB_TK
    k = skv[:, :SB_WIDTH].reshape(bsz, nt, SB_TK, SB_HEADS, HEAD_DIM)
    kt = k.transpose(0, 3, 1, 4, 2).astype(BF16)
    v = skv[:, SB_WIDTH:].reshape(bsz, t, SB_HEADS, HEAD_DIM).transpose(0, 2, 1, 3).astype(BF16)
    return pl.pallas_call(
        _sb_kernel,
        grid=(bsz, nt),
        in_specs=[
            pl.BlockSpec((SB_TK, SB_WIDTH), lambda b, i: (b * nt + i, 0)),
            pl.BlockSpec((1, SB_HEADS, nt, HEAD_DIM, SB_TK), lambda b, i: (b, 0, 0, 0, 0)),
            pl.BlockSpec((1, SB_HEADS, t, HEAD_DIM), lambda b, i: (b, 0, 0, 0)),
            pl.BlockSpec((2 * SB_TK, 2 * SB_TK), lambda b, i: (0, 0)),
        ],
        out_specs=pl.BlockSpec((SB_TK, SB_WIDTH), lambda b, i: (b * nt + i, 0)),
        out_shape=jax.ShapeDtypeStruct((bsz * t, SB_WIDTH), F32),
        scratch_shapes=[pltpu.VMEM((SB_TK, HEAD_DIM), F32), pltpu.VMEM((SB_TK, SB_TK), F32)],
        compiler_params=_cparams(("parallel", "arbitrary")),
        name="sb_prompt",
    )(sq, kt, v, _sb_tri())


NSA_TQ = 128
NSA_TQC = 512
NSA_KC = 512
MASK_BIG = 2.0 ** 100
CMP_CHUNK_W = CMP_STRIDE * 2 * NSA_KV_WIDTH


def _cmp_weights(w_cmp, pe_cmp):
    r = CMP_LEN // CMP_STRIDE
    wj = w_cmp.reshape(2, r, CMP_STRIDE, HEAD_DIM, HEAD_DIM)
    eye = jnp.eye(2 * NSA_KV_HEADS, dtype=F32).reshape(2, NSA_KV_HEADS, 2, NSA_KV_HEADS)
    wb = jnp.einsum('kjsde,kgmh->jskgdmhe', wj, eye)
    wbig = wb.reshape(r, CMP_CHUNK_W, 2 * NSA_KV_WIDTH).transpose(1, 0, 2).reshape(CMP_CHUNK_W, r * 2 * NSA_KV_WIDTH)
    pj = pe_cmp.reshape(2, r, CMP_STRIDE, HEAD_DIM)
    pe = jnp.broadcast_to(pj.transpose(1, 2, 0, 3)[:, :, :, None, :],
                          (r, CMP_STRIDE, 2, NSA_KV_HEADS, HEAD_DIM)).reshape(r, CMP_CHUNK_W)
    pe8 = jnp.concatenate([pe, jnp.zeros((8 - r, CMP_CHUNK_W), F32)], axis=0)
    return wbig.astype(BF16), pe8.astype(BF16)


def _compress_kernel(x_ref, w_ref, pe_ref, o_ref):
    w = w_ref[...]
    width = 2 * NSA_KV_WIDTH
    a = jnp.dot(x_ref[0].astype(BF16), w, preferred_element_type=F32)
    pb = jnp.dot(pe_ref[...], w, preferred_element_type=F32)
    bias = pb[0:1, :width] + pb[1:2, width:]
    nch = a.shape[0]
    a1 = pltpu.roll(a[:, width:], nch - 1, 0)
    o_ref[0] = a[:, :width] + a1 + bias


def _nsa_compress(xc, wbig, pe8):
    bsz, nch, _ = xc.shape
    return pl.pallas_call(
        _compress_kernel,
        grid=(bsz,),
        in_specs=[pl.BlockSpec((1, nch, CMP_CHUNK_W), lambda b: (b, 0, 0)),
                  pl.BlockSpec(wbig.shape, lambda b: (0, 0)),
                  pl.BlockSpec(pe8.shape, lambda b: (0, 0))],
        out_specs=pl.BlockSpec((1, nch, 2 * NSA_KV_WIDTH), lambda b: (b, 0, 0)),
        out_shape=jax.ShapeDtypeStruct((bsz, nch, 2 * NSA_KV_WIDTH), F32),
        compiler_params=_cparams(("parallel",)),
        name="nsa_compress",
    )(xc, wbig, pe8)


def _split_hi_lo(x):
    hi = x.astype(BF16)
    lo = (x - hi.astype(F32)).astype(BF16)
    return jnp.concatenate([hi, lo], axis=1)


def _nsa_cmp_kernel(qn_ref, kct_ref, vc_ref, cmap_ref, ocmp_ref, unsel_ref):
    tq = qn_ref.shape[0]
    ncp = kct_ref.shape[3]
    q0 = pl.program_id(2) * tq
    qpos = q0 + lax.broadcasted_iota(jnp.int32, (tq, 1), 0)
    cidx = lax.broadcasted_iota(jnp.int32, (tq, ncp), 1)
    mc = (cidx * CMP_STRIDE + (CMP_LEN - 1) <= qpos) & (cidx < ncp - 1)
    kct = kct_ref[0, 0]
    vc = vc_ref[0, 0]
    psum = jnp.zeros((tq, ncp), F32)
    for r in range(NSA_REP):
        q = qn_ref[:, r * HEAD_DIM:(r + 1) * HEAD_DIM].astype(BF16)
        s = jnp.where(mc, jnp.dot(q, kct, preferred_element_type=F32), NEG)
        m = jnp.max(s, axis=-1, keepdims=True)
        e = jnp.exp(s - m)
        p = jnp.where(mc, e / jnp.sum(e, axis=-1, keepdims=True), 0.0)
        ocmp_ref[:, r * HEAD_DIM:(r + 1) * HEAD_DIM] = jnp.dot(p.astype(BF16), vc, preferred_element_type=F32)
        psum = psum + p
    imp = jnp.dot(_split_hi_lo(psum), cmap_ref[...], preferred_element_type=F32)
    blk = lax.broadcasted_iota(jnp.int32, imp.shape, 1)
    cur = qpos // SEL_LEN
    imp = jnp.where(blk == cur, FORCE, jnp.where(blk < cur, imp, -1.0))
    unsel = jnp.ones(imp.shape, F32)
    for _ in range(SEL_TOPK):
        m = jnp.max(imp, axis=-1, keepdims=True)
        idx = jnp.min(jnp.where(imp == m, blk, LANES), axis=-1, keepdims=True)
        hit = blk == idx
        unsel = jnp.where(hit, 0.0, unsel)
        imp = jnp.where(hit, -3.0e38, imp)
    unsel_ref[0] = jnp.where(blk <= cur, unsel, 1.0).astype(BF16)


def _nsa_cmp(qn, kct, vc, cmap2, bsz, t):
    nqt = t // NSA_TQC
    ncp = kct.shape[3]
    return pl.pallas_call(
        _nsa_cmp_kernel,
        grid=(bsz, NSA_KV_HEADS, nqt),
        in_specs=[
            pl.BlockSpec((NSA_TQC, NSA_REP * HEAD_DIM), lambda b, g, i: (b * nqt + i, g)),
            pl.BlockSpec((1, 1, HEAD_DIM, ncp), lambda b, g, i: (b, g, 0, 0)),
            pl.BlockSpec((1, 1, ncp, HEAD_DIM), lambda b, g, i: (b, g, 0, 0)),
            pl.BlockSpec(cmap2.shape, lambda b, g, i: (0, 0)),
        ],
        out_specs=[pl.BlockSpec((NSA_TQC, NSA_REP * HEAD_DIM), lambda b, g, i: (b * nqt + i, g)),
                   pl.BlockSpec((1, NSA_TQC, LANES), lambda b, g, i: (g, b * nqt + i, 0))],
        out_shape=[jax.ShapeDtypeStruct((bsz * t, NSA_WIDTH), F32),
                   jax.ShapeDtypeStruct((NSA_KV_HEADS, bsz * t, LANES), BF16)],
        compiler_params=_cparams(("parallel", "parallel", "arbitrary")),
        name="nsa_cmp_select",
    )(qn, kct, vc, cmap2)


def _stack_heads(x):
    return jnp.concatenate([x[:, r * HEAD_DIM:(r + 1) * HEAD_DIM] for r in range(NSA_REP)], axis=0)


def _unstack_heads(x, tq):
    return jnp.concatenate([x[r * tq:(r + 1) * tq, :] for r in range(NSA_REP)], axis=1)


def _nsa_sel_kernel(qr_ref, unsel_ref, ocmp_ref, gate_ref, kst_ref, ksd_ref, vsa_ref, kwt_ref, vwa_ref, bige_ref,
                    gexp_ref, y_ref, m_scr, acc_scr):
    tq = qr_ref.shape[0]
    rows = NSA_REP * tq
    qi = pl.program_id(2)
    q4 = _stack_heads(qr_ref[...]).astype(BF16)
    rowq = lax.broadcasted_iota(jnp.int32, (tq, LANES), 0)
    lane = lax.broadcasted_iota(jnp.int32, (tq, LANES), 1)
    unsel = unsel_ref[0].astype(F32)

    def rep4(s, bias):
        return (s.reshape(NSA_REP, tq, s.shape[-1]) + bias[None]).reshape(rows, s.shape[-1])

    u0 = jnp.max(jnp.where(lane == 2 * qi, unsel, 0.0), axis=-1, keepdims=True)
    u1 = jnp.max(jnp.where(lane == 2 * qi + 1, unsel, 0.0), axis=-1, keepdims=True)
    ud = jnp.where(lane < SEL_LEN, u0, u1)
    bias_d = jnp.where((ud < 0.5) & (lane <= rowq), 0.0, -MASK_BIG)
    s = rep4(jnp.dot(q4, ksd_ref[0, 0, qi], preferred_element_type=F32), bias_d)
    m = jnp.max(s, axis=-1, keepdims=True)
    p = jnp.exp(s - m)
    vd = vsa_ref[0, 0, pl.ds(pl.multiple_of(qi * tq, tq), tq), :]
    acc_scr[...] = jnp.dot(p.astype(BF16), vd, preferred_element_type=F32)
    m_scr[...] = jnp.broadcast_to(m, (rows, LANES))

    unsel_past = jnp.where(lane >= 2 * qi, 1.0, unsel).astype(BF16)

    def chunk(c, carry):
        bias = jnp.dot(unsel_past, bige_ref[c], preferred_element_type=F32)
        s = rep4(jnp.dot(q4, kst_ref[0, 0, c], preferred_element_type=F32), bias)
        m_old = m_scr[...][:, 0:1]
        m_new = jnp.maximum(m_old, jnp.max(s, axis=-1, keepdims=True))
        alpha = jnp.exp(m_old - m_new)
        p = jnp.exp(s - m_new)
        vt = vsa_ref[0, 0, pl.ds(pl.multiple_of(c * NSA_KC, NSA_KC), NSA_KC), :]
        acc_scr[...] = acc_scr[...] * alpha + jnp.dot(p.astype(BF16), vt, preferred_element_type=F32)
        m_scr[...] = jnp.broadcast_to(m_new, (rows, LANES))
        return carry

    lax.fori_loop(0, (qi * tq + NSA_KC - 1) // NSA_KC, chunk, 0)
    acc = acc_scr[...]
    o_sel = acc[:, :HEAD_DIM] / acc[:, HEAD_DIM:HEAD_DIM + 1]

    ntw = (WINDOW + tq) // tq
    t0 = jnp.maximum(qi - WINDOW // tq, 0)
    sw = jnp.concatenate([jnp.dot(q4, kwt_ref[0, 0, t0 + i], preferred_element_type=F32) for i in range(ntw)],
                         axis=1)
    kpos = t0 * tq + lax.broadcasted_iota(jnp.int32, (tq, ntw * tq), 1)
    qpw = qi * tq + lax.broadcasted_iota(jnp.int32, (tq, ntw * tq), 0)
    bias_w = jnp.where((kpos <= qpw) & (kpos > qpw - WINDOW), 0.0, NEG)
    sw = rep4(sw, bias_w)
    mw = jnp.max(sw, axis=-1, keepdims=True)
    pw = jnp.exp(sw - mw)
    vw = vwa_ref[0, 0, pl.ds(pl.multiple_of(t0 * tq, tq), ntw * tq), :]
    aw = jnp.dot(pw.astype(BF16), vw, preferred_element_type=F32)
    o_win = aw[:, :HEAD_DIM] / aw[:, HEAD_DIM:HEAD_DIM + 1]

    ge = jnp.dot(_split_hi_lo(gate_ref[...]), gexp_ref[0], preferred_element_type=F32)
    w = NSA_REP * HEAD_DIM
    y_ref[...] = (ge[:, 0:w] * ocmp_ref[...] + ge[:, w:2 * w] * _unstack_heads(o_sel, tq)
                  + ge[:, 2 * w:3 * w] * _unstack_heads(o_win, tq))


def _nsa_sel(qr, unsel, ocmp, gates, kst, ksd, vsa, kwt, vwa, bige, gexp, bsz, t):
    nq = t // NSA_TQ
    w = NSA_REP * HEAD_DIM
    full = lambda a: pl.BlockSpec((1, 1) + a.shape[2:], lambda b, g, i: (b, g) + (0,) * (a.ndim - 2))
    return pl.pallas_call(
        _nsa_sel_kernel,
        grid=(bsz, NSA_KV_HEADS, nq),
        in_specs=[
            pl.BlockSpec((NSA_TQ, w), lambda b, g, i: (b * nq + i, g)),
            pl.BlockSpec((1, NSA_TQ, LANES), lambda b, g, i: (g, b * nq + i, 0)),
            pl.BlockSpec((NSA_TQ, w), lambda b, g, i: (b * nq + i, g)),
            pl.BlockSpec((NSA_TQ, LANES), lambda b, g, i: (b * nq + i, 0)),
            full(kst), full(ksd), full(vsa), full(kwt), full(vwa),
            pl.BlockSpec(bige.shape, lambda b, g, i: (0, 0, 0)),
            pl.BlockSpec((1,) + gexp.shape[1:], lambda b, g, i: (g, 0, 0)),
        ],
        out_specs=pl.BlockSpec((NSA_TQ, w), lambda b, g, i: (b * nq + i, g)),
        out_shape=jax.ShapeDtypeStruct((bsz * t, NSA_WIDTH), F32),
        scratch_shapes=[pltpu.VMEM((NSA_REP * NSA_TQ, LANES), F32), pltpu.VMEM((NSA_REP * NSA_TQ, LANES), F32)],
        compiler_params=_cparams(("parallel", "parallel", "arbitrary")),
        name="nsa_sel_win",
    )(qr, unsel, ocmp, gates, kst, ksd, vsa, kwt, vwa, bige, gexp)


def _nsa_consts(t):
    ncp = t // CMP_STRIDE
    ns = t // SEL_LEN
    cs = np.arange(ncp) * CMP_STRIDE
    ss = np.arange(LANES) * SEL_LEN
    ov = np.clip(np.minimum(cs[:, None] + CMP_LEN, ss[None, :] + SEL_LEN) - np.maximum(cs[:, None], ss[None, :]), 0, None)
    cmap = (ov / CMP_LEN).astype(np.float32)
    cmap[ncp - 1:, :] = 0.0
    cmap[:, ns:] = 0.0
    cmap2 = jnp.asarray(np.concatenate([cmap, cmap], axis=0), dtype=BF16)
    key_blk = (np.arange(t) // SEL_LEN).reshape(t // NSA_KC, 1, NSA_KC)
    bige = jnp.asarray(np.where(np.arange(LANES)[None, :, None] == key_blk, -MASK_BIG, 0.0), dtype=BF16)
    ge = np.zeros((NSA_KV_HEADS, LANES, 3 * NSA_REP * HEAD_DIM), np.float32)
    for g in range(NSA_KV_HEADS):
        for r in range(NSA_REP):
            for j in range(3):
                c0 = j * NSA_REP * HEAD_DIM + r * HEAD_DIM
                ge[g, (g * NSA_REP + r) * 3 + j, c0:c0 + HEAD_DIM] = 1.0
    gexp = jnp.asarray(np.concatenate([ge, ge], axis=1), dtype=BF16)
    return cmap2, bige, gexp


def _with_ones(v):
    return jnp.concatenate([v, jnp.ones_like(v)], axis=-1)


def _nsa_prompt(qn, qr, nsa, win, gates, wbig, pe8, consts, bsz, t):
    cmap2, bige, gexp = consts
    ncp = t // CMP_STRIDE
    xc = nsa[:, :2 * NSA_KV_WIDTH].reshape(bsz, ncp, CMP_CHUNK_W)
    kcv = _nsa_compress(xc, wbig, pe8).reshape(bsz, ncp, 2, NSA_KV_HEADS, HEAD_DIM)
    kct = kcv[:, :, 0].transpose(0, 2, 3, 1).astype(BF16)
    vc = kcv[:, :, 1].transpose(0, 2, 1, 3).astype(BF16)
    ocmp, unsel = _nsa_cmp(qn, kct, vc, cmap2, bsz, t)
    ks = nsa[:, 256:384].reshape(bsz, t // NSA_KC, NSA_KC, NSA_KV_HEADS, HEAD_DIM)
    kst = ks.transpose(0, 3, 1, 4, 2).astype(BF16)
    ksd = nsa[:, 256:384].reshape(bsz, t // NSA_TQ, NSA_TQ, NSA_KV_HEADS, HEAD_DIM).transpose(0, 3, 1, 4, 2).astype(BF16)
    vsa = _with_ones(nsa[:, 384:512].reshape(bsz, t, NSA_KV_HEADS, HEAD_DIM).transpose(0, 2, 1, 3)).astype(BF16)
    kw = win[:, :128].reshape(bsz, t // NSA_TQ, NSA_TQ, NSA_KV_HEADS, HEAD_DIM)
    kwt = kw.transpose(0, 3, 1, 4, 2).astype(BF16)
    vwa = _with_ones(win[:, 128:].reshape(bsz, t, NSA_KV_HEADS, HEAD_DIM).transpose(0, 2, 1, 3)).astype(BF16)
    return _nsa_sel(qr, unsel, ocmp, gates, kst, ksd, vsa, kwt, vwa, bige, gexp, bsz, t)


def _cplx_affine(e1, e2):
    a1r, a1i, b1r, b1i = e1
    a2r, a2i, b2r, b2i = e2
    return (a2r * a1r - a2i * a1i, a2r * a1i + a2i * a1r,
            a2r * b1r - a2i * b1i + b2r, a2r * b1i + a2i * b1r + b2i)


def _s5_branch(u, s0_re, s0_im, a_re, a_im, log_dt, b_re, b_im, c_re, c_im, d):
    bsz, t = u.shape[:2]
    dt = jnp.exp(log_dt)[:, None]
    mag = jnp.exp(a_re * dt)
    lb_re, lb_im = mag * jnp.cos(a_im * dt), mag * jnp.sin(a_im * dt)
    den = a_re * a_re + a_im * a_im
    co_re = ((lb_re - 1.0) * a_re + lb_im * a_im) / den
    co_im = (lb_im * a_re - (lb_re - 1.0) * a_im) / den
    bb_re = co_re[..., None] * b_re - co_im[..., None] * b_im
    bb_im = co_re[..., None] * b_im + co_im[..., None] * b_re
    ug = u.reshape(bsz, t, SSM_GROUPS, SSM_GROUP)
    bu_re = jnp.einsum('btgp,gnp->btgn', ug, bb_re)
    bu_im = jnp.einsum('btgp,gnp->btgn', ug, bb_im)
    bu_re = bu_re.at[:, 0].add(lb_re * s0_re - lb_im * s0_im)
    bu_im = bu_im.at[:, 0].add(lb_re * s0_im + lb_im * s0_re)
    la_re = jnp.broadcast_to(lb_re, bu_re.shape)
    la_im = jnp.broadcast_to(lb_im, bu_im.shape)
    _, _, s_re, s_im = lax.associative_scan(_cplx_affine, (la_re, la_im, bu_re, bu_im), axis=1)
    y = (jnp.einsum('btgn,gpn->btgp', s_re, c_re) - jnp.einsum('btgn,gpn->btgp', s_im, c_im) + d * ug)
    return y.reshape(bsz, t, SSM_WIDTH), s_re[:, -1], s_im[:, -1]


def _sb_core(q, k, v, qpos, kpos):
    z = jnp.einsum('bqhd,bshd->bhqs', q, k) / math.sqrt(HEAD_DIM)
    mask = kpos[None, :] < qpos[:, None]
    log_1m = jnp.where(mask, jax.nn.log_sigmoid(-z), 0.0)
    between = lax.cumsum(log_1m, axis=3, reverse=True) - log_1m
    w = jnp.where(mask, jnp.exp(jax.nn.log_sigmoid(z) + between), 0.0)
    return jnp.einsum('bhqs,bshd->bqhd', w, v)


def _compress(rows, pe, w):
    bsz, length = rows.shape[:2]
    r = CMP_LEN // CMP_STRIDE
    n_chunks = length // CMP_STRIDE
    nc = n_chunks - r + 1
    chunks = rows[:, :n_chunks * CMP_STRIDE].reshape(bsz, n_chunks, CMP_STRIDE, NSA_KV_HEADS, HEAD_DIM)
    wj = w.reshape(r, CMP_STRIDE, HEAD_DIM, HEAD_DIM)
    pj = pe.reshape(r, CMP_STRIDE, HEAD_DIM)
    out = 0.0
    for j in range(r):
        out = out + jnp.einsum('bnsgd,sde->bnge', chunks[:, j:j + nc], wj[j]) + jnp.einsum('sd,sde->e', pj[j], wj[j])
    cend = jnp.arange(nc) * CMP_STRIDE + CMP_LEN - 1
    return out, cend


def _cmp_to_sel(nc, ns):
    cs = np.arange(nc) * CMP_STRIDE
    ss = np.arange(ns) * SEL_LEN
    ov = np.clip(np.minimum(cs[:, None] + CMP_LEN, ss[None, :] + SEL_LEN) - np.maximum(cs[:, None], ss[None, :]), 0, None)
    return jnp.asarray(ov / CMP_LEN, dtype=F32)


def _pad_blocks(a):
    pad = (-a.shape[1]) % SEL_LEN
    return jnp.pad(a, ((0, 0), (0, pad), (0, 0), (0, 0)))


def _nsa_prepare(k_cmp, v_cmp, k_sel, v_sel, w_cmp, pe_cmp):
    kc, cend = _compress(k_cmp, pe_cmp[0], w_cmp[0])
    vc, _ = _compress(v_cmp, pe_cmp[1], w_cmp[1])
    ks, vs = _pad_blocks(k_sel), _pad_blocks(v_sel)
    cmap = _cmp_to_sel(kc.shape[1], ks.shape[1] // SEL_LEN)
    return kc, vc, cend, cmap, ks, vs


def _nsa_core(qr, qn, g, qpos, kc, vc, cend, cmap, ks, vs, kw, vw, kwpos):
    bsz, nq = qr.shape[:2]
    qr = qr.reshape(bsz, nq, NSA_KV_HEADS, NSA_REP, HEAD_DIM)
    qn = qn.reshape(bsz, nq, NSA_KV_HEADS, NSA_REP, HEAD_DIM)
    mc = cend[None, :] <= qpos[:, None]
    sc = jnp.einsum('bqgrd,bcgd->bgrqc', qn, kc)
    pc = jax.nn.softmax(jnp.where(mc, sc, NEG), axis=-1) * mc
    o_cmp = jnp.einsum('bgrqc,bcgd->bqgrd', pc, vc)
    ns = cmap.shape[1]
    topk = min(SEL_TOPK, ns)
    imp = jnp.einsum('bgrqc,cn->bgqn', pc, cmap)
    blk = jnp.arange(ns)[None, :]
    cur = (qpos // SEL_LEN)[:, None]
    imp = jnp.where(blk == cur, FORCE, jnp.where(blk < cur, imp, -1.0))
    _, idx = lax.top_k(imp, topk)
    ksb = ks.reshape(bsz, ns, SEL_LEN, NSA_KV_HEADS, HEAD_DIM).transpose(0, 3, 1, 2, 4)
    vsb = vs.reshape(bsz, ns, SEL_LEN, NSA_KV_HEADS, HEAD_DIM).transpose(0, 3, 1, 2, 4)
    bi = jnp.arange(bsz)[:, None, None, None]
    gi = jnp.arange(NSA_KV_HEADS)[None, :, None, None]
    kg, vg = ksb[bi, gi, idx], vsb[bi, gi, idx]
    spos = idx[..., None] * SEL_LEN + jnp.arange(SEL_LEN)
    ms = spos <= qpos[None, None, :, None, None]
    ss = jnp.einsum('bqgrd,bgqkld->bgrqkl', qr, kg)
    ss = jnp.where(ms[:, :, None], ss, NEG).reshape(bsz, NSA_KV_HEADS, NSA_REP, nq, topk * SEL_LEN)
    ps = jax.nn.softmax(ss, axis=-1).reshape(bsz, NSA_KV_HEADS, NSA_REP, nq, topk, SEL_LEN)
    o_sel = jnp.einsum('bgrqkl,bgqkld->bqgrd', ps, vg)
    mw = (kwpos[None, :] <= qpos[:, None]) & (kwpos[None, :] > qpos[:, None] - WINDOW) & (kwpos[None, :] >= 0)
    sw = jnp.einsum('bqgrd,bwgd->bgrqw', qr, kw)
    pw = jax.nn.softmax(jnp.where(mw, sw, NEG), axis=-1)
    o_win = jnp.einsum('bgrqw,bwgd->bqgrd', pw, vw)
    g = g.reshape(bsz, nq, NSA_KV_HEADS, NSA_REP, 3)
    o = g[..., 0:1] * o_cmp + g[..., 1:2] * o_sel + g[..., 2:3] * o_win
    return o.reshape(bsz, nq, NSA_WIDTH)


def _rope_tables(pos):
    half = HEAD_DIM // 2
    freqs = ROPE_THETA ** (-jnp.arange(half, dtype=F32) / half)
    ang = pos.astype(F32)[:, None] * freqs
    cos, sin = jnp.cos(ang), jnp.sin(ang)
    cos2 = jnp.concatenate([cos, cos], axis=-1)
    sin2 = jnp.concatenate([-sin, sin], axis=-1)
    return jnp.tile(cos2, (1, LANES // HEAD_DIM)), jnp.tile(sin2, (1, LANES // HEAD_DIM))


def _pack_w_in(w):
    pad = jnp.zeros((D_MODEL, LANES - 3 * NSA_HEADS), w.dtype)
    return jnp.concatenate([w[:, :2304], w[:, 2304:2328], pad, w[:, 2328:]], axis=1).astype(BF16)


def kernel(x_prompt, x_sample, cache_sb, cache_nsa, state_win, state_ssm, page_table, c_prompt, c_sample,
           w_ada, b_ada, norm_mix, norm_ffn, w_in, ssm_a_re, ssm_a_im, ssm_log_dt, ssm_b_re, ssm_b_im,
           ssm_c_re, ssm_c_im, ssm_d, w_glu, b_glu, nsa_w_cmp, nsa_pe_cmp, w_branch, w_out,
           w_grp, b_grp, w_rt, b_rt, w_e_gate, w_e_up, w_e_down, final_norm):
    bp, t = x_prompt.shape[:2]
    bd, s = x_sample.shape[:2]
    np_tok = bp * t
    win_len = state_win.shape[2]
    tm_p = 256
    tiles_pb = t // tm_p
    tm_f = 1024
    tiles_fb = t // tm_f

    xp = x_prompt.reshape(np_tok, D_MODEL)
    xs = x_sample.reshape(bd * s, D_MODEL)
    cos_p, sin_p = _rope_tables(jnp.arange(t))
    cos_s, sin_s = _rope_tables(jnp.full((bd,), PAST_LEN))
    pos_s = PAST_LEN + jnp.arange(s)
    nsa_consts = _nsa_consts(t)

    sb_p, sb_s, nsa_p, nsa_s, win_p, win_s, ssm_p, ssm_s = [], [], [], [], [], [], [], []
    for l in range(DEPTH):
        w_in_l = _pack_w_in(w_in[l])
        wb_l = w_branch[l].astype(BF16)
        wo_l = w_out[l].astype(BF16)
        wr_l = jnp.concatenate([w_grp[l], w_rt[l], jnp.zeros((D_MODEL, LANES - 20), F32)], axis=1)
        br_l = jnp.concatenate([b_grp[l], b_rt[l], jnp.zeros((LANES - 20,), F32)])[None, :]
        wg_l, wu_l, wd_l = w_e_gate[l].astype(BF16), w_e_up[l].astype(BF16), w_e_down[l].astype(BF16)
        gn_mix = norm_mix[l][None, :]
        gn_ffn = norm_ffn[l][None, :]
        ssm_l = (ssm_a_re[l], ssm_a_im[l], ssm_log_dt[l], ssm_b_re[l], ssm_b_im[l], ssm_c_re[l], ssm_c_im[l],
                 ssm_d[l])
        wglu_l = w_glu[l].astype(BF16)
        bglu_l = b_glu[l][None, :]

        mp = jnp.split(jax.nn.silu(c_prompt) @ w_ada[l] + b_ada[l], 6, axis=-1)
        mp = [m[:, None, :] for m in mp]
        u, sq, skv, qn, qr, nsa, win, ng, g = _inproj(
            xp, mp[0], mp[1], gn_mix, cos_p, sin_p, w_in_l, tm_p,
            lambda i: i // tiles_pb, lambda i: i % tiles_pb)
        y_a, sf = _s5_prompt(u.reshape(bp, t, SSM_WIDTH), _s5_mats(*ssm_l[:7]), ssm_d[l])
        y_b = _sb_prompt(sq, skv, bp, t)
        wbig, pe8 = _cmp_weights(nsa_w_cmp[l], nsa_pe_cmp[l])
        y_c = _nsa_prompt(qn, qr, nsa, win, ng, wbig, pe8, nsa_consts, bp, t)
        xp = _merge(xp, mp[2], y_a.reshape(np_tok, SSM_WIDTH), y_b, y_c, g, wglu_l, bglu_l, wb_l, wo_l,
                    tm_p, lambda i: i // tiles_pb)
        xp = _ffn(xp, mp[3], mp[4], mp[5], gn_ffn, wr_l, br_l, wg_l, wu_l, wd_l, tm_f, lambda i: i // tiles_fb)
        sb_p.append(skv.reshape(bp, t, 2, SB_HEADS, HEAD_DIM))
        nsa_p.append(nsa.reshape(bp, t, 4, NSA_KV_HEADS, HEAD_DIM))
        win_p.append(win.reshape(bp, t, 2, NSA_KV_HEADS, HEAD_DIM)[:, t - min(WINDOW, t):])
        ssm_p.append(sf)

        ms = jnp.split(jax.nn.silu(c_sample) @ w_ada[l] + b_ada[l], 6, axis=-1)
        ms = [m[None, :, :] for m in ms]
        u, sq, skv, qn, qr, nsa, win, ng, g = _inproj(
            xs, ms[0], ms[1], gn_mix, cos_s, sin_s, w_in_l, bd, lambda i: 0, lambda i: 0)
        y_a, sr, si = _s5_branch(u.reshape(bd, s, SSM_WIDTH), state_ssm[l][:, 0], state_ssm[l][:, 1], *ssm_l)
        new_sb = skv.reshape(bd, s, 2, SB_HEADS, HEAD_DIM)
        past_sb = cache_sb[l, page_table].reshape(bd, PAST_LEN, 2, SB_HEADS, HEAD_DIM)
        kv_sb = jnp.concatenate([past_sb, new_sb], axis=1)
        y_b = _sb_core(sq.reshape(bd, s, SB_HEADS, HEAD_DIM), kv_sb[:, :, 0], kv_sb[:, :, 1],
                       pos_s, jnp.arange(PAST_LEN + s)).reshape(bd, s, SB_WIDTH)
        new_nsa = nsa.reshape(bd, s, 4, NSA_KV_HEADS, HEAD_DIM)
        past_nsa = cache_nsa[l, page_table].reshape(bd, PAST_LEN, 4, NSA_KV_HEADS, HEAD_DIM)
        kv4 = jnp.concatenate([past_nsa, new_nsa], axis=1)
        kc, vc, cend, cmap, ksel, vsel = _nsa_prepare(kv4[:, :, 0], kv4[:, :, 1], kv4[:, :, 2], kv4[:, :, 3],
                                                      nsa_w_cmp[l], nsa_pe_cmp[l])
        new_win = win.reshape(bd, s, 2, NSA_KV_HEADS, HEAD_DIM)
        winc = jnp.concatenate([state_win[l], new_win], axis=1)
        kwpos = PAST_LEN - win_len + jnp.arange(win_len + s)
        gates = ng[:, :3 * NSA_HEADS].reshape(bd, s, NSA_HEADS, 3)
        y_c = _nsa_core(qr.reshape(bd, s, NSA_HEADS, HEAD_DIM), qn.reshape(bd, s, NSA_HEADS, HEAD_DIM), gates,
                        pos_s, kc, vc, cend, cmap, ksel, vsel, winc[:, :, 0], winc[:, :, 1], kwpos)
        xs = _merge(xs, ms[2], y_a.reshape(bd * s, SSM_WIDTH), y_b.reshape(bd * s, SB_WIDTH),
                    y_c.reshape(bd * s, NSA_WIDTH), g, wglu_l, bglu_l, wb_l, wo_l, bd, lambda i: 0)
        xs = _ffn(xs, ms[3], ms[4], ms[5], gn_ffn, wr_l, br_l, wg_l, wu_l, wd_l, bd, lambda i: 0)
        sb_s.append(new_sb)
        nsa_s.append(new_nsa)
        win_s.append(winc[:, s:])
        ssm_s.append(jnp.stack([sr, si], axis=1))

    fn = final_norm[None, :]
    y_prompt = _final_norm(xp, fn, tm_f).reshape(bp, t, D_MODEL)
    y_sample = _final_norm(xs, fn, bd).reshape(bd, s, D_MODEL)
    return (y_prompt, y_sample,
            jnp.stack(sb_p), jnp.stack(sb_s),
            jnp.stack(nsa_p), jnp.stack(nsa_s),
            jnp.stack(win_p), jnp.stack(win_s),
            jnp.stack(ssm_p), jnp.stack(ssm_s))
```

```python
import functools
import math

import jax
import jax.numpy as jnp
import numpy as np
from jax import lax
from jax.experimental import pallas as pl
from jax.experimental.pallas import tpu as pltpu

F32 = jnp.float32
BF16 = jnp.bfloat16

D_MODEL = 1024
DEPTH = 4
PAST_LEN = 2048
HEAD_DIM = 64
SSM_WIDTH = 256
SSM_GROUP = 16
SSM_GROUPS = 16
SSM_STATE = 64
SB_HEADS = 4
SB_WIDTH = 256
NSA_HEADS = 8
NSA_KV_HEADS = 2
NSA_REP = 4
NSA_WIDTH = 512
NSA_KV_WIDTH = 128
CMP_LEN = 32
CMP_STRIDE = 16
SEL_LEN = 64
SEL_TOPK = 16
WINDOW = 512
N_BRANCH = 3
Q_BLOCK = 128
ROPE_THETA = 10000.0
MOE_GROUPS = 4
EXPERTS_PER_GROUP = 4
N_EXPERTS = 16
EXPERT_FF = 256
EPS = 1e-6
NEG = -1e30
FORCE = 1e9

LANES = 128
VMEM_LIMIT = 56 * 1024 * 1024

_C_U = 0
_C_SQ = 256
_C_SKV = 512
_C_NQ = 1024
_C_NKV = 1536
_C_NG = 2304
_C_MG = 2432
_C_END = 5504


def _cparams(sem):
    return pltpu.CompilerParams(dimension_semantics=sem, vmem_limit_bytes=VMEM_LIMIT)


def _rope_slab(v, cos, sin_signed):
    lane = lax.broadcasted_iota(jnp.int32, v.shape, 1)
    first = (lane % HEAD_DIM) < (HEAD_DIM // 2)
    swapped = jnp.where(first, pltpu.roll(v, LANES - HEAD_DIM // 2, 1), pltpu.roll(v, HEAD_DIM // 2, 1))
    return v * cos + swapped * sin_signed


def _inproj_kernel(x_ref, shift_ref, scale_ref, gn_ref, cos_ref, sin_ref, w_ref,
                   u_ref, sq_ref, skv_ref, qn_ref, qr_ref, nsa_ref, win_ref, ng_ref, g_ref):
    x = x_ref[...]
    ms = jnp.mean(x * x, axis=-1, keepdims=True)
    h = x * lax.rsqrt(ms + EPS) * gn_ref[...]
    h = h * (1.0 + scale_ref[0]) + shift_ref[0]
    hb = h.astype(BF16)

    def mm(lo, hi):
        return jnp.dot(hb, w_ref[:, lo:hi], preferred_element_type=F32)

    cos = cos_ref[...]
    sin = sin_ref[...]
    u_ref[...] = mm(_C_U, _C_SQ)
    sq_ref[...] = mm(_C_SQ, _C_SKV)
    skv_ref[...] = mm(_C_SKV, _C_NQ)
    q = mm(_C_NQ, _C_NKV) * (1.0 / math.sqrt(HEAD_DIM))
    qn_ref[...] = q
    for s in range(NSA_WIDTH // LANES):
        qr_ref[:, s * LANES:(s + 1) * LANES] = _rope_slab(q[:, s * LANES:(s + 1) * LANES], cos, sin)
    kv = mm(_C_NKV, _C_NG)
    nsa_ref[:, 0:256] = kv[:, 0:256]
    nsa_ref[:, 256:384] = _rope_slab(kv[:, 256:384], cos, sin)
    nsa_ref[:, 384:512] = kv[:, 384:512]
    win_ref[:, 0:128] = _rope_slab(kv[:, 512:640], cos, sin)
    win_ref[:, 128:256] = kv[:, 640:768]
    ng_ref[...] = jax.nn.sigmoid(mm(_C_NG, _C_MG))
    g_ref[...] = jax.nn.sigmoid(mm(_C_MG, _C_END))


def _inproj(x, shift, scale, gn, cos, sin, w, tm, mod_map, pos_map):
    n = x.shape[0]
    mrows = shift.shape[1]
    row = lambda width: pl.BlockSpec((tm, width), lambda i: (i, 0))
    widths = (256, 256, 512, 512, 512, 512, 256, 128, 3072)
    return pl.pallas_call(
        _inproj_kernel,
        grid=(n // tm,),
        in_specs=[
            row(D_MODEL),
            pl.BlockSpec((1, mrows, D_MODEL), lambda i: (mod_map(i), 0, 0)),
            pl.BlockSpec((1, mrows, D_MODEL), lambda i: (mod_map(i), 0, 0)),
            pl.BlockSpec((1, D_MODEL), lambda i: (0, 0)),
            pl.BlockSpec((tm, LANES), lambda i: (pos_map(i), 0)),
            pl.BlockSpec((tm, LANES), lambda i: (pos_map(i), 0)),
            pl.BlockSpec((D_MODEL, _C_END), lambda i: (0, 0)),
        ],
        out_specs=[row(wd) for wd in widths],
        out_shape=[jax.ShapeDtypeStruct((n, wd), F32) for wd in widths],
        compiler_params=_cparams(("parallel",)),
        name="inproj",
    )(x, shift, scale, gn, cos, sin, w)


def _merge_kernel(x_ref, gate_ref, ya_ref, yb_ref, yc_ref, g_ref, wglu_ref, bglu_ref, wb_ref, wo_ref, o_ref):
    def mm(a, w):
        return jnp.dot(a.astype(BF16), w, preferred_element_type=F32)

    ya = jax.nn.gelu(ya_ref[...])
    ya = ya * jax.nn.sigmoid(mm(ya, wglu_ref[...]) + bglu_ref[...])
    br_a = mm(ya, wb_ref[0:SSM_WIDTH, :])
    br_b = mm(yb_ref[...], wb_ref[SSM_WIDTH:SSM_WIDTH + SB_WIDTH, :])
    br_c = mm(yc_ref[...], wb_ref[SSM_WIDTH + SB_WIDTH:, :])
    merged = (g_ref[:, 0:D_MODEL] * br_a + g_ref[:, D_MODEL:2 * D_MODEL] * br_b
              + g_ref[:, 2 * D_MODEL:] * br_c)
    out = mm(merged, wo_ref[...])
    o_ref[...] = x_ref[...] + gate_ref[0] * out


def _merge(x, gate, ya, yb, yc, g, wglu, bglu, wb, wo, tm, mod_map):
    n = x.shape[0]
    mrows = gate.shape[1]
    row = lambda width: pl.BlockSpec((tm, width), lambda i: (i, 0))
    return pl.pallas_call(
        _merge_kernel,
        grid=(n // tm,),
        in_specs=[
            row(D_MODEL),
            pl.BlockSpec((1, mrows, D_MODEL), lambda i: (mod_map(i), 0, 0)),
            row(SSM_WIDTH), row(SB_WIDTH), row(NSA_WIDTH), row(N_BRANCH * D_MODEL),
            pl.BlockSpec((SSM_WIDTH, SSM_WIDTH), lambda i: (0, 0)),
            pl.BlockSpec((1, SSM_WIDTH), lambda i: (0, 0)),
            pl.BlockSpec((D_MODEL, D_MODEL), lambda i: (0, 0)),
            pl.BlockSpec((D_MODEL, D_MODEL), lambda i: (0, 0)),
        ],
        out_specs=row(D_MODEL),
        out_shape=jax.ShapeDtypeStruct((n, D_MODEL), F32),
        compiler_params=_cparams(("parallel",)),
        name="merge",
    )(x, gate, ya, yb, yc, g, wglu, bglu, wb, wo)


def _route(logits):
    lane = lax.broadcasted_iota(jnp.int32, logits.shape, 1)
    big = jnp.int32(1 << 20)
    is_g = lane < MOE_GROUPS
    gl = jnp.where(is_g, logits, NEG)
    gmax = jnp.max(gl, axis=-1, keepdims=True)
    p_sel = 1.0 / jnp.sum(jnp.where(is_g, jnp.exp(gl - gmax), 0.0), axis=-1, keepdims=True)
    g_sel = jnp.min(jnp.where(is_g & (gl == gmax), lane, big), axis=-1, keepdims=True)
    e_idx = lane - MOE_GROUPS
    in_grp = (e_idx >= g_sel * EXPERTS_PER_GROUP) & (e_idx < (g_sel + 1) * EXPERTS_PER_GROUP)
    el = jnp.where(in_grp, logits, NEG)
    m1 = jnp.max(el, axis=-1, keepdims=True)
    i1 = jnp.min(jnp.where(in_grp & (el == m1), lane, big), axis=-1, keepdims=True)
    el2 = jnp.where(lane == i1, NEG, el)
    m2 = jnp.max(el2, axis=-1, keepdims=True)
    i2 = jnp.min(jnp.where(in_grp & (lane != i1) & (el2 == m2), lane, big), axis=-1, keepdims=True)
    e2 = jnp.exp(m2 - m1)
    w1 = p_sel / (1.0 + e2)
    w2 = p_sel * e2 / (1.0 + e2)
    return jnp.where(lane == i1, w1, jnp.where(lane == i2, w2, 0.0))


def _ffn_kernel(x_ref, shift_ref, scale_ref, gate_ref, gn_ref, wr_ref, br_ref, wg_ref, wu_ref, wd_ref,
                o_ref, h_scr, comb_scr, acc_scr):
    e = pl.program_id(1)

    @pl.when(e == 0)
    def _():
        x = x_ref[...]
        ms = jnp.mean(x * x, axis=-1, keepdims=True)
        h = x * lax.rsqrt(ms + EPS) * gn_ref[...]
        h = h * (1.0 + scale_ref[0]) + shift_ref[0]
        logits = jnp.dot(h, wr_ref[...], preferred_element_type=F32,
                         precision=lax.Precision.HIGHEST) + br_ref[...]
        comb_scr[...] = _route(logits)
        h_scr[...] = h.astype(BF16)
        acc_scr[...] = jnp.zeros_like(acc_scr)

    hb = h_scr[...]
    a = jnp.dot(hb, wg_ref[0], preferred_element_type=F32)
    b = jnp.dot(hb, wu_ref[0], preferred_element_type=F32)
    comb = comb_scr[...]
    lane = lax.broadcasted_iota(jnp.int32, comb.shape, 1)
    c = jnp.sum(jnp.where(lane == e + MOE_GROUPS, comb, 0.0), axis=-1, keepdims=True)
    hid = (a * jax.nn.sigmoid(a)) * b * c
    acc_scr[...] += jnp.dot(hid.astype(BF16), wd_ref[0], preferred_element_type=F32)

    @pl.when(e == N_EXPERTS - 1)
    def _():
        o_ref[...] = x_ref[...] + gate_ref[0] * acc_scr[...]


def _ffn(x, shift, scale, gate, gn, wr, br, wg, wu, wd, tm, mod_map):
    n = x.shape[0]
    mrows = shift.shape[1]
    mod = pl.BlockSpec((1, mrows, D_MODEL), lambda i, e: (mod_map(i), 0, 0))
    return pl.pallas_call(
        _ffn_kernel,
        grid=(n // tm, N_EXPERTS),
        in_specs=[
            pl.BlockSpec((tm, D_MODEL), lambda i, e: (i, 0)),
            mod, mod, mod,
            pl.BlockSpec((1, D_MODEL), lambda i, e: (0, 0)),
            pl.BlockSpec((D_MODEL, LANES), lambda i, e: (0, 0)),
            pl.BlockSpec((1, LANES), lambda i, e: (0, 0)),
            pl.BlockSpec((1, D_MODEL, EXPERT_FF), lambda i, e: (e, 0, 0)),
            pl.BlockSpec((1, D_MODEL, EXPERT_FF), lambda i, e: (e, 0, 0)),
            pl.BlockSpec((1, EXPERT_FF, D_MODEL), lambda i, e: (e, 0, 0)),
        ],
        out_specs=pl.BlockSpec((tm, D_MODEL), lambda i, e: (i, 0)),
        out_shape=jax.ShapeDtypeStruct((n, D_MODEL), F32),
        scratch_shapes=[pltpu.VMEM((tm, D_MODEL), BF16), pltpu.VMEM((tm, LANES), F32),
                        pltpu.VMEM((tm, D_MODEL), F32)],
        compiler_params=_cparams(("parallel", "arbitrary")),
        name="ffn_moe",
    )(x, shift, scale, gate, gn, wr, br, wg, wu, wd)


def _final_norm_kernel(x_ref, g_ref, o_ref):
    x = x_ref[...]
    o_ref[...] = x * lax.rsqrt(jnp.mean(x * x, axis=-1, keepdims=True) + EPS) * g_ref[...]


def _final_norm(x, g, tm):
    n = x.shape[0]
    return pl.pallas_call(
        _final_norm_kernel,
        grid=(n // tm,),
        in_specs=[pl.BlockSpec((tm, D_MODEL), lambda i: (i, 0)), pl.BlockSpec((1, D_MODEL), lambda i: (0, 0))],
        out_specs=pl.BlockSpec((tm, D_MODEL), lambda i: (i, 0)),
        out_shape=jax.ShapeDtypeStruct((n, D_MODEL), F32),
        compiler_params=_cparams(("parallel",)),
        name="final_norm",
    )(x, g)


S5_CHUNK = 64
S5_LW = S5_CHUNK * SSM_GROUP


def _s5_mats(a_re, a_im, log_dt, b_re, b_im, c_re, c_im):
    L = S5_CHUNK
    hp = lax.Precision.HIGHEST
    dt = jnp.exp(log_dt)[:, None]
    k = jnp.arange(L + 1, dtype=F32)[:, None, None]
    mag = jnp.exp(a_re * dt * k)
    ang = a_im * dt * k
    lk_re, lk_im = mag * jnp.cos(ang), mag * jnp.sin(ang)
    lb_re, lb_im = lk_re[1], lk_im[1]
    den = a_re * a_re + a_im * a_im
    co_re = ((lb_re - 1.0) * a_re + lb_im * a_im) / den
    co_im = (lb_im * a_re - (lb_re - 1.0) * a_im) / den
    bb_re = co_re[..., None] * b_re - co_im[..., None] * b_im
    bb_im = co_re[..., None] * b_im + co_im[..., None] * b_re
    cl_re = c_re[None] * lk_re[:, :, None, :] - c_im[None] * lk_im[:, :, None, :]
    cl_im = c_re[None] * lk_im[:, :, None, :] + c_im[None] * lk_re[:, :, None, :]
    kk = (jnp.einsum('kgpn,gnq->kgpq', cl_re[:L], bb_re, precision=hp)
          - jnp.einsum('kgpn,gnq->kgpq', cl_im[:L], bb_im, precision=hp))
    kq = kk.transpose(1, 3, 0, 2).reshape(SSM_GROUPS, SSM_GROUP, S5_LW).astype(BF16)
    kp = jnp.concatenate([jnp.zeros_like(kq), kq], axis=-1)
    tmat = jnp.stack([kp[:, :, S5_LW - SSM_GROUP * i:2 * S5_LW - SSM_GROUP * i] for i in range(L)], axis=1)
    tmat = tmat.reshape(SSM_GROUPS, S5_LW, S5_LW)
    rev = L - 1 - jnp.arange(L)
    be_re = lk_re[rev][:, :, :, None] * bb_re[None] - lk_im[rev][:, :, :, None] * bb_im[None]
    be_im = lk_re[rev][:, :, :, None] * bb_im[None] + lk_im[rev][:, :, :, None] * bb_re[None]
    bmat = jnp.concatenate([be_re.transpose(1, 0, 3, 2).reshape(SSM_GROUPS, S5_LW, SSM_STATE),
                            be_im.transpose(1, 0, 3, 2).reshape(SSM_GROUPS, S5_LW, SSM_STATE)], axis=-1)
    cm_re = cl_re[1:].transpose(1, 3, 0, 2).reshape(SSM_GROUPS, SSM_STATE, S5_LW)
    cm_im = cl_im[1:].transpose(1, 3, 0, 2).reshape(SSM_GROUPS, SSM_STATE, S5_LW)
    cmat = jnp.concatenate([cm_re, -cm_im], axis=1)
    lbl = jnp.stack([lk_re[L], lk_im[L]], axis=1)
    return tmat.astype(BF16), bmat.astype(BF16), cmat.astype(BF16), lbl


def _s5_kernel(x_ref, t_ref, b_ref, c_ref, lbl_ref, d_ref, y_ref, sf_ref, er_scr, ei_scr, pr_scr, pi_scr):
    nb, _, nch, _ = x_ref.shape
    a_re = lbl_ref[0, 0:1, :]
    a_im = lbl_ref[0, 1:2, :]
    for b in range(nb):
        x = x_ref[b, 0]
        xb = x.astype(BF16)
        e = jnp.dot(xb, b_ref[0], preferred_element_type=F32)
        er_scr[...] = e[:, :SSM_STATE]
        ei_scr[...] = e[:, SSM_STATE:]

        def step(c, carry):
            s_re, s_im = carry
            pr_scr[pl.ds(c, 1), :] = s_re
            pi_scr[pl.ds(c, 1), :] = s_im
            n_re = a_re * s_re - a_im * s_im + er_scr[pl.ds(c, 1), :]
            n_im = a_re * s_im + a_im * s_re + ei_scr[pl.ds(c, 1), :]
            return n_re, n_im

        zero = jnp.zeros((1, SSM_STATE), F32)
        s_re, s_im = lax.fori_loop(0, nch, step, (zero, zero))
        sprev = jnp.concatenate([pr_scr[...], pi_scr[...]], axis=1).astype(BF16)
        y = (jnp.dot(xb, t_ref[0], preferred_element_type=F32)
             + jnp.dot(sprev, c_ref[0], preferred_element_type=F32) + d_ref[0] * x)
        y_ref[b, 0] = y
        sf_ref[b, 0] = jnp.concatenate([s_re, s_im], axis=1)


def _s5_prompt(u, mats, d):
    tmat, bmat, cmat, lbl = mats
    bsz, t = u.shape[:2]
    nch = t // S5_CHUNK
    x = u.reshape(bsz, nch, S5_CHUNK, SSM_GROUPS, SSM_GROUP).transpose(0, 3, 1, 2, 4).reshape(
        bsz, SSM_GROUPS, nch, S5_LW)
    dt = jnp.tile(d, (1, S5_CHUNK))[:, None, :]
    y, sf = pl.pallas_call(
        _s5_kernel,
        grid=(SSM_GROUPS,),
        in_specs=[
            pl.BlockSpec((bsz, 1, nch, S5_LW), lambda g: (0, g, 0, 0)),
            pl.BlockSpec((1, S5_LW, S5_LW), lambda g: (g, 0, 0)),
            pl.BlockSpec((1, S5_LW, 2 * SSM_STATE), lambda g: (g, 0, 0)),
            pl.BlockSpec((1, 2 * SSM_STATE, S5_LW), lambda g: (g, 0, 0)),
            pl.BlockSpec((1, 2, SSM_STATE), lambda g: (g, 0, 0)),
            pl.BlockSpec((1, 1, S5_LW), lambda g: (g, 0, 0)),
        ],
        out_specs=[pl.BlockSpec((bsz, 1, nch, S5_LW), lambda g: (0, g, 0, 0)),
                   pl.BlockSpec((bsz, 1, 1, 2 * SSM_STATE), lambda g: (0, g, 0, 0))],
        out_shape=[jax.ShapeDtypeStruct((bsz, SSM_GROUPS, nch, S5_LW), F32),
                   jax.ShapeDtypeStruct((bsz, SSM_GROUPS, 1, 2 * SSM_STATE), F32)],
        scratch_shapes=[pltpu.VMEM((nch, SSM_STATE), F32)] * 4,
        compiler_params=_cparams(("arbitrary",)),
        name="s5_prompt",
    )(x, tmat, bmat, cmat, lbl, dt)
    y = y.reshape(bsz, SSM_GROUPS, nch, S5_CHUNK, SSM_GROUP).transpose(0, 2, 3, 1, 4).reshape(bsz, t, SSM_WIDTH)
    sf = sf.reshape(bsz, SSM_GROUPS, 2, SSM_STATE).transpose(0, 2, 1, 3)
    return y, sf


SB_TK = 128
SB_TQ = 256
SB_PER = SB_TQ // SB_TK


def _sb_tri():
    j = np.arange(SB_TK)[:, None]
    s = np.arange(SB_TK)[None, :]
    return jnp.asarray(-(s > j).astype(np.float32), dtype=BF16)


def _sb_kernel(qt_ref, k_ref, vt_ref, tri_ref, o_ref, acc_scr, c_scr):
    qi = pl.program_id(1)
    tri = tri_ref[...]
    kidx = lax.broadcasted_iota(jnp.int32, (SB_TK, SB_TQ), 0)
    qidx = lax.broadcasted_iota(jnp.int32, (SB_TK, SB_TQ), 1)

    def softplus_tile(h, j, mask):
        kt = k_ref[0, h, pl.ds(pl.multiple_of(j * SB_TK, SB_TK), SB_TK), :]
        z = jnp.dot(kt, qt_ref[0, h], preferred_element_type=F32)
        sp = jnp.maximum(z, 0.0) + jnp.log(1.0 + jnp.exp(-jnp.abs(z)))
        if mask is not None:
            sp = jnp.where(mask, sp, 0.0)
        return z, sp

    def weights(z, sp, c, mask):
        r = jnp.dot(tri, sp.astype(BF16), preferred_element_type=F32)
        w = jnp.exp(z - sp + r - c)
        if mask is not None:
            w = jnp.where(mask, w, 0.0)
        return w.astype(BF16)

    def two_tiles(j_hi, masks):
        heads = range(SB_HEADS)
        zs1 = [softplus_tile(h, j_hi, masks[0]) for h in heads]
        zs0 = [softplus_tile(h, j_hi - 1, masks[1]) for h in heads]
        cs = [c_scr[h] for h in heads]
        t1 = [jnp.sum(zs1[h][1], axis=0, keepdims=True) for h in heads]
        w1 = [weights(zs1[h][0], zs1[h][1], cs[h], masks[0]) for h in heads]
        w0 = [weights(zs0[h][0], zs0[h][1], cs[h] + t1[h], masks[1]) for h in heads]
        for h in heads:
            acc_scr[h] += (jnp.dot(vt_ref[0, h, j_hi], w1[h], preferred_element_type=F32)
                           + jnp.dot(vt_ref[0, h, j_hi - 1], w0[h], preferred_element_type=F32))
            c_scr[h] = cs[h] + t1[h] + jnp.sum(zs0[h][1], axis=0, keepdims=True)

    acc_scr[...] = jnp.zeros_like(acc_scr)
    c_scr[...] = jnp.zeros_like(c_scr)
    two_tiles(SB_PER * qi + 1, (kidx + SB_TK < qidx, kidx < qidx))

    def body(p, carry):
        two_tiles(SB_PER * (qi - 1 - p) + 1, (None, None))
        return carry

    lax.fori_loop(0, qi, body, 0)
    o_ref[0] = acc_scr[...]


def _sb_prompt(sq, skv, bsz, t):
    assert SB_PER == 2
    nt = t // SB_TK
    nq = t // SB_TQ
    qt = (sq * (1.0 / math.sqrt(HEAD_DIM))).reshape(bsz, t, SB_HEADS, HEAD_DIM).transpose(0, 2, 3, 1).astype(BF16)
    k = skv[:, :SB_WIDTH].reshape(bsz, t, SB_HEADS, HEAD_DIM).transpose(0, 2, 1, 3).astype(BF16)
    vt = skv[:, SB_WIDTH:].reshape(bsz, nt, SB_TK, SB_HEADS, HEAD_DIM).transpose(0, 3, 1, 4, 2).astype(BF16)
    yt = pl.pallas_call(
        _sb_kernel,
        grid=(bsz, nq),
        in_specs=[
            pl.BlockSpec((1, SB_HEADS, HEAD_DIM, SB_TQ), lambda b, i: (b, 0, 0, i)),
            pl.BlockSpec((1, SB_HEADS, t, HEAD_DIM), lambda b, i: (b, 0, 0, 0)),
            pl.BlockSpec((1, SB_HEADS, nt, HEAD_DIM, SB_TK), lambda b, i: (b, 0, 0, 0, 0)),
            pl.BlockSpec((SB_TK, SB_TK), lambda b, i: (0, 0)),
        ],
        out_specs=pl.BlockSpec((1, SB_HEADS, HEAD_DIM, SB_TQ), lambda b, i: (b, 0, 0, i)),
        out_shape=jax.ShapeDtypeStruct((bsz, SB_HEADS, HEAD_DIM, t), F32),
        scratch_shapes=[pltpu.VMEM((SB_HEADS, HEAD_DIM, SB_TQ), F32), pltpu.VMEM((SB_HEADS, 1, SB_TQ), F32)],
        compiler_params=_cparams(("parallel", "arbitrary")),
        name="sb_prompt",
    )(qt, k, vt, _sb_tri())
    return yt.transpose(0, 3, 1, 2).reshape(bsz * t, SB_WIDTH)


NSA_TQ = 128
NSA_TQC = 512
NSA_KC = 512
MASK_BIG = 2.0 ** 100
CMP_CHUNK_W = CMP_STRIDE * 2 * NSA_KV_WIDTH


def _cmp_weights(w_cmp, pe_cmp):
    r = CMP_LEN // CMP_STRIDE
    wj = w_cmp.reshape(2, r, CMP_STRIDE, HEAD_DIM, HEAD_DIM)
    eye = jnp.eye(2 * NSA_KV_HEADS, dtype=F32).reshape(2, NSA_KV_HEADS, 2, NSA_KV_HEADS)
    wb = jnp.einsum('kjsde,kgmh->jskgdmhe', wj, eye)
    wbig = wb.reshape(r, CMP_CHUNK_W, 2 * NSA_KV_WIDTH).transpose(1, 0, 2).reshape(CMP_CHUNK_W, r * 2 * NSA_KV_WIDTH)
    pj = pe_cmp.reshape(2, r, CMP_STRIDE, HEAD_DIM)
    pe = jnp.broadcast_to(pj.transpose(1, 2, 0, 3)[:, :, :, None, :],
                          (r, CMP_STRIDE, 2, NSA_KV_HEADS, HEAD_DIM)).reshape(r, CMP_CHUNK_W)
    pe8 = jnp.concatenate([pe, jnp.zeros((8 - r, CMP_CHUNK_W), F32)], axis=0)
    return wbig.astype(BF16), pe8.astype(BF16)


def _compress_kernel(x_ref, w_ref, pe_ref, o_ref):
    w = w_ref[...]
    width = 2 * NSA_KV_WIDTH
    a = jnp.dot(x_ref[0].astype(BF16), w, preferred_element_type=F32)
    pb = jnp.dot(pe_ref[...], w, preferred_element_type=F32)
    bias = pb[0:1, :width] + pb[1:2, width:]
    nch = a.shape[0]
    a1 = pltpu.roll(a[:, width:], nch - 1, 0)
    o_ref[0] = a[:, :width] + a1 + bias


def _nsa_compress(xc, wbig, pe8):
    bsz, nch, _ = xc.shape
    return pl.pallas_call(
        _compress_kernel,
        grid=(bsz,),
        in_specs=[pl.BlockSpec((1, nch, CMP_CHUNK_W), lambda b: (b, 0, 0)),
                  pl.BlockSpec(wbig.shape, lambda b: (0, 0)),
                  pl.BlockSpec(pe8.shape, lambda b: (0, 0))],
        out_specs=pl.BlockSpec((1, nch, 2 * NSA_KV_WIDTH), lambda b: (b, 0, 0)),
        out_shape=jax.ShapeDtypeStruct((bsz, nch, 2 * NSA_KV_WIDTH), F32),
        compiler_params=_cparams(("parallel",)),
        name="nsa_compress",
    )(xc, wbig, pe8)


def _split_hi_lo(x):
    hi = x.astype(BF16)
    lo = (x - hi.astype(F32)).astype(BF16)
    return jnp.concatenate([hi, lo], axis=1)


def _topk_unselected(imp, cur):
    blk = lax.broadcasted_iota(jnp.int32, imp.shape, 1)
    imp = jnp.where(blk == cur, FORCE, jnp.where(blk < cur, imp, -1.0))
    unsel = jnp.ones(imp.shape, F32)
    for _ in range(SEL_TOPK):
        m = jnp.max(imp, axis=-1, keepdims=True)
        idx = jnp.min(jnp.where(imp == m, blk, LANES), axis=-1, keepdims=True)
        hit = blk == idx
        unsel = jnp.where(hit, 0.0, unsel)
        imp = jnp.where(hit, -3.0e38, imp)
    return jnp.where(blk <= cur, unsel, 1.0)


def _nsa_cmp_kernel(qn_ref, kct_ref, vc_ref, cmap_ref, ocmp_ref, unsel_ref):
    tq = qn_ref.shape[0]
    ncp = kct_ref.shape[3]
    q0 = pl.program_id(2) * tq
    qpos = q0 + lax.broadcasted_iota(jnp.int32, (tq, 1), 0)
    cidx = lax.broadcasted_iota(jnp.int32, (tq, ncp), 1)
    mc = (cidx * CMP_STRIDE + (CMP_LEN - 1) <= qpos) & (cidx < ncp - 1)
    kct = kct_ref[0, 0]
    vc = vc_ref[0, 0]
    psum = jnp.zeros((tq, ncp), F32)
    for r in range(NSA_REP):
        q = qn_ref[:, r * HEAD_DIM:(r + 1) * HEAD_DIM].astype(BF16)
        s = jnp.where(mc, jnp.dot(q, kct, preferred_element_type=F32), NEG)
        m = jnp.max(s, axis=-1, keepdims=True)
        e = jnp.exp(s - m)
        p = jnp.where(mc, e / jnp.sum(e, axis=-1, keepdims=True), 0.0)
        ocmp_ref[:, r * HEAD_DIM:(r + 1) * HEAD_DIM] = jnp.dot(p.astype(BF16), vc, preferred_element_type=F32)
        psum = psum + p
    imp = jnp.dot(_split_hi_lo(psum), cmap_ref[...], preferred_element_type=F32)
    unsel_ref[0] = _topk_unselected(imp, qpos // SEL_LEN).astype(BF16)


def _nsa_cmp(qn, kct, vc, cmap2, bsz, t):
    nqt = t // NSA_TQC
    ncp = kct.shape[3]
    return pl.pallas_call(
        _nsa_cmp_kernel,
        grid=(bsz, NSA_KV_HEADS, nqt),
        in_specs=[
            pl.BlockSpec((NSA_TQC, NSA_REP * HEAD_DIM), lambda b, g, i: (b * nqt + i, g)),
            pl.BlockSpec((1, 1, HEAD_DIM, ncp), lambda b, g, i: (b, g, 0, 0)),
            pl.BlockSpec((1, 1, ncp, HEAD_DIM), lambda b, g, i: (b, g, 0, 0)),
            pl.BlockSpec(cmap2.shape, lambda b, g, i: (0, 0)),
        ],
        out_specs=[pl.BlockSpec((NSA_TQC, NSA_REP * HEAD_DIM), lambda b, g, i: (b * nqt + i, g)),
                   pl.BlockSpec((1, NSA_TQC, LANES), lambda b, g, i: (g, b * nqt + i, 0))],
        out_shape=[jax.ShapeDtypeStruct((bsz * t, NSA_WIDTH), F32),
                   jax.ShapeDtypeStruct((NSA_KV_HEADS, bsz * t, LANES), BF16)],
        compiler_params=_cparams(("parallel", "parallel", "arbitrary")),
        name="nsa_cmp_select",
    )(qn, kct, vc, cmap2)


def _stack_heads(x):
    return jnp.concatenate([x[:, r * HEAD_DIM:(r + 1) * HEAD_DIM] for r in range(NSA_REP)], axis=0)


def _unstack_heads(x, tq):
    return jnp.concatenate([x[r * tq:(r + 1) * tq, :] for r in range(NSA_REP)], axis=1)


def _nsa_sel_kernel(qr_ref, unsel_ref, ocmp_ref, gate_ref, kst_ref, ksd_ref, vsa_ref, kwt_ref, vwa_ref, bige_ref,
                    gexp_ref, y_ref, m_scr, acc_scr):
    tq = qr_ref.shape[0]
    rows = NSA_REP * tq
    qi = pl.program_id(2)
    q4 = _stack_heads(qr_ref[...]).astype(BF16)
    rowq = lax.broadcasted_iota(jnp.int32, (tq, LANES), 0)
    lane = lax.broadcasted_iota(jnp.int32, (tq, LANES), 1)
    unsel = unsel_ref[0].astype(F32)

    def rep4(s, bias):
        return (s.reshape(NSA_REP, tq, s.shape[-1]) + bias[None]).reshape(rows, s.shape[-1])

    u0 = jnp.max(jnp.where(lane == 2 * qi, unsel, 0.0), axis=-1, keepdims=True)
    u1 = jnp.max(jnp.where(lane == 2 * qi + 1, unsel, 0.0), axis=-1, keepdims=True)
    ud = jnp.where(lane < SEL_LEN, u0, u1)
    bias_d = jnp.where((ud < 0.5) & (lane <= rowq), 0.0, -MASK_BIG)
    s = rep4(jnp.dot(q4, ksd_ref[0, 0, qi], preferred_element_type=F32), bias_d)
    m = jnp.max(s, axis=-1, keepdims=True)
    p = jnp.exp(s - m)
    vd = vsa_ref[0, 0, pl.ds(pl.multiple_of(qi * tq, tq), tq), :]
    acc_scr[...] = jnp.dot(p.astype(BF16), vd, preferred_element_type=F32)
    m_scr[...] = jnp.broadcast_to(m, (rows, LANES))

    unsel_past = jnp.where(lane >= 2 * qi, 1.0, unsel).astype(BF16)

    def chunk(c, carry):
        bias = jnp.dot(unsel_past, bige_ref[c], preferred_element_type=F32)
        s = rep4(jnp.dot(q4, kst_ref[0, 0, c], preferred_element_type=F32), bias)
        m_old = m_scr[...][:, 0:1]
        m_new = jnp.maximum(m_old, jnp.max(s, axis=-1, keepdims=True))
        alpha = jnp.exp(m_old - m_new)
        p = jnp.exp(s - m_new)
        vt = vsa_ref[0, 0, pl.ds(pl.multiple_of(c * NSA_KC, NSA_KC), NSA_KC), :]
        acc_scr[...] = acc_scr[...] * alpha + jnp.dot(p.astype(BF16), vt, preferred_element_type=F32)
        m_scr[...] = jnp.broadcast_to(m_new, (rows, LANES))
        return carry

    lax.fori_loop(0, (qi * tq + NSA_KC - 1) // NSA_KC, chunk, 0)
    acc = acc_scr[...]
    o_sel = acc[:, :HEAD_DIM] / acc[:, HEAD_DIM:HEAD_DIM + 1]

    ntw = (WINDOW + tq) // tq
    t0 = jnp.maximum(qi - WINDOW // tq, 0)
    sw = jnp.concatenate([jnp.dot(q4, kwt_ref[0, 0, t0 + i], preferred_element_type=F32) for i in range(ntw)],
                         axis=1)
    kpos = t0 * tq + lax.broadcasted_iota(jnp.int32, (tq, ntw * tq), 1)
    qpw = qi * tq + lax.broadcasted_iota(jnp.int32, (tq, ntw * tq), 0)
    bias_w = jnp.where((kpos <= qpw) & (kpos > qpw - WINDOW), 0.0, NEG)
    sw = rep4(sw, bias_w)
    mw = jnp.max(sw, axis=-1, keepdims=True)
    pw = jnp.exp(sw - mw)
    vw = vwa_ref[0, 0, pl.ds(pl.multiple_of(t0 * tq, tq), ntw * tq), :]
    aw = jnp.dot(pw.astype(BF16), vw, preferred_element_type=F32)
    o_win = aw[:, :HEAD_DIM] / aw[:, HEAD_DIM:HEAD_DIM + 1]

    ge = jnp.dot(_split_hi_lo(gate_ref[...]), gexp_ref[0], preferred_element_type=F32)
    w = NSA_REP * HEAD_DIM
    y_ref[...] = (ge[:, 0:w] * ocmp_ref[...] + ge[:, w:2 * w] * _unstack_heads(o_sel, tq)
                  + ge[:, 2 * w:3 * w] * _unstack_heads(o_win, tq))


def _nsa_sel(qr, unsel, ocmp, gates, kst, ksd, vsa, kwt, vwa, bige, gexp, bsz, t):
    nq = t // NSA_TQ
    w = NSA_REP * HEAD_DIM
    full = lambda a: pl.BlockSpec((1, 1) + a.shape[2:], lambda b, g, i: (b, g) + (0,) * (a.ndim - 2))
    return pl.pallas_call(
        _nsa_sel_kernel,
        grid=(bsz, NSA_KV_HEADS, nq),
        in_specs=[
            pl.BlockSpec((NSA_TQ, w), lambda b, g, i: (b * nq + i, g)),
            pl.BlockSpec((1, NSA_TQ, LANES), lambda b, g, i: (g, b * nq + i, 0)),
            pl.BlockSpec((NSA_TQ, w), lambda b, g, i: (b * nq + i, g)),
            pl.BlockSpec((NSA_TQ, LANES), lambda b, g, i: (b * nq + i, 0)),
            full(kst), full(ksd), full(vsa), full(kwt), full(vwa),
            pl.BlockSpec(bige.shape, lambda b, g, i: (0, 0, 0)),
            pl.BlockSpec((1,) + gexp.shape[1:], lambda b, g, i: (g, 0, 0)),
        ],
        out_specs=pl.BlockSpec((NSA_TQ, w), lambda b, g, i: (b * nq + i, g)),
        out_shape=jax.ShapeDtypeStruct((bsz * t, NSA_WIDTH), F32),
        scratch_shapes=[pltpu.VMEM((NSA_REP * NSA_TQ, LANES), F32), pltpu.VMEM((NSA_REP * NSA_TQ, LANES), F32)],
        compiler_params=_cparams(("parallel", "parallel", "arbitrary")),
        name="nsa_sel_win",
    )(qr, unsel, ocmp, gates, kst, ksd, vsa, kwt, vwa, bige, gexp)


def _nsa_consts(t):
    ncp = t // CMP_STRIDE
    ns = t // SEL_LEN
    cs = np.arange(ncp) * CMP_STRIDE
    ss = np.arange(LANES) * SEL_LEN
    ov = np.clip(np.minimum(cs[:, None] + CMP_LEN, ss[None, :] + SEL_LEN) - np.maximum(cs[:, None], ss[None, :]), 0, None)
    cmap = (ov / CMP_LEN).astype(np.float32)
    cmap[ncp - 1:, :] = 0.0
    cmap[:, ns:] = 0.0
    cmap2 = jnp.asarray(np.concatenate([cmap, cmap], axis=0), dtype=BF16)
    key_blk = (np.arange(t) // SEL_LEN).reshape(t // NSA_KC, 1, NSA_KC)
    bige = jnp.asarray(np.where(np.arange(LANES)[None, :, None] == key_blk, -MASK_BIG, 0.0), dtype=BF16)
    ge = np.zeros((NSA_KV_HEADS, LANES, 3 * NSA_REP * HEAD_DIM), np.float32)
    for g in range(NSA_KV_HEADS):
        for r in range(NSA_REP):
            for j in range(3):
                c0 = j * NSA_REP * HEAD_DIM + r * HEAD_DIM
                ge[g, (g * NSA_REP + r) * 3 + j, c0:c0 + HEAD_DIM] = 1.0
    gexp = jnp.asarray(np.concatenate([ge, ge], axis=1), dtype=BF16)
    return cmap2, bige, gexp


def _with_ones(v):
    return jnp.concatenate([v, jnp.ones_like(v)], axis=-1)


def _nsa_prompt(qn, qr, nsa, win, gates, wbig, pe8, consts, bsz, t):
    cmap2, bige, gexp = consts
    ncp = t // CMP_STRIDE
    xc = nsa[:, :2 * NSA_KV_WIDTH].reshape(bsz, ncp, CMP_CHUNK_W)
    kcv = _nsa_compress(xc, wbig, pe8).reshape(bsz, ncp, 2, NSA_KV_HEADS, HEAD_DIM)
    kct = kcv[:, :, 0].transpose(0, 2, 3, 1).astype(BF16)
    vc = kcv[:, :, 1].transpose(0, 2, 1, 3).astype(BF16)
    ocmp, unsel = _nsa_cmp(qn, kct, vc, cmap2, bsz, t)
    ks = nsa[:, 256:384].reshape(bsz, t // NSA_KC, NSA_KC, NSA_KV_HEADS, HEAD_DIM)
    kst = ks.transpose(0, 3, 1, 4, 2).astype(BF16)
    ksd = nsa[:, 256:384].reshape(bsz, t // NSA_TQ, NSA_TQ, NSA_KV_HEADS, HEAD_DIM).transpose(0, 3, 1, 4, 2).astype(BF16)
    vsa = _with_ones(nsa[:, 384:512].reshape(bsz, t, NSA_KV_HEADS, HEAD_DIM).transpose(0, 2, 1, 3)).astype(BF16)
    kw = win[:, :128].reshape(bsz, t // NSA_TQ, NSA_TQ, NSA_KV_HEADS, HEAD_DIM)
    kwt = kw.transpose(0, 3, 1, 4, 2).astype(BF16)
    vwa = _with_ones(win[:, 128:].reshape(bsz, t, NSA_KV_HEADS, HEAD_DIM).transpose(0, 2, 1, 3)).astype(BF16)
    return _nsa_sel(qr, unsel, ocmp, gates, kst, ksd, vsa, kwt, vwa, bige, gexp, bsz, t)


PAGE = 128
N_PAGES = PAST_LEN // PAGE
HROWS = 8
NT_DIMS = (((1,), (1,)), ((), ()))


def _sb_tri_rows():
    s = np.arange(PAGE)[:, None]
    j = np.arange(2 * PAGE)[None, :]
    u = -((s > j) | (j >= PAGE)).astype(np.float32)
    return jnp.asarray(np.concatenate([u, u], axis=0), dtype=BF16)


def _page_specs(width, col):
    return [pl.BlockSpec((1, PAGE, width), lambda b, pt, p=p: (pt[b, p], 0, col)) for p in range(N_PAGES)]


def _dec_sb_kernel(pt_ref, q_ref, *rest):
    pages = rest[:N_PAGES]
    tri_ref, dmask_ref, y_ref = rest[N_PAGES:]
    q8 = q_ref[0]
    z = jnp.concatenate(
        [lax.dot_general(q8, pg[0, :, 0:SB_WIDTH].astype(BF16), NT_DIMS, preferred_element_type=F32)
         for pg in pages], axis=0)
    sp = jnp.maximum(z, 0.0) + jnp.log(1.0 + jnp.exp(-jnp.abs(z)))
    rt = jnp.dot(_split_hi_lo(sp), tri_ref[...], preferred_element_type=F32)
    cs = [None] * N_PAGES
    c = jnp.zeros((HROWS, PAGE), F32)
    for p in range(N_PAGES - 1, -1, -1):
        cs[p] = c
        c = c + rt[p * HROWS:(p + 1) * HROWS, PAGE:]
    w = jnp.exp(z - sp + rt[:, :PAGE] + jnp.concatenate(cs, axis=0)).astype(BF16)
    y8 = jnp.zeros((HROWS, SB_WIDTH), F32)
    for p, pg in enumerate(pages):
        y8 = y8 + jnp.dot(w[p * HROWS:(p + 1) * HROWS], pg[0, :, SB_WIDTH:].astype(BF16),
                          preferred_element_type=F32)
    y_ref[0] = jnp.sum(y8 * dmask_ref[...], axis=0, keepdims=True)


def _dec_sb(rows, sq, cache2d):
    bd = sq.shape[0]
    hm = np.zeros((HROWS, SB_WIDTH), np.float32)
    for h in range(SB_HEADS):
        hm[h, h * HEAD_DIM:(h + 1) * HEAD_DIM] = 1.0
    q8 = ((sq * (1.0 / math.sqrt(HEAD_DIM)))[:, None, :] * hm[None]).astype(BF16)
    grid_spec = pltpu.PrefetchScalarGridSpec(
        num_scalar_prefetch=1, grid=(bd,),
        in_specs=[pl.BlockSpec((1, HROWS, SB_WIDTH), lambda b, pt: (b, 0, 0))] + _page_specs(2 * SB_WIDTH, 0)
        + [pl.BlockSpec((2 * PAGE, 2 * PAGE), lambda b, pt: (0, 0)),
           pl.BlockSpec((HROWS, SB_WIDTH), lambda b, pt: (0, 0))],
        out_specs=pl.BlockSpec((1, 1, SB_WIDTH), lambda b, pt: (b, 0, 0)))
    y = pl.pallas_call(
        _dec_sb_kernel, grid_spec=grid_spec,
        out_shape=jax.ShapeDtypeStruct((bd, 1, SB_WIDTH), F32),
        compiler_params=_cparams(("arbitrary",)),
        name="dec_sb",
    )(rows, q8, *([cache2d] * N_PAGES), _sb_tri_rows(), jnp.asarray(hm))
    return y.reshape(bd, SB_WIDTH)


def _dec_cmp_kernel(pt_ref, q_ref, *rest):
    pages = rest[:N_PAGES]
    w_ref, pe_ref, cmap_ref, perm_ref, o_ref, imp_ref = rest[N_PAGES:]
    width = 2 * NSA_KV_WIDTH
    cpp = PAGE // CMP_STRIDE
    ncp = N_PAGES * cpp
    perm = perm_ref[...]
    pp = [jnp.dot(perm, pg[0].astype(BF16), preferred_element_type=F32) for pg in pages]
    a = jnp.zeros((ncp, 2 * width), F32)
    for s in range(CMP_STRIDE):
        xs = jnp.concatenate([x[s * cpp:(s + 1) * cpp] for x in pp], axis=0)
        a = a + jnp.dot(xs.astype(BF16), w_ref[s * width:(s + 1) * width, :], preferred_element_type=F32)
    pb = jnp.dot(pe_ref[...], w_ref[...], preferred_element_type=F32)
    bias = pb[0:1, :width] + pb[1:2, width:]
    kcv = a[:, :width] + pltpu.roll(a[:, width:], ncp - 1, 0) + bias
    q8 = q_ref[0]
    s8 = lax.dot_general(q8, kcv[:, :NSA_KV_WIDTH].astype(BF16), NT_DIMS, preferred_element_type=F32)
    valid = lax.broadcasted_iota(jnp.int32, s8.shape, 1) < ncp - 1
    s8 = jnp.where(valid, s8, NEG)
    e = jnp.exp(s8 - jnp.max(s8, axis=-1, keepdims=True))
    p = jnp.where(valid, e / jnp.sum(e, axis=-1, keepdims=True), 0.0)
    o_ref[0] = jnp.dot(p.astype(BF16), kcv[:, NSA_KV_WIDTH:].astype(BF16), preferred_element_type=F32)
    row = lax.broadcasted_iota(jnp.int32, p.shape, 0)
    psum = jnp.where(row < NSA_REP, jnp.sum(p[:NSA_REP], axis=0, keepdims=True),
                     jnp.sum(p[NSA_REP:], axis=0, keepdims=True))
    imp_ref[0] = jnp.dot(_split_hi_lo(psum), cmap_ref[...], preferred_element_type=F32)


def _head_rows(x, bd):
    x = x.reshape(bd, NSA_KV_HEADS, NSA_REP, 1, HEAD_DIM)
    eye = jnp.eye(NSA_KV_HEADS, dtype=x.dtype).reshape(1, NSA_KV_HEADS, 1, NSA_KV_HEADS, 1)
    return (x * eye).reshape(bd, NSA_HEADS, NSA_KV_WIDTH)


def _dec_cmp(rows, qn, cache2d, wbig, pe8, cmap2):
    bd = qn.shape[0]
    q8 = _head_rows(qn, bd).astype(BF16)
    cpp = PAGE // CMP_STRIDE
    pm = np.zeros((PAGE, PAGE), np.float32)
    for s in range(CMP_STRIDE):
        for n in range(cpp):
            pm[s * cpp + n, CMP_STRIDE * n + s] = 1.0
    perm = jnp.asarray(pm, dtype=BF16)
    grid_spec = pltpu.PrefetchScalarGridSpec(
        num_scalar_prefetch=1, grid=(bd,),
        in_specs=[pl.BlockSpec((1, NSA_HEADS, NSA_KV_WIDTH), lambda b, pt: (b, 0, 0))]
        + _page_specs(2 * NSA_KV_WIDTH, 0)
        + [pl.BlockSpec(wbig.shape, lambda b, pt: (0, 0)), pl.BlockSpec(pe8.shape, lambda b, pt: (0, 0)),
           pl.BlockSpec(cmap2.shape, lambda b, pt: (0, 0)), pl.BlockSpec((PAGE, PAGE), lambda b, pt: (0, 0))],
        out_specs=[pl.BlockSpec((1, NSA_HEADS, NSA_KV_WIDTH), lambda b, pt: (b, 0, 0)),
                   pl.BlockSpec((1, NSA_HEADS, LANES), lambda b, pt: (b, 0, 0))])
    return pl.pallas_call(
        _dec_cmp_kernel, grid_spec=grid_spec,
        out_shape=[jax.ShapeDtypeStruct((bd, NSA_HEADS, NSA_KV_WIDTH), F32),
                   jax.ShapeDtypeStruct((bd, NSA_HEADS, LANES), F32)],
        compiler_params=_cparams(("arbitrary",)),
        name="dec_cmp",
    )(rows, q8, *([cache2d] * N_PAGES), wbig, pe8, cmap2, perm)


def _dec_topk_kernel(imp_ref, unsel_ref):
    unsel_ref[...] = _topk_unselected(imp_ref[...], PAST_LEN // SEL_LEN)


def _dec_topk(imp):
    return pl.pallas_call(
        _dec_topk_kernel,
        out_shape=jax.ShapeDtypeStruct(imp.shape, F32),
        name="dec_topk",
    )(imp)


def _dec_sel_kernel(pt_ref, q_ref, unsel_ref, ocmp_ref, gate_ref, new_ref, win_ref, nwin_ref, *rest):
    pages = rest[:N_PAGES]
    bige_ref, y_ref = rest[N_PAGES:]
    q8 = q_ref[0]
    q8f = q8.astype(F32)

    def new_key_score(krow):
        return jnp.sum(q8f * krow.astype(BF16).astype(F32), axis=-1, keepdims=True)

    def attend(s3, s_new, vals, v_new):
        m = jnp.maximum(jnp.max(jnp.max(s3, axis=0), axis=-1, keepdims=True), s_new)
        p3 = jnp.exp(s3 - m[None])
        p_new = jnp.exp(s_new - m)
        den = jnp.sum(jnp.sum(p3, axis=0), axis=-1, keepdims=True) + p_new
        o = p_new * v_new.astype(BF16).astype(F32)
        for i, v in enumerate(vals):
            o = o + jnp.dot(p3[i].astype(BF16), v, preferred_element_type=F32)
        return o / den

    bias = jnp.dot(unsel_ref[0].astype(BF16), bige_ref[...], preferred_element_type=F32)
    s_sel = jnp.stack(
        [lax.dot_general(q8, pg[0, :, 0:NSA_KV_WIDTH].astype(BF16), NT_DIMS, preferred_element_type=F32)
         + bias[:, p * PAGE:(p + 1) * PAGE] for p, pg in enumerate(pages)], axis=0)
    new = new_ref[0]
    o_sel = attend(s_sel, new_key_score(new[:, 2 * NSA_KV_WIDTH:3 * NSA_KV_WIDTH]),
                   [pg[0, :, NSA_KV_WIDTH:].astype(BF16) for pg in pages], new[:, 3 * NSA_KV_WIDTH:])

    nt = WINDOW // PAGE
    kpos0 = lax.broadcasted_iota(jnp.int32, (HROWS, PAGE), 1) == 0
    s_win = []
    for i in range(nt):
        s = lax.dot_general(q8, win_ref[0, i * PAGE:(i + 1) * PAGE, 0:NSA_KV_WIDTH].astype(BF16), NT_DIMS,
                            preferred_element_type=F32)
        s_win.append(jnp.where(kpos0, NEG, s) if i == 0 else s)
    nwin = nwin_ref[0]
    o_win = attend(jnp.stack(s_win, axis=0), new_key_score(nwin[:, :NSA_KV_WIDTH]),
                   [win_ref[0, i * PAGE:(i + 1) * PAGE, NSA_KV_WIDTH:].astype(BF16) for i in range(nt)],
                   nwin[:, NSA_KV_WIDTH:])

    g = gate_ref[0]
    y8 = g[:, 0:LANES] * ocmp_ref[0] + g[:, LANES:2 * LANES] * o_sel + g[:, 2 * LANES:] * o_win
    for h in range(NSA_HEADS):
        c0 = (h // NSA_REP) * HEAD_DIM
        y_ref[0, :, h * HEAD_DIM:(h + 1) * HEAD_DIM] = y8[h:h + 1, c0:c0 + HEAD_DIM]


def _dec_sel(rows, qr, unsel, ocmp, gates, new_nsa, win_state, new_win, cache2d, bige):
    bd = qr.shape[0]
    q8 = _head_rows(qr, bd).astype(BF16)
    g3 = jnp.broadcast_to(gates[:, :3 * NSA_HEADS].reshape(bd, NSA_HEADS, 3, 1),
                          (bd, NSA_HEADS, 3, LANES)).reshape(bd, NSA_HEADS, 3 * LANES)
    per_b = lambda *shape: pl.BlockSpec((1,) + shape, lambda b, pt: (b,) + (0,) * len(shape))
    grid_spec = pltpu.PrefetchScalarGridSpec(
        num_scalar_prefetch=1, grid=(bd,),
        in_specs=[per_b(NSA_HEADS, NSA_KV_WIDTH), per_b(NSA_HEADS, LANES), per_b(NSA_HEADS, NSA_KV_WIDTH),
                  per_b(NSA_HEADS, 3 * LANES), per_b(1, 4 * NSA_KV_WIDTH), per_b(WINDOW, 2 * NSA_KV_WIDTH),
                  per_b(1, 2 * NSA_KV_WIDTH)]
        + _page_specs(2 * NSA_KV_WIDTH, 1)
        + [pl.BlockSpec(bige.shape, lambda b, pt: (0, 0))],
        out_specs=per_b(1, NSA_WIDTH))
    y = pl.pallas_call(
        _dec_sel_kernel, grid_spec=grid_spec,
        out_shape=jax.ShapeDtypeStruct((bd, 1, NSA_WIDTH), F32),
        compiler_params=_cparams(("arbitrary",)),
        name="dec_sel_win",
    )(rows, q8, unsel, ocmp, g3, new_nsa.reshape(bd, 1, -1), win_state, new_win.reshape(bd, 1, -1),
      *([cache2d] * N_PAGES), bige)
    return y.reshape(bd, NSA_WIDTH)


def _dec_consts():
    cmap2 = _nsa_consts(PAST_LEN)[0]
    blk = np.arange(PAST_LEN) // SEL_LEN
    bige = jnp.asarray(np.where(np.arange(LANES)[:, None] == blk[None, :], -MASK_BIG, 0.0), dtype=BF16)
    return cmap2, bige


def _cplx_affine(e1, e2):
    a1r, a1i, b1r, b1i = e1
    a2r, a2i, b2r, b2i = e2
    return (a2r * a1r - a2i * a1i, a2r * a1i + a2i * a1r,
            a2r * b1r - a2i * b1i + b2r, a2r * b1i + a2i * b1r + b2i)


def _s5_branch(u, s0_re, s0_im, a_re, a_im, log_dt, b_re, b_im, c_re, c_im, d):
    bsz, t = u.shape[:2]
    dt = jnp.exp(log_dt)[:, None]
    mag = jnp.exp(a_re * dt)
    lb_re, lb_im = mag * jnp.cos(a_im * dt), mag * jnp.sin(a_im * dt)
    den = a_re * a_re + a_im * a_im
    co_re = ((lb_re - 1.0) * a_re + lb_im * a_im) / den
    co_im = (lb_im * a_re - (lb_re - 1.0) * a_im) / den
    bb_re = co_re[..., None] * b_re - co_im[..., None] * b_im
    bb_im = co_re[..., None] * b_im + co_im[..., None] * b_re
    ug = u.reshape(bsz, t, SSM_GROUPS, SSM_GROUP)
    bu_re = jnp.einsum('btgp,gnp->btgn', ug, bb_re)
    bu_im = jnp.einsum('btgp,gnp->btgn', ug, bb_im)
    bu_re = bu_re.at[:, 0].add(lb_re * s0_re - lb_im * s0_im)
    bu_im = bu_im.at[:, 0].add(lb_re * s0_im + lb_im * s0_re)
    la_re = jnp.broadcast_to(lb_re, bu_re.shape)
    la_im = jnp.broadcast_to(lb_im, bu_im.shape)
    _, _, s_re, s_im = lax.associative_scan(_cplx_affine, (la_re, la_im, bu_re, bu_im), axis=1)
    y = (jnp.einsum('btgn,gpn->btgp', s_re, c_re) - jnp.einsum('btgn,gpn->btgp', s_im, c_im) + d * ug)
    return y.reshape(bsz, t, SSM_WIDTH), s_re[:, -1], s_im[:, -1]


def _sb_core(q, k, v, qpos, kpos):
    z = jnp.einsum('bqhd,bshd->bhqs', q, k) / math.sqrt(HEAD_DIM)
    mask = kpos[None, :] < qpos[:, None]
    log_1m = jnp.where(mask, jax.nn.log_sigmoid(-z), 0.0)
    between = lax.cumsum(log_1m, axis=3, reverse=True) - log_1m
    w = jnp.where(mask, jnp.exp(jax.nn.log_sigmoid(z) + between), 0.0)
    return jnp.einsum('bhqs,bshd->bqhd', w, v)


def _compress(rows, pe, w):
    bsz, length = rows.shape[:2]
    r = CMP_LEN // CMP_STRIDE
    n_chunks = length // CMP_STRIDE
    nc = n_chunks - r + 1
    chunks = rows[:, :n_chunks * CMP_STRIDE].reshape(bsz, n_chunks, CMP_STRIDE, NSA_KV_HEADS, HEAD_DIM)
    wj = w.reshape(r, CMP_STRIDE, HEAD_DIM, HEAD_DIM)
    pj = pe.reshape(r, CMP_STRIDE, HEAD_DIM)
    out = 0.0
    for j in range(r):
        out = out + jnp.einsum('bnsgd,sde->bnge', chunks[:, j:j + nc], wj[j]) + jnp.einsum('sd,sde->e', pj[j], wj[j])
    cend = jnp.arange(nc) * CMP_STRIDE + CMP_LEN - 1
    return out, cend


def _cmp_to_sel(nc, ns):
    cs = np.arange(nc) * CMP_STRIDE
    ss = np.arange(ns) * SEL_LEN
    ov = np.clip(np.minimum(cs[:, None] + CMP_LEN, ss[None, :] + SEL_LEN) - np.maximum(cs[:, None], ss[None, :]), 0, None)
    return jnp.asarray(ov / CMP_LEN, dtype=F32)


def _pad_blocks(a):
    pad = (-a.shape[1]) % SEL_LEN
    return jnp.pad(a, ((0, 0), (0, pad), (0, 0), (0, 0)))


def _nsa_prepare(k_cmp, v_cmp, k_sel, v_sel, w_cmp, pe_cmp):
    kc, cend = _compress(k_cmp, pe_cmp[0], w_cmp[0])
    vc, _ = _compress(v_cmp, pe_cmp[1], w_cmp[1])
    ks, vs = _pad_blocks(k_sel), _pad_blocks(v_sel)
    cmap = _cmp_to_sel(kc.shape[1], ks.shape[1] // SEL_LEN)
    return kc, vc, cend, cmap, ks, vs


def _nsa_core(qr, qn, g, qpos, kc, vc, cend, cmap, ks, vs, kw, vw, kwpos):
    bsz, nq = qr.shape[:2]
    qr = qr.reshape(bsz, nq, NSA_KV_HEADS, NSA_REP, HEAD_DIM)
    qn = qn.reshape(bsz, nq, NSA_KV_HEADS, NSA_REP, HEAD_DIM)
    mc = cend[None, :] <= qpos[:, None]
    sc = jnp.einsum('bqgrd,bcgd->bgrqc', qn, kc)
    pc = jax.nn.softmax(jnp.where(mc, sc, NEG), axis=-1) * mc
    o_cmp = jnp.einsum('bgrqc,bcgd->bqgrd', pc, vc)
    ns = cmap.shape[1]
    topk = min(SEL_TOPK, ns)
    imp = jnp.einsum('bgrqc,cn->bgqn', pc, cmap)
    blk = jnp.arange(ns)[None, :]
    cur = (qpos // SEL_LEN)[:, None]
    imp = jnp.where(blk == cur, FORCE, jnp.where(blk < cur, imp, -1.0))
    _, idx = lax.top_k(imp, topk)
    ksb = ks.reshape(bsz, ns, SEL_LEN, NSA_KV_HEADS, HEAD_DIM).transpose(0, 3, 1, 2, 4)
    vsb = vs.reshape(bsz, ns, SEL_LEN, NSA_KV_HEADS, HEAD_DIM).transpose(0, 3, 1, 2, 4)
    bi = jnp.arange(bsz)[:, None, None, None]
    gi = jnp.arange(NSA_KV_HEADS)[None, :, None, None]
    kg, vg = ksb[bi, gi, idx], vsb[bi, gi, idx]
    spos = idx[..., None] * SEL_LEN + jnp.arange(SEL_LEN)
    ms = spos <= qpos[None, None, :, None, None]
    ss = jnp.einsum('bqgrd,bgqkld->bgrqkl', qr, kg)
    ss = jnp.where(ms[:, :, None], ss, NEG).reshape(bsz, NSA_KV_HEADS, NSA_REP, nq, topk * SEL_LEN)
    ps = jax.nn.softmax(ss, axis=-1).reshape(bsz, NSA_KV_HEADS, NSA_REP, nq, topk, SEL_LEN)
    o_sel = jnp.einsum('bgrqkl,bgqkld->bqgrd', ps, vg)
    mw = (kwpos[None, :] <= qpos[:, None]) & (kwpos[None, :] > qpos[:, None] - WINDOW) & (kwpos[None, :] >= 0)
    sw = jnp.einsum('bqgrd,bwgd->bgrqw', qr, kw)
    pw = jax.nn.softmax(jnp.where(mw, sw, NEG), axis=-1)
    o_win = jnp.einsum('bgrqw,bwgd->bqgrd', pw, vw)
    g = g.reshape(bsz, nq, NSA_KV_HEADS, NSA_REP, 3)
    o = g[..., 0:1] * o_cmp + g[..., 1:2] * o_sel + g[..., 2:3] * o_win
    return o.reshape(bsz, nq, NSA_WIDTH)


def _rope_tables(pos):
    half = HEAD_DIM // 2
    freqs = ROPE_THETA ** (-jnp.arange(half, dtype=F32) / half)
    ang = pos.astype(F32)[:, None] * freqs
    cos, sin = jnp.cos(ang), jnp.sin(ang)
    cos2 = jnp.concatenate([cos, cos], axis=-1)
    sin2 = jnp.concatenate([-sin, sin], axis=-1)
    return jnp.tile(cos2, (1, LANES // HEAD_DIM)), jnp.tile(sin2, (1, LANES // HEAD_DIM))


def _pack_w_in(w):
    pad = jnp.zeros((D_MODEL, LANES - 3 * NSA_HEADS), w.dtype)
    return jnp.concatenate([w[:, :2304], w[:, 2304:2328], pad, w[:, 2328:]], axis=1).astype(BF16)


def kernel(x_prompt, x_sample, cache_sb, cache_nsa, state_win, state_ssm, page_table, c_prompt, c_sample,
           w_ada, b_ada, norm_mix, norm_ffn, w_in, ssm_a_re, ssm_a_im, ssm_log_dt, ssm_b_re, ssm_b_im,
           ssm_c_re, ssm_c_im, ssm_d, w_glu, b_glu, nsa_w_cmp, nsa_pe_cmp, w_branch, w_out,
           w_grp, b_grp, w_rt, b_rt, w_e_gate, w_e_up, w_e_down, final_norm):
    bp, t = x_prompt.shape[:2]
    bd, s = x_sample.shape[:2]
    np_tok = bp * t
    win_len = state_win.shape[2]
    tm_p = 256
    tiles_pb = t // tm_p
    tm_f = 1024
    tiles_fb = t // tm_f

    xp = x_prompt.reshape(np_tok, D_MODEL)
    xs = x_sample.reshape(bd * s, D_MODEL)
    cos_p, sin_p = _rope_tables(jnp.arange(t))
    cos_s, sin_s = _rope_tables(jnp.full((bd,), PAST_LEN))
    pos_s = PAST_LEN + jnp.arange(s)
    nsa_consts = _nsa_consts(t)
    dec_cmap2, dec_bige = _dec_consts()
    n_pool = cache_sb.shape[1]
    cache_sb2d = cache_sb.reshape(DEPTH * n_pool, PAGE, 2 * SB_WIDTH)
    cache_nsa2d = cache_nsa.reshape(DEPTH * n_pool, PAGE, 4 * NSA_KV_WIDTH)

    sb_p, sb_s, nsa_p, nsa_s, win_p, win_s, ssm_p, ssm_s = [], [], [], [], [], [], [], []
    for l in range(DEPTH):
        w_in_l = _pack_w_in(w_in[l])
        wb_l = w_branch[l].astype(BF16)
        wo_l = w_out[l].astype(BF16)
        wr_l = jnp.concatenate([w_grp[l], w_rt[l], jnp.zeros((D_MODEL, LANES - 20), F32)], axis=1)
        br_l = jnp.concatenate([b_grp[l], b_rt[l], jnp.zeros((LANES - 20,), F32)])[None, :]
        wg_l, wu_l, wd_l = w_e_gate[l].astype(BF16), w_e_up[l].astype(BF16), w_e_down[l].astype(BF16)
        gn_mix = norm_mix[l][None, :]
        gn_ffn = norm_ffn[l][None, :]
        ssm_l = (ssm_a_re[l], ssm_a_im[l], ssm_log_dt[l], ssm_b_re[l], ssm_b_im[l], ssm_c_re[l], ssm_c_im[l],
                 ssm_d[l])
        wglu_l = w_glu[l].astype(BF16)
        bglu_l = b_glu[l][None, :]

        mp = jnp.split(jax.nn.silu(c_prompt) @ w_ada[l] + b_ada[l], 6, axis=-1)
        mp = [m[:, None, :] for m in mp]
        u, sq, skv, qn, qr, nsa, win, ng, g = _inproj(
            xp, mp[0], mp[1], gn_mix, cos_p, sin_p, w_in_l, tm_p,
            lambda i: i // tiles_pb, lambda i: i % tiles_pb)
        y_a, sf = _s5_prompt(u.reshape(bp, t, SSM_WIDTH), _s5_mats(*ssm_l[:7]), ssm_d[l])
        y_b = _sb_prompt(sq, skv, bp, t)
        wbig, pe8 = _cmp_weights(nsa_w_cmp[l], nsa_pe_cmp[l])
        y_c = _nsa_prompt(qn, qr, nsa, win, ng, wbig, pe8, nsa_consts, bp, t)
        xp = _merge(xp, mp[2], y_a.reshape(np_tok, SSM_WIDTH), y_b, y_c, g, wglu_l, bglu_l, wb_l, wo_l,
                    tm_p, lambda i: i // tiles_pb)
        xp = _ffn(xp, mp[3], mp[4], mp[5], gn_ffn, wr_l, br_l, wg_l, wu_l, wd_l, tm_f, lambda i: i // tiles_fb)
        sb_p.append(skv.reshape(bp, t, 2, SB_HEADS, HEAD_DIM))
        nsa_p.append(nsa.reshape(bp, t, 4, NSA_KV_HEADS, HEAD_DIM))
        win_p.append(win.reshape(bp, t, 2, NSA_KV_HEADS, HEAD_DIM)[:, t - min(WINDOW, t):])
        ssm_p.append(sf)

        ms = jnp.split(jax.nn.silu(c_sample) @ w_ada[l] + b_ada[l], 6, axis=-1)
        ms = [m[None, :, :] for m in ms]
        u, sq, skv, qn, qr, nsa, win, ng, g = _inproj(
            xs, ms[0], ms[1], gn_mix, cos_s, sin_s, w_in_l, bd, lambda i: 0, lambda i: 0)
        y_a, sr, si = _s5_branch(u.reshape(bd, s, SSM_WIDTH), state_ssm[l][:, 0], state_ssm[l][:, 1], *ssm_l)
        new_sb = skv.reshape(bd, s, 2, SB_HEADS, HEAD_DIM)
        rows_l = page_table + l * n_pool
        y_b = _dec_sb(rows_l, sq, cache_sb2d)
        new_nsa = nsa.reshape(bd, s, 4, NSA_KV_HEADS, HEAD_DIM)
        ocmp, imp = _dec_cmp(rows_l, qn, cache_nsa2d, wbig, pe8, dec_cmap2)
        unsel = _dec_topk(imp.reshape(bd * NSA_HEADS, LANES)).reshape(bd, NSA_HEADS, LANES)
        new_win = win.reshape(bd, s, 2, NSA_KV_HEADS, HEAD_DIM)
        winc = jnp.concatenate([state_win[l], new_win], axis=1)
        y_c = _dec_sel(rows_l, qr, unsel, ocmp, ng, nsa, state_win[l].reshape(bd, win_len, 2 * NSA_KV_WIDTH), win,
                       cache_nsa2d, dec_bige)
        xs = _merge(xs, ms[2], y_a.reshape(bd * s, SSM_WIDTH), y_b, y_c, g, wglu_l, bglu_l, wb_l, wo_l,
                    bd, lambda i: 0)
        xs = _ffn(xs, ms[3], ms[4], ms[5], gn_ffn, wr_l, br_l, wg_l, wu_l, wd_l, bd, lambda i: 0)
        sb_s.append(new_sb)
        nsa_s.append(new_nsa)
        win_s.append(winc[:, s:])
        ssm_s.append(jnp.stack([sr, si], axis=1))

    fn = final_norm[None, :]
    y_prompt = _final_norm(xp, fn, tm_f).reshape(bp, t, D_MODEL)
    y_sample = _final_norm(xs, fn, bd).reshape(bd, s, D_MODEL)
    return (y_prompt, y_sample,
            jnp.stack(sb_p), jnp.stack(sb_s),
            jnp.stack(nsa_p), jnp.stack(nsa_s),
            jnp.stack(win_p), jnp.stack(win_s),
            jnp.stack(ssm_p), jnp.stack(ssm_s))
```

```python
import functools
import math

import jax
import jax.numpy as jnp
import numpy as np
from jax import lax
from jax.experimental import pallas as pl
from jax.experimental.pallas import tpu as pltpu

F32 = jnp.float32
BF16 = jnp.bfloat16

D_MODEL = 1024
DEPTH = 4
PAST_LEN = 2048
HEAD_DIM = 64
SSM_WIDTH = 256
SSM_GROUP = 16
SSM_GROUPS = 16
SSM_STATE = 64
SB_HEADS = 4
SB_WIDTH = 256
NSA_HEADS = 8
NSA_KV_HEADS = 2
NSA_REP = 4
NSA_WIDTH = 512
NSA_KV_WIDTH = 128
CMP_LEN = 32
CMP_STRIDE = 16
SEL_LEN = 64
SEL_TOPK = 16
WINDOW = 512
N_BRANCH = 3
Q_BLOCK = 128
ROPE_THETA = 10000.0
MOE_GROUPS = 4
EXPERTS_PER_GROUP = 4
N_EXPERTS = 16
EXPERT_FF = 256
EPS = 1e-6
NEG = -1e30
FORCE = 1e9

LANES = 128
VMEM_LIMIT = 56 * 1024 * 1024

_C_U = 0
_C_SQ = 256
_C_SKV = 512
_C_NQ = 1024
_C_NKV = 1536
_C_NG = 2304
_C_MG = 2432
_C_END = 5504


def _cparams(sem):
    return pltpu.CompilerParams(dimension_semantics=sem, vmem_limit_bytes=VMEM_LIMIT)


def _rope_slab(v, cos, sin_signed):
    lane = lax.broadcasted_iota(jnp.int32, v.shape, 1)
    first = (lane % HEAD_DIM) < (HEAD_DIM // 2)
    swapped = jnp.where(first, pltpu.roll(v, LANES - HEAD_DIM // 2, 1), pltpu.roll(v, HEAD_DIM // 2, 1))
    return v * cos + swapped * sin_signed


def _inproj_kernel(x_ref, shift_ref, scale_ref, gn_ref, cos_ref, sin_ref, w_ref,
                   u_ref, sq_ref, skv_ref, qn_ref, qr_ref, nsa_ref, win_ref, ng_ref, g_ref):
    x = x_ref[...]
    ms = jnp.mean(x * x, axis=-1, keepdims=True)
    h = x * lax.rsqrt(ms + EPS) * gn_ref[...]
    h = h * (1.0 + scale_ref[0]) + shift_ref[0]
    hb = h.astype(BF16)

    def mm(lo, hi):
        return jnp.dot(hb, w_ref[:, lo:hi], preferred_element_type=F32)

    cos = cos_ref[...]
    sin = sin_ref[...]
    u_ref[...] = mm(_C_U, _C_SQ)
    sq_ref[...] = mm(_C_SQ, _C_SKV)
    skv_ref[...] = mm(_C_SKV, _C_NQ)
    q = mm(_C_NQ, _C_NKV) * (1.0 / math.sqrt(HEAD_DIM))
    qn_ref[...] = q
    for s in range(NSA_WIDTH // LANES):
        qr_ref[:, s * LANES:(s + 1) * LANES] = _rope_slab(q[:, s * LANES:(s + 1) * LANES], cos, sin)
    kv = mm(_C_NKV, _C_NG)
    nsa_ref[:, 0:256] = kv[:, 0:256]
    nsa_ref[:, 256:384] = _rope_slab(kv[:, 256:384], cos, sin)
    nsa_ref[:, 384:512] = kv[:, 384:512]
    win_ref[:, 0:128] = _rope_slab(kv[:, 512:640], cos, sin)
    win_ref[:, 128:256] = kv[:, 640:768]
    ng_ref[...] = jax.nn.sigmoid(mm(_C_NG, _C_MG))
    g_ref[...] = jax.nn.sigmoid(mm(_C_MG, _C_END))


def _inproj(x, shift, scale, gn, cos, sin, w, tm, mod_map, pos_map):
    n = x.shape[0]
    mrows = shift.shape[1]
    row = lambda width: pl.BlockSpec((tm, width), lambda i: (i, 0))
    widths = (256, 256, 512, 512, 512, 512, 256, 128, 3072)
    return pl.pallas_call(
        _inproj_kernel,
        grid=(n // tm,),
        in_specs=[
            row(D_MODEL),
            pl.BlockSpec((1, mrows, D_MODEL), lambda i: (mod_map(i), 0, 0)),
            pl.BlockSpec((1, mrows, D_MODEL), lambda i: (mod_map(i), 0, 0)),
            pl.BlockSpec((1, D_MODEL), lambda i: (0, 0)),
            pl.BlockSpec((tm, LANES), lambda i: (pos_map(i), 0)),
            pl.BlockSpec((tm, LANES), lambda i: (pos_map(i), 0)),
            pl.BlockSpec((D_MODEL, _C_END), lambda i: (0, 0)),
        ],
        out_specs=[row(wd) for wd in widths],
        out_shape=[jax.ShapeDtypeStruct((n, wd), F32) for wd in widths],
        compiler_params=_cparams(("parallel",)),
        name="inproj",
    )(x, shift, scale, gn, cos, sin, w)


def _merge_kernel(x_ref, gate_ref, ya_ref, yb_ref, yc_ref, g_ref, wglu_ref, bglu_ref, wb_ref, wo_ref, o_ref):
    def mm(a, w):
        return jnp.dot(a.astype(BF16), w, preferred_element_type=F32)

    ya = jax.nn.gelu(ya_ref[...])
    ya = ya * jax.nn.sigmoid(mm(ya, wglu_ref[...]) + bglu_ref[...])
    br_a = mm(ya, wb_ref[0:SSM_WIDTH, :])
    br_b = mm(yb_ref[...], wb_ref[SSM_WIDTH:SSM_WIDTH + SB_WIDTH, :])
    br_c = mm(yc_ref[...], wb_ref[SSM_WIDTH + SB_WIDTH:, :])
    merged = (g_ref[:, 0:D_MODEL] * br_a + g_ref[:, D_MODEL:2 * D_MODEL] * br_b
              + g_ref[:, 2 * D_MODEL:] * br_c)
    out = mm(merged, wo_ref[...])
    o_ref[...] = x_ref[...] + gate_ref[0] * out


def _merge(x, gate, ya, yb, yc, g, wglu, bglu, wb, wo, tm, mod_map):
    n = x.shape[0]
    mrows = gate.shape[1]
    row = lambda width: pl.BlockSpec((tm, width), lambda i: (i, 0))
    return pl.pallas_call(
        _merge_kernel,
        grid=(n // tm,),
        in_specs=[
            row(D_MODEL),
            pl.BlockSpec((1, mrows, D_MODEL), lambda i: (mod_map(i), 0, 0)),
            row(SSM_WIDTH), row(SB_WIDTH), row(NSA_WIDTH), row(N_BRANCH * D_MODEL),
            pl.BlockSpec((SSM_WIDTH, SSM_WIDTH), lambda i: (0, 0)),
            pl.BlockSpec((1, SSM_WIDTH), lambda i: (0, 0)),
            pl.BlockSpec((D_MODEL, D_MODEL), lambda i: (0, 0)),
            pl.BlockSpec((D_MODEL, D_MODEL), lambda i: (0, 0)),
        ],
        out_specs=row(D_MODEL),
        out_shape=jax.ShapeDtypeStruct((n, D_MODEL), F32),
        compiler_params=_cparams(("parallel",)),
        name="merge",
    )(x, gate, ya, yb, yc, g, wglu, bglu, wb, wo)


def _route(logits):
    lane = lax.broadcasted_iota(jnp.int32, logits.shape, 1)
    big = jnp.int32(1 << 20)
    is_g = lane < MOE_GROUPS
    gl = jnp.where(is_g, logits, NEG)
    gmax = jnp.max(gl, axis=-1, keepdims=True)
    p_sel = 1.0 / jnp.sum(jnp.where(is_g, jnp.exp(gl - gmax), 0.0), axis=-1, keepdims=True)
    g_sel = jnp.min(jnp.where(is_g & (gl == gmax), lane, big), axis=-1, keepdims=True)
    e_idx = lane - MOE_GROUPS
    in_grp = (e_idx >= g_sel * EXPERTS_PER_GROUP) & (e_idx < (g_sel + 1) * EXPERTS_PER_GROUP)
    el = jnp.where(in_grp, logits, NEG)
    m1 = jnp.max(el, axis=-1, keepdims=True)
    i1 = jnp.min(jnp.where(in_grp & (el == m1), lane, big), axis=-1, keepdims=True)
    el2 = jnp.where(lane == i1, NEG, el)
    m2 = jnp.max(el2, axis=-1, keepdims=True)
    i2 = jnp.min(jnp.where(in_grp & (lane != i1) & (el2 == m2), lane, big), axis=-1, keepdims=True)
    e2 = jnp.exp(m2 - m1)
    w1 = p_sel / (1.0 + e2)
    w2 = p_sel * e2 / (1.0 + e2)
    return jnp.where(lane == i1, w1, jnp.where(lane == i2, w2, 0.0))


def _ffn_kernel(x_ref, shift_ref, scale_ref, gate_ref, gn_ref, wr_ref, br_ref, wg_ref, wu_ref, wd_ref,
                o_ref, h_scr, comb_scr, acc_scr):
    e = pl.program_id(1)

    @pl.when(e == 0)
    def _():
        x = x_ref[...]
        ms = jnp.mean(x * x, axis=-1, keepdims=True)
        h = x * lax.rsqrt(ms + EPS) * gn_ref[...]
        h = h * (1.0 + scale_ref[0]) + shift_ref[0]
        logits = jnp.dot(h, wr_ref[...], preferred_element_type=F32,
                         precision=lax.Precision.HIGHEST) + br_ref[...]
        comb_scr[...] = _route(logits)
        h_scr[...] = h.astype(BF16)
        acc_scr[...] = jnp.zeros_like(acc_scr)

    hb = h_scr[...]
    a = jnp.dot(hb, wg_ref[0], preferred_element_type=F32)
    b = jnp.dot(hb, wu_ref[0], preferred_element_type=F32)
    comb = comb_scr[...]
    lane = lax.broadcasted_iota(jnp.int32, comb.shape, 1)
    c = jnp.sum(jnp.where(lane == e + MOE_GROUPS, comb, 0.0), axis=-1, keepdims=True)
    hid = (a * jax.nn.sigmoid(a)) * b * c
    acc_scr[...] += jnp.dot(hid.astype(BF16), wd_ref[0], preferred_element_type=F32)

    @pl.when(e == N_EXPERTS - 1)
    def _():
        o_ref[...] = x_ref[...] + gate_ref[0] * acc_scr[...]


def _ffn(x, shift, scale, gate, gn, wr, br, wg, wu, wd, tm, mod_map):
    n = x.shape[0]
    mrows = shift.shape[1]
    mod = pl.BlockSpec((1, mrows, D_MODEL), lambda i, e: (mod_map(i), 0, 0))
    return pl.pallas_call(
        _ffn_kernel,
        grid=(n // tm, N_EXPERTS),
        in_specs=[
            pl.BlockSpec((tm, D_MODEL), lambda i, e: (i, 0)),
            mod, mod, mod,
            pl.BlockSpec((1, D_MODEL), lambda i, e: (0, 0)),
            pl.BlockSpec((D_MODEL, LANES), lambda i, e: (0, 0)),
            pl.BlockSpec((1, LANES), lambda i, e: (0, 0)),
            pl.BlockSpec((1, D_MODEL, EXPERT_FF), lambda i, e: (e, 0, 0)),
            pl.BlockSpec((1, D_MODEL, EXPERT_FF), lambda i, e: (e, 0, 0)),
            pl.BlockSpec((1, EXPERT_FF, D_MODEL), lambda i, e: (e, 0, 0)),
        ],
        out_specs=pl.BlockSpec((tm, D_MODEL), lambda i, e: (i, 0)),
        out_shape=jax.ShapeDtypeStruct((n, D_MODEL), F32),
        scratch_shapes=[pltpu.VMEM((tm, D_MODEL), BF16), pltpu.VMEM((tm, LANES), F32),
                        pltpu.VMEM((tm, D_MODEL), F32)],
        compiler_params=_cparams(("parallel", "arbitrary")),
        name="ffn_moe",
    )(x, shift, scale, gate, gn, wr, br, wg, wu, wd)


def _final_norm_kernel(x_ref, g_ref, o_ref):
    x = x_ref[...]
    o_ref[...] = x * lax.rsqrt(jnp.mean(x * x, axis=-1, keepdims=True) + EPS) * g_ref[...]


def _final_norm(x, g, tm):
    n = x.shape[0]
    return pl.pallas_call(
        _final_norm_kernel,
        grid=(n // tm,),
        in_specs=[pl.BlockSpec((tm, D_MODEL), lambda i: (i, 0)), pl.BlockSpec((1, D_MODEL), lambda i: (0, 0))],
        out_specs=pl.BlockSpec((tm, D_MODEL), lambda i: (i, 0)),
        out_shape=jax.ShapeDtypeStruct((n, D_MODEL), F32),
        compiler_params=_cparams(("parallel",)),
        name="final_norm",
    )(x, g)


S5_CHUNK = 64
S5_LW = S5_CHUNK * SSM_GROUP


def _s5_mats(a_re, a_im, log_dt, b_re, b_im, c_re, c_im):
    L = S5_CHUNK
    hp = lax.Precision.HIGHEST
    dt = jnp.exp(log_dt)[:, None]
    k = jnp.arange(L + 1, dtype=F32)[:, None, None]
    mag = jnp.exp(a_re * dt * k)
    ang = a_im * dt * k
    lk_re, lk_im = mag * jnp.cos(ang), mag * jnp.sin(ang)
    lb_re, lb_im = lk_re[1], lk_im[1]
    den = a_re * a_re + a_im * a_im
    co_re = ((lb_re - 1.0) * a_re + lb_im * a_im) / den
    co_im = (lb_im * a_re - (lb_re - 1.0) * a_im) / den
    bb_re = co_re[..., None] * b_re - co_im[..., None] * b_im
    bb_im = co_re[..., None] * b_im + co_im[..., None] * b_re
    cl_re = c_re[None] * lk_re[:, :, None, :] - c_im[None] * lk_im[:, :, None, :]
    cl_im = c_re[None] * lk_im[:, :, None, :] + c_im[None] * lk_re[:, :, None, :]
    kk = (jnp.einsum('kgpn,gnq->kgpq', cl_re[:L], bb_re, precision=hp)
          - jnp.einsum('kgpn,gnq->kgpq', cl_im[:L], bb_im, precision=hp))
    kq = kk.transpose(1, 3, 0, 2).reshape(SSM_GROUPS, SSM_GROUP, S5_LW).astype(BF16)
    kp = jnp.concatenate([jnp.zeros_like(kq), kq], axis=-1)
    tmat = jnp.stack([kp[:, :, S5_LW - SSM_GROUP * i:2 * S5_LW - SSM_GROUP * i] for i in range(L)], axis=1)
    tmat = tmat.reshape(SSM_GROUPS, S5_LW, S5_LW)
    rev = L - 1 - jnp.arange(L)
    be_re = lk_re[rev][:, :, :, None] * bb_re[None] - lk_im[rev][:, :, :, None] * bb_im[None]
    be_im = lk_re[rev][:, :, :, None] * bb_im[None] + lk_im[rev][:, :, :, None] * bb_re[None]
    bmat = jnp.concatenate([be_re.transpose(1, 0, 3, 2).reshape(SSM_GROUPS, S5_LW, SSM_STATE),
                            be_im.transpose(1, 0, 3, 2).reshape(SSM_GROUPS, S5_LW, SSM_STATE)], axis=-1)
    cm_re = cl_re[1:].transpose(1, 3, 0, 2).reshape(SSM_GROUPS, SSM_STATE, S5_LW)
    cm_im = cl_im[1:].transpose(1, 3, 0, 2).reshape(SSM_GROUPS, SSM_STATE, S5_LW)
    cmat = jnp.concatenate([cm_re, -cm_im], axis=1)
    lbl = jnp.stack([lk_re[L], lk_im[L]], axis=1)
    return tmat.astype(BF16), bmat.astype(BF16), cmat.astype(BF16), lbl


def _s5_kernel(x_ref, t_ref, b_ref, c_ref, lbl_ref, d_ref, y_ref, sf_ref, er_scr, ei_scr, pr_scr, pi_scr):
    nb, _, nch, _ = x_ref.shape
    a_re = lbl_ref[0, 0:1, :]
    a_im = lbl_ref[0, 1:2, :]
    for b in range(nb):
        x = x_ref[b, 0]
        xb = x.astype(BF16)
        e = jnp.dot(xb, b_ref[0], preferred_element_type=F32)
        er_scr[...] = e[:, :SSM_STATE]
        ei_scr[...] = e[:, SSM_STATE:]

        def step(c, carry):
            s_re, s_im = carry
            pr_scr[pl.ds(c, 1), :] = s_re
            pi_scr[pl.ds(c, 1), :] = s_im
            n_re = a_re * s_re - a_im * s_im + er_scr[pl.ds(c, 1), :]
            n_im = a_re * s_im + a_im * s_re + ei_scr[pl.ds(c, 1), :]
            return n_re, n_im

        zero = jnp.zeros((1, SSM_STATE), F32)
        s_re, s_im = lax.fori_loop(0, nch, step, (zero, zero))
        sprev = jnp.concatenate([pr_scr[...], pi_scr[...]], axis=1).astype(BF16)
        y = (jnp.dot(xb, t_ref[0], preferred_element_type=F32)
             + jnp.dot(sprev, c_ref[0], preferred_element_type=F32) + d_ref[0] * x)
        y_ref[b, 0] = y
        sf_ref[b, 0] = jnp.concatenate([s_re, s_im], axis=1)


def _s5_prompt(u, mats, d):
    tmat, bmat, cmat, lbl = mats
    bsz, t = u.shape[:2]
    nch = t // S5_CHUNK
    x = u.reshape(bsz, nch, S5_CHUNK, SSM_GROUPS, SSM_GROUP).transpose(0, 3, 1, 2, 4).reshape(
        bsz, SSM_GROUPS, nch, S5_LW)
    dt = jnp.tile(d, (1, S5_CHUNK))[:, None, :]
    y, sf = pl.pallas_call(
        _s5_kernel,
        grid=(SSM_GROUPS,),
        in_specs=[
            pl.BlockSpec((bsz, 1, nch, S5_LW), lambda g: (0, g, 0, 0)),
            pl.BlockSpec((1, S5_LW, S5_LW), lambda g: (g, 0, 0)),
            pl.BlockSpec((1, S5_LW, 2 * SSM_STATE), lambda g: (g, 0, 0)),
            pl.BlockSpec((1, 2 * SSM_STATE, S5_LW), lambda g: (g, 0, 0)),
            pl.BlockSpec((1, 2, SSM_STATE), lambda g: (g, 0, 0)),
            pl.BlockSpec((1, 1, S5_LW), lambda g: (g, 0, 0)),
        ],
        out_specs=[pl.BlockSpec((bsz, 1, nch, S5_LW), lambda g: (0, g, 0, 0)),
                   pl.BlockSpec((bsz, 1, 1, 2 * SSM_STATE), lambda g: (0, g, 0, 0))],
        out_shape=[jax.ShapeDtypeStruct((bsz, SSM_GROUPS, nch, S5_LW), F32),
                   jax.ShapeDtypeStruct((bsz, SSM_GROUPS, 1, 2 * SSM_STATE), F32)],
        scratch_shapes=[pltpu.VMEM((nch, SSM_STATE), F32)] * 4,
        compiler_params=_cparams(("arbitrary",)),
        name="s5_prompt",
    )(x, tmat, bmat, cmat, lbl, dt)
    y = y.reshape(bsz, SSM_GROUPS, nch, S5_CHUNK, SSM_GROUP).transpose(0, 2, 3, 1, 4).reshape(bsz, t, SSM_WIDTH)
    sf = sf.reshape(bsz, SSM_GROUPS, 2, SSM_STATE).transpose(0, 2, 1, 3)
    return y, sf


SB_TK = 128
SB_TQ = 256
SB_PER = SB_TQ // SB_TK


def _sb_tri():
    j = np.arange(SB_TK)[:, None]
    s = np.arange(SB_TK)[None, :]
    return jnp.asarray(-(s > j).astype(np.float32), dtype=BF16)


def _sb_kernel(qt_ref, k_ref, vt_ref, tri_ref, o_ref, acc_scr, c_scr):
    qi = pl.program_id(1)
    tri = tri_ref[...]
    kidx = lax.broadcasted_iota(jnp.int32, (SB_TK, SB_TQ), 0)
    qidx = lax.broadcasted_iota(jnp.int32, (SB_TK, SB_TQ), 1)

    def softplus_tile(h, j, mask):
        kt = k_ref[0, h, pl.ds(pl.multiple_of(j * SB_TK, SB_TK), SB_TK), :]
        z = jnp.dot(kt, qt_ref[0, h], preferred_element_type=F32)
        sp = jnp.maximum(z, 0.0) + jnp.log(1.0 + jnp.exp(-jnp.abs(z)))
        if mask is not None:
            sp = jnp.where(mask, sp, 0.0)
        return z, sp

    def weights(z, sp, c, mask):
        r = jnp.dot(tri, sp.astype(BF16), preferred_element_type=F32)
        w = jnp.exp(z - sp + r - c)
        if mask is not None:
            w = jnp.where(mask, w, 0.0)
        return w.astype(BF16)

    def two_tiles(j_hi, masks):
        heads = range(SB_HEADS)
        zs1 = [softplus_tile(h, j_hi, masks[0]) for h in heads]
        zs0 = [softplus_tile(h, j_hi - 1, masks[1]) for h in heads]
        cs = [c_scr[h] for h in heads]
        t1 = [jnp.sum(zs1[h][1], axis=0, keepdims=True) for h in heads]
        w1 = [weights(zs1[h][0], zs1[h][1], cs[h], masks[0]) for h in heads]
        w0 = [weights(zs0[h][0], zs0[h][1], cs[h] + t1[h], masks[1]) for h in heads]
        for h in heads:
            acc_scr[h] += (jnp.dot(vt_ref[0, h, j_hi], w1[h], preferred_element_type=F32)
                           + jnp.dot(vt_ref[0, h, j_hi - 1], w0[h], preferred_element_type=F32))
            c_scr[h] = cs[h] + t1[h] + jnp.sum(zs0[h][1], axis=0, keepdims=True)

    acc_scr[...] = jnp.zeros_like(acc_scr)
    c_scr[...] = jnp.zeros_like(c_scr)
    two_tiles(SB_PER * qi + 1, (kidx + SB_TK < qidx, kidx < qidx))

    def body(p, carry):
        two_tiles(SB_PER * (qi - 1 - p) + 1, (None, None))
        return carry

    lax.fori_loop(0, qi, body, 0)
    o_ref[0] = acc_scr[...]


def _sb_prompt(sq, skv, bsz, t):
    assert SB_PER == 2
    nt = t // SB_TK
    nq = t // SB_TQ
    qt = (sq * (1.0 / math.sqrt(HEAD_DIM))).reshape(bsz, t, SB_HEADS, HEAD_DIM).transpose(0, 2, 3, 1).astype(BF16)
    k = skv[:, :SB_WIDTH].reshape(bsz, t, SB_HEADS, HEAD_DIM).transpose(0, 2, 1, 3).astype(BF16)
    vt = skv[:, SB_WIDTH:].reshape(bsz, nt, SB_TK, SB_HEADS, HEAD_DIM).transpose(0, 3, 1, 4, 2).astype(BF16)
    yt = pl.pallas_call(
        _sb_kernel,
        grid=(bsz, nq),
        in_specs=[
            pl.BlockSpec((1, SB_HEADS, HEAD_DIM, SB_TQ), lambda b, i: (b, 0, 0, i)),
            pl.BlockSpec((1, SB_HEADS, t, HEAD_DIM), lambda b, i: (b, 0, 0, 0)),
            pl.BlockSpec((1, SB_HEADS, nt, HEAD_DIM, SB_TK), lambda b, i: (b, 0, 0, 0, 0)),
            pl.BlockSpec((SB_TK, SB_TK), lambda b, i: (0, 0)),
        ],
        out_specs=pl.BlockSpec((1, SB_HEADS, HEAD_DIM, SB_TQ), lambda b, i: (b, 0, 0, i)),
        out_shape=jax.ShapeDtypeStruct((bsz, SB_HEADS, HEAD_DIM, t), F32),
        scratch_shapes=[pltpu.VMEM((SB_HEADS, HEAD_DIM, SB_TQ), F32), pltpu.VMEM((SB_HEADS, 1, SB_TQ), F32)],
        compiler_params=_cparams(("parallel", "arbitrary")),
        name="sb_prompt",
    )(qt, k, vt, _sb_tri())
    return yt.transpose(0, 3, 1, 2).reshape(bsz * t, SB_WIDTH)


NSA_TQ = 128
NSA_TQC = 512
NSA_KC = 512
MASK_BIG = 2.0 ** 100
CMP_CHUNK_W = CMP_STRIDE * 2 * NSA_KV_WIDTH


def _cmp_weights(w_cmp, pe_cmp):
    r = CMP_LEN // CMP_STRIDE
    wj = w_cmp.reshape(2, r, CMP_STRIDE, HEAD_DIM, HEAD_DIM)
    eye = jnp.eye(2 * NSA_KV_HEADS, dtype=F32).reshape(2, NSA_KV_HEADS, 2, NSA_KV_HEADS)
    wb = jnp.einsum('kjsde,kgmh->jskgdmhe', wj, eye)
    wbig = wb.reshape(r, CMP_CHUNK_W, 2 * NSA_KV_WIDTH).transpose(1, 0, 2).reshape(CMP_CHUNK_W, r * 2 * NSA_KV_WIDTH)
    pj = pe_cmp.reshape(2, r, CMP_STRIDE, HEAD_DIM)
    pe = jnp.broadcast_to(pj.transpose(1, 2, 0, 3)[:, :, :, None, :],
                          (r, CMP_STRIDE, 2, NSA_KV_HEADS, HEAD_DIM)).reshape(r, CMP_CHUNK_W)
    pe8 = jnp.concatenate([pe, jnp.zeros((8 - r, CMP_CHUNK_W), F32)], axis=0)
    return wbig.astype(BF16), pe8.astype(BF16)


def _compress_kernel(x_ref, w_ref, pe_ref, o_ref):
    w = w_ref[...]
    width = 2 * NSA_KV_WIDTH
    a = jnp.dot(x_ref[0].astype(BF16), w, preferred_element_type=F32)
    pb = jnp.dot(pe_ref[...], w, preferred_element_type=F32)
    bias = pb[0:1, :width] + pb[1:2, width:]
    nch = a.shape[0]
    a1 = pltpu.roll(a[:, width:], nch - 1, 0)
    o_ref[0] = a[:, :width] + a1 + bias


def _nsa_compress(xc, wbig, pe8):
    bsz, nch, _ = xc.shape
    return pl.pallas_call(
        _compress_kernel,
        grid=(bsz,),
        in_specs=[pl.BlockSpec((1, nch, CMP_CHUNK_W), lambda b: (b, 0, 0)),
                  pl.BlockSpec(wbig.shape, lambda b: (0, 0)),
                  pl.BlockSpec(pe8.shape, lambda b: (0, 0))],
        out_specs=pl.BlockSpec((1, nch, 2 * NSA_KV_WIDTH), lambda b: (b, 0, 0)),
        out_shape=jax.ShapeDtypeStruct((bsz, nch, 2 * NSA_KV_WIDTH), F32),
        compiler_params=_cparams(("parallel",)),
        name="nsa_compress",
    )(xc, wbig, pe8)


def _split_hi_lo(x):
    hi = x.astype(BF16)
    lo = (x - hi.astype(F32)).astype(BF16)
    return jnp.concatenate([hi, lo], axis=1)


def _topk_unselected(imp, cur):
    blk = lax.broadcasted_iota(jnp.int32, imp.shape, 1)
    imp = jnp.where(blk == cur, FORCE, jnp.where(blk < cur, imp, -1.0))
    unsel = jnp.ones(imp.shape, F32)
    for _ in range(SEL_TOPK):
        m = jnp.max(imp, axis=-1, keepdims=True)
        idx = jnp.min(jnp.where(imp == m, blk, LANES), axis=-1, keepdims=True)
        hit = blk == idx
        unsel = jnp.where(hit, 0.0, unsel)
        imp = jnp.where(hit, -3.0e38, imp)
    return jnp.where(blk <= cur, unsel, 1.0)


def _nsa_cmp_kernel(qn_ref, kct_ref, vc_ref, cmap_ref, ocmp_ref, unsel_ref):
    tq = qn_ref.shape[0]
    ncp = kct_ref.shape[3]
    q0 = pl.program_id(2) * tq
    qpos = q0 + lax.broadcasted_iota(jnp.int32, (tq, 1), 0)
    cidx = lax.broadcasted_iota(jnp.int32, (tq, ncp), 1)
    mc = (cidx * CMP_STRIDE + (CMP_LEN - 1) <= qpos) & (cidx < ncp - 1)
    kct = kct_ref[0, 0]
    vc = vc_ref[0, 0]
    psum = jnp.zeros((tq, ncp), F32)
    for r in range(NSA_REP):
        q = qn_ref[:, r * HEAD_DIM:(r + 1) * HEAD_DIM].astype(BF16)
        s = jnp.where(mc, jnp.dot(q, kct, preferred_element_type=F32), NEG)
        m = jnp.max(s, axis=-1, keepdims=True)
        e = jnp.exp(s - m)
        p = jnp.where(mc, e / jnp.sum(e, axis=-1, keepdims=True), 0.0)
        ocmp_ref[:, r * HEAD_DIM:(r + 1) * HEAD_DIM] = jnp.dot(p.astype(BF16), vc, preferred_element_type=F32)
        psum = psum + p
    imp = jnp.dot(_split_hi_lo(psum), cmap_ref[...], preferred_element_type=F32)
    unsel = _topk_unselected(imp, qpos // SEL_LEN)
    for i in range(tq // NSA_TQ):
        unsel_ref[0, i] = unsel[i * NSA_TQ:(i + 1) * NSA_TQ].T


def _nsa_cmp(qn, kct, vc, cmap2, bsz, t):
    nqt = t // NSA_TQC
    ncp = kct.shape[3]
    return pl.pallas_call(
        _nsa_cmp_kernel,
        grid=(bsz, NSA_KV_HEADS, nqt),
        in_specs=[
            pl.BlockSpec((NSA_TQC, NSA_REP * HEAD_DIM), lambda b, g, i: (b * nqt + i, g)),
            pl.BlockSpec((1, 1, HEAD_DIM, ncp), lambda b, g, i: (b, g, 0, 0)),
            pl.BlockSpec((1, 1, ncp, HEAD_DIM), lambda b, g, i: (b, g, 0, 0)),
            pl.BlockSpec(cmap2.shape, lambda b, g, i: (0, 0)),
        ],
        out_specs=[pl.BlockSpec((NSA_TQC, NSA_REP * HEAD_DIM), lambda b, g, i: (b * nqt + i, g)),
                   pl.BlockSpec((1, NSA_TQC // NSA_TQ, LANES, NSA_TQ), lambda b, g, i: (g, b * nqt + i, 0, 0))],
        out_shape=[jax.ShapeDtypeStruct((bsz * t, NSA_WIDTH), F32),
                   jax.ShapeDtypeStruct((NSA_KV_HEADS, bsz * t // NSA_TQ, LANES, NSA_TQ), F32)],
        compiler_params=_cparams(("parallel", "parallel", "arbitrary")),
        name="nsa_cmp_select",
    )(qn, kct, vc, cmap2)


NSA_COLS = NSA_REP * NSA_TQ
NSA_KAUG = HEAD_DIM + LANES


def _nsa_selt_kernel(qt_ref, unsel_ref, ocmp_ref, gate_ref, ksa_ref, vst_ref, vsd_ref, kw_ref, vwt_ref, gexp_ref,
                     y_ref, acc_scr):
    tq = NSA_TQ
    qi = pl.program_id(2)
    qt = qt_ref[0, 0, 0]
    unsel = unsel_ref[0, 0]
    krow = lax.broadcasted_iota(jnp.int32, (tq, tq), 0)
    qcol = lax.broadcasted_iota(jnp.int32, (tq, tq), 1)

    def cols4(x):
        return jnp.concatenate([x] * NSA_REP, axis=1)

    u0 = unsel_ref[0, 0, pl.ds(2 * qi, 1), :]
    u1 = unsel_ref[0, 0, pl.ds(2 * qi + 1, 1), :]
    ud = jnp.where(krow < SEL_LEN, u0, u1)
    bias_d = jnp.where((ud < 0.5) & (krow <= qcol), 0.0, -MASK_BIG)
    kd = ksa_ref[0, 0, pl.ds(pl.multiple_of(qi * tq, tq), tq), 0:HEAD_DIM]
    s = jnp.dot(kd, qt, preferred_element_type=F32) + cols4(bias_d)
    m0 = jnp.max(s, axis=0, keepdims=True)
    p = jnp.exp(s - m0)
    l0 = jnp.sum(p, axis=0, keepdims=True)
    acc_scr[...] = jnp.dot(vsd_ref[0, 0, qi], p.astype(BF16), preferred_element_type=F32)

    unsel_past = jnp.where(krow >= 2 * qi, 1.0, unsel).astype(BF16)
    rhs = jnp.concatenate([qt, cols4(unsel_past)], axis=0)

    def chunk(c, carry):
        m_old, l_old = carry
        ka = ksa_ref[0, 0, pl.ds(pl.multiple_of(c * NSA_KC, NSA_KC), NSA_KC), :]
        s = jnp.dot(ka, rhs, preferred_element_type=F32)
        m_new = jnp.maximum(m_old, jnp.max(s, axis=0, keepdims=True))
        alpha = jnp.exp(m_old - m_new)
        p = jnp.exp(s - m_new)
        acc_scr[...] = acc_scr[...] * alpha + jnp.dot(vst_ref[0, 0, c], p.astype(BF16),
                                                      preferred_element_type=F32)
        return m_new, l_old * alpha + jnp.sum(p, axis=0, keepdims=True)

    _, l_sel = lax.fori_loop(0, (qi * tq + NSA_KC - 1) // NSA_KC, chunk, (m0, l0))
    o_sel_t = acc_scr[...] / l_sel

    ntw = (WINDOW + tq) // tq
    t0 = jnp.maximum(qi - WINDOW // tq, 0)
    kw = kw_ref[0, 0, pl.ds(pl.multiple_of(t0 * tq, tq), ntw * tq), :]
    kpos = t0 * tq + lax.broadcasted_iota(jnp.int32, (ntw * tq, tq), 0)
    qpw = qi * tq + lax.broadcasted_iota(jnp.int32, (ntw * tq, tq), 1)
    bias_w = jnp.where((kpos <= qpw) & (kpos > qpw - WINDOW), 0.0, NEG)
    sw = jnp.dot(kw, qt, preferred_element_type=F32) + cols4(bias_w)
    pw = jnp.exp(sw - jnp.max(sw, axis=0, keepdims=True))
    lw = jnp.sum(pw, axis=0, keepdims=True)
    pwb = pw.astype(BF16)
    o_win_t = jnp.zeros((HEAD_DIM, NSA_COLS), F32)
    for i in range(ntw):
        o_win_t = o_win_t + jnp.dot(vwt_ref[0, 0, t0 + i], pwb[i * tq:(i + 1) * tq], preferred_element_type=F32)
    o_win_t = o_win_t / lw

    both = [jnp.concatenate([o_sel_t[:, r * tq:(r + 1) * tq], o_win_t[:, r * tq:(r + 1) * tq]], axis=0).T
            for r in range(NSA_REP)]
    o_sel = jnp.concatenate([x[:, :HEAD_DIM] for x in both], axis=1)
    o_win = jnp.concatenate([x[:, HEAD_DIM:] for x in both], axis=1)
    ge = jnp.dot(_split_hi_lo(gate_ref[...]), gexp_ref[0], preferred_element_type=F32)
    w = NSA_REP * HEAD_DIM
    y_ref[...] = ge[:, 0:w] * ocmp_ref[...] + ge[:, w:2 * w] * o_sel + ge[:, 2 * w:3 * w] * o_win


def _nsa_selt(qt, unsel_t, ocmp, gates, ksa, vst, vsd, kw, vwt, gexp, bsz, t):
    nq = t // NSA_TQ
    w = NSA_REP * HEAD_DIM
    full = lambda a: pl.BlockSpec((1, 1) + a.shape[2:], lambda b, g, i: (b, g) + (0,) * (a.ndim - 2))
    return pl.pallas_call(
        _nsa_selt_kernel,
        grid=(bsz, NSA_KV_HEADS, nq),
        in_specs=[
            pl.BlockSpec((1, 1, 1, HEAD_DIM, NSA_COLS), lambda b, g, i: (b, g, i, 0, 0)),
            pl.BlockSpec((1, 1, LANES, NSA_TQ), lambda b, g, i: (g, b * nq + i, 0, 0)),
            pl.BlockSpec((NSA_TQ, w), lambda b, g, i: (b * nq + i, g)),
            pl.BlockSpec((NSA_TQ, LANES), lambda b, g, i: (b * nq + i, 0)),
            full(ksa), full(vst), full(vsd), full(kw), full(vwt),
            pl.BlockSpec((1,) + gexp.shape[1:], lambda b, g, i: (g, 0, 0)),
        ],
        out_specs=pl.BlockSpec((NSA_TQ, w), lambda b, g, i: (b * nq + i, g)),
        out_shape=jax.ShapeDtypeStruct((bsz * t, NSA_WIDTH), F32),
        scratch_shapes=[pltpu.VMEM((HEAD_DIM, NSA_COLS), F32)],
        compiler_params=_cparams(("parallel", "parallel", "arbitrary")),
        name="nsa_sel_win",
    )(qt, unsel_t, ocmp, gates, ksa, vst, vsd, kw, vwt, gexp)


def _nsa_consts(t):
    ncp = t // CMP_STRIDE
    ns = t // SEL_LEN
    cs = np.arange(ncp) * CMP_STRIDE
    ss = np.arange(LANES) * SEL_LEN
    ov = np.clip(np.minimum(cs[:, None] + CMP_LEN, ss[None, :] + SEL_LEN) - np.maximum(cs[:, None], ss[None, :]), 0, None)
    cmap = (ov / CMP_LEN).astype(np.float32)
    cmap[ncp - 1:, :] = 0.0
    cmap[:, ns:] = 0.0
    cmap2 = jnp.asarray(np.concatenate([cmap, cmap], axis=0), dtype=BF16)
    key_blk = (np.arange(t) // SEL_LEN)[:, None]
    bige = jnp.asarray(np.where(np.arange(LANES)[None, :] == key_blk, -MASK_BIG, 0.0), dtype=BF16)
    ge = np.zeros((NSA_KV_HEADS, LANES, 3 * NSA_REP * HEAD_DIM), np.float32)
    for g in range(NSA_KV_HEADS):
        for r in range(NSA_REP):
            for j in range(3):
                c0 = j * NSA_REP * HEAD_DIM + r * HEAD_DIM
                ge[g, (g * NSA_REP + r) * 3 + j, c0:c0 + HEAD_DIM] = 1.0
    gexp = jnp.asarray(np.concatenate([ge, ge], axis=1), dtype=BF16)
    return cmap2, bige, gexp


def _nsa_prompt(qn, qr, nsa, win, gates, wbig, pe8, consts, bsz, t):
    cmap2, bige, gexp = consts
    ncp = t // CMP_STRIDE
    xc = nsa[:, :2 * NSA_KV_WIDTH].reshape(bsz, ncp, CMP_CHUNK_W)
    kcv = _nsa_compress(xc, wbig, pe8).reshape(bsz, ncp, 2, NSA_KV_HEADS, HEAD_DIM)
    kct = kcv[:, :, 0].transpose(0, 2, 3, 1).astype(BF16)
    vc = kcv[:, :, 1].transpose(0, 2, 1, 3).astype(BF16)
    ocmp, unsel_t = _nsa_cmp(qn, kct, vc, cmap2, bsz, t)
    nq = t // NSA_TQ
    qt = qr.reshape(bsz, nq, NSA_TQ, NSA_KV_HEADS, NSA_REP, HEAD_DIM).transpose(0, 3, 1, 5, 4, 2)
    qt = qt.reshape(bsz, NSA_KV_HEADS, nq, HEAD_DIM, NSA_COLS).astype(BF16)

    def rows_major(x):
        return x.reshape(bsz, t, NSA_KV_HEADS, HEAD_DIM).transpose(0, 2, 1, 3).astype(BF16)

    def tiles_t(x, tk):
        return x.reshape(bsz, t // tk, tk, NSA_KV_HEADS, HEAD_DIM).transpose(0, 3, 1, 4, 2).astype(BF16)

    ksa = jnp.concatenate([rows_major(nsa[:, 256:384]),
                           jnp.broadcast_to(bige, (bsz, NSA_KV_HEADS, t, LANES))], axis=-1)
    vst = tiles_t(nsa[:, 384:512], NSA_KC)
    vsd = tiles_t(nsa[:, 384:512], NSA_TQ)
    kw = rows_major(win[:, :128])
    vwt = tiles_t(win[:, 128:], NSA_TQ)
    return _nsa_selt(qt, unsel_t, ocmp, gates, ksa, vst, vsd, kw, vwt, gexp, bsz, t)


PAGE = 128
N_PAGES = PAST_LEN // PAGE
HROWS = 8
NT_DIMS = (((1,), (1,)), ((), ()))


def _sb_tri_rows():
    s = np.arange(PAGE)[:, None]
    j = np.arange(2 * PAGE)[None, :]
    u = -((s > j) | (j >= PAGE)).astype(np.float32)
    return jnp.asarray(np.concatenate([u, u], axis=0), dtype=BF16)


def _pages_t(cache):
    dp, npool = cache.shape[:2]
    return cache.transpose(0, 1, 3, 4, 5, 2).reshape(dp * npool, -1, PAGE)


def _page_specs(feat, blk):
    return [pl.BlockSpec((1, feat, PAGE), lambda b, pt, p=p: (pt[b, p], blk, 0)) for p in range(N_PAGES)]


def _dec_sb_kernel(pt_ref, q_ref, *rest):
    pages = rest[:N_PAGES]
    tri_ref, dmask_ref, y_ref = rest[N_PAGES:]
    q8 = q_ref[0]
    z = jnp.concatenate(
        [jnp.dot(q8, pg[0, 0:SB_WIDTH, :].astype(BF16), preferred_element_type=F32) for pg in pages],
        axis=0)
    sp = jnp.maximum(z, 0.0) + jnp.log(1.0 + jnp.exp(-jnp.abs(z)))
    rt = jnp.dot(_split_hi_lo(sp), tri_ref[...], preferred_element_type=F32)
    cs = [None] * N_PAGES
    c = jnp.zeros((HROWS, PAGE), F32)
    for p in range(N_PAGES - 1, -1, -1):
        cs[p] = c
        c = c + rt[p * HROWS:(p + 1) * HROWS, PAGE:]
    w = jnp.exp(z - sp + rt[:, :PAGE] + jnp.concatenate(cs, axis=0)).astype(BF16)
    y8 = jnp.zeros((HROWS, SB_WIDTH), F32)
    for p, pg in enumerate(pages):
        y8 = y8 + lax.dot_general(w[p * HROWS:(p + 1) * HROWS], pg[0, SB_WIDTH:, :].astype(BF16), NT_DIMS,
                                  preferred_element_type=F32)
    y_ref[0] = jnp.sum(y8 * dmask_ref[...], axis=0, keepdims=True)


def _dec_sb(rows, sq, cache2d):
    bd = sq.shape[0]
    hm = np.zeros((HROWS, SB_WIDTH), np.float32)
    for h in range(SB_HEADS):
        hm[h, h * HEAD_DIM:(h + 1) * HEAD_DIM] = 1.0
    q8 = ((sq * (1.0 / math.sqrt(HEAD_DIM)))[:, None, :] * hm[None]).astype(BF16)
    grid_spec = pltpu.PrefetchScalarGridSpec(
        num_scalar_prefetch=1, grid=(bd,),
        in_specs=[pl.BlockSpec((1, HROWS, SB_WIDTH), lambda b, pt: (b, 0, 0))] + _page_specs(2 * SB_WIDTH, 0)
        + [pl.BlockSpec((2 * PAGE, 2 * PAGE), lambda b, pt: (0, 0)),
           pl.BlockSpec((HROWS, SB_WIDTH), lambda b, pt: (0, 0))],
        out_specs=pl.BlockSpec((1, 1, SB_WIDTH), lambda b, pt: (b, 0, 0)))
    y = pl.pallas_call(
        _dec_sb_kernel, grid_spec=grid_spec,
        out_shape=jax.ShapeDtypeStruct((bd, 1, SB_WIDTH), F32),
        compiler_params=_cparams(("arbitrary",)),
        name="dec_sb",
    )(rows, q8, *([cache2d] * N_PAGES), _sb_tri_rows(), jnp.asarray(hm))
    return y.reshape(bd, SB_WIDTH)


def _dec_cmp_kernel(pt_ref, q_ref, *rest):
    pages = rest[:N_PAGES]
    w_ref, pe_ref, cmap_ref, perm_ref, o_ref, imp_ref = rest[N_PAGES:]
    width = 2 * NSA_KV_WIDTH
    cpp = PAGE // CMP_STRIDE
    ncp = N_PAGES * cpp
    perm = perm_ref[...]
    pp = [lax.dot_general(perm, pg[0].astype(BF16), NT_DIMS, preferred_element_type=F32) for pg in pages]
    a = jnp.zeros((ncp, 2 * width), F32)
    for s in range(CMP_STRIDE):
        xs = jnp.concatenate([x[s * cpp:(s + 1) * cpp] for x in pp], axis=0)
        a = a + jnp.dot(xs.astype(BF16), w_ref[s * width:(s + 1) * width, :], preferred_element_type=F32)
    pb = jnp.dot(pe_ref[...], w_ref[...], preferred_element_type=F32)
    bias = pb[0:1, :width] + pb[1:2, width:]
    kcv = a[:, :width] + pltpu.roll(a[:, width:], ncp - 1, 0) + bias
    q8 = q_ref[0]
    s8 = lax.dot_general(q8, kcv[:, :NSA_KV_WIDTH].astype(BF16), NT_DIMS, preferred_element_type=F32)
    valid = lax.broadcasted_iota(jnp.int32, s8.shape, 1) < ncp - 1
    s8 = jnp.where(valid, s8, NEG)
    e = jnp.exp(s8 - jnp.max(s8, axis=-1, keepdims=True))
    p = jnp.where(valid, e / jnp.sum(e, axis=-1, keepdims=True), 0.0)
    o_ref[0] = jnp.dot(p.astype(BF16), kcv[:, NSA_KV_WIDTH:].astype(BF16), preferred_element_type=F32)
    row = lax.broadcasted_iota(jnp.int32, p.shape, 0)
    psum = jnp.where(row < NSA_REP, jnp.sum(p[:NSA_REP], axis=0, keepdims=True),
                     jnp.sum(p[NSA_REP:], axis=0, keepdims=True))
    imp_ref[0] = jnp.dot(_split_hi_lo(psum), cmap_ref[...], preferred_element_type=F32)


def _head_rows(x, bd):
    x = x.reshape(bd, NSA_KV_HEADS, NSA_REP, 1, HEAD_DIM)
    eye = jnp.eye(NSA_KV_HEADS, dtype=x.dtype).reshape(1, NSA_KV_HEADS, 1, NSA_KV_HEADS, 1)
    return (x * eye).reshape(bd, NSA_HEADS, NSA_KV_WIDTH)


def _dec_cmp(rows, qn, cache2d, wbig, pe8, cmap2):
    bd = qn.shape[0]
    q8 = _head_rows(qn, bd).astype(BF16)
    cpp = PAGE // CMP_STRIDE
    pm = np.zeros((PAGE, PAGE), np.float32)
    for s in range(CMP_STRIDE):
        for n in range(cpp):
            pm[s * cpp + n, CMP_STRIDE * n + s] = 1.0
    perm = jnp.asarray(pm, dtype=BF16)
    grid_spec = pltpu.PrefetchScalarGridSpec(
        num_scalar_prefetch=1, grid=(bd,),
        in_specs=[pl.BlockSpec((1, NSA_HEADS, NSA_KV_WIDTH), lambda b, pt: (b, 0, 0))]
        + _page_specs(2 * NSA_KV_WIDTH, 0)
        + [pl.BlockSpec(wbig.shape, lambda b, pt: (0, 0)), pl.BlockSpec(pe8.shape, lambda b, pt: (0, 0)),
           pl.BlockSpec(cmap2.shape, lambda b, pt: (0, 0)), pl.BlockSpec((PAGE, PAGE), lambda b, pt: (0, 0))],
        out_specs=[pl.BlockSpec((1, NSA_HEADS, NSA_KV_WIDTH), lambda b, pt: (b, 0, 0)),
                   pl.BlockSpec((1, NSA_HEADS, LANES), lambda b, pt: (b, 0, 0))])
    return pl.pallas_call(
        _dec_cmp_kernel, grid_spec=grid_spec,
        out_shape=[jax.ShapeDtypeStruct((bd, NSA_HEADS, NSA_KV_WIDTH), F32),
                   jax.ShapeDtypeStruct((bd, NSA_HEADS, LANES), F32)],
        compiler_params=_cparams(("arbitrary",)),
        name="dec_cmp",
    )(rows, q8, *([cache2d] * N_PAGES), wbig, pe8, cmap2, perm)


def _dec_topk_kernel(imp_ref, unsel_ref):
    unsel_ref[...] = _topk_unselected(imp_ref[...], PAST_LEN // SEL_LEN)


def _dec_topk(imp):
    return pl.pallas_call(
        _dec_topk_kernel,
        out_shape=jax.ShapeDtypeStruct(imp.shape, F32),
        name="dec_topk",
    )(imp)


def _dec_sel_kernel(pt_ref, q_ref, unsel_ref, ocmp_ref, gate_ref, new_ref, win_ref, nwin_ref, *rest):
    pages = rest[:N_PAGES]
    bige_ref, y_ref = rest[N_PAGES:]
    q8 = q_ref[0]
    q8f = q8.astype(F32)

    def new_key_score(krow):
        return jnp.sum(q8f * krow.astype(BF16).astype(F32), axis=-1, keepdims=True)

    def attend(s3, s_new, vals_t, v_new):
        m = jnp.maximum(jnp.max(jnp.max(s3, axis=0), axis=-1, keepdims=True), s_new)
        p3 = jnp.exp(s3 - m[None])
        p_new = jnp.exp(s_new - m)
        den = jnp.sum(jnp.sum(p3, axis=0), axis=-1, keepdims=True) + p_new
        o = p_new * v_new.astype(BF16).astype(F32)
        for i, vt in enumerate(vals_t):
            o = o + lax.dot_general(p3[i].astype(BF16), vt, NT_DIMS, preferred_element_type=F32)
        return o / den

    bias = jnp.dot(unsel_ref[0].astype(BF16), bige_ref[...], preferred_element_type=F32)
    s_sel = jnp.stack(
        [jnp.dot(q8, pg[0, 0:NSA_KV_WIDTH, :].astype(BF16), preferred_element_type=F32)
         + bias[:, p * PAGE:(p + 1) * PAGE] for p, pg in enumerate(pages)], axis=0)
    new = new_ref[0]
    o_sel = attend(s_sel, new_key_score(new[:, 2 * NSA_KV_WIDTH:3 * NSA_KV_WIDTH]),
                   [pg[0, NSA_KV_WIDTH:, :].astype(BF16) for pg in pages], new[:, 3 * NSA_KV_WIDTH:])

    nt = WINDOW // PAGE
    kpos0 = lax.broadcasted_iota(jnp.int32, (HROWS, PAGE), 1) == 0
    s_win = []
    for i in range(nt):
        s = jnp.dot(q8, win_ref[0, 0:NSA_KV_WIDTH, i * PAGE:(i + 1) * PAGE].astype(BF16),
                    preferred_element_type=F32)
        s_win.append(jnp.where(kpos0, NEG, s) if i == 0 else s)
    nwin = nwin_ref[0]
    o_win = attend(jnp.stack(s_win, axis=0), new_key_score(nwin[:, :NSA_KV_WIDTH]),
                   [win_ref[0, NSA_KV_WIDTH:, i * PAGE:(i + 1) * PAGE].astype(BF16) for i in range(nt)],
                   nwin[:, NSA_KV_WIDTH:])

    g = gate_ref[0]
    y8 = g[:, 0:LANES] * ocmp_ref[0] + g[:, LANES:2 * LANES] * o_sel + g[:, 2 * LANES:] * o_win
    for h in range(NSA_HEADS):
        c0 = (h // NSA_REP) * HEAD_DIM
        y_ref[0, :, h * HEAD_DIM:(h + 1) * HEAD_DIM] = y8[h:h + 1, c0:c0 + HEAD_DIM]


def _dec_sel(rows, qr, unsel, ocmp, gates, new_nsa, win_state, new_win, cache2d, bige):
    bd = qr.shape[0]
    q8 = _head_rows(qr, bd).astype(BF16)
    g3 = jnp.broadcast_to(gates[:, :3 * NSA_HEADS].reshape(bd, NSA_HEADS, 3, 1),
                          (bd, NSA_HEADS, 3, LANES)).reshape(bd, NSA_HEADS, 3 * LANES)
    per_b = lambda *shape: pl.BlockSpec((1,) + shape, lambda b, pt: (b,) + (0,) * len(shape))
    grid_spec = pltpu.PrefetchScalarGridSpec(
        num_scalar_prefetch=1, grid=(bd,),
        in_specs=[per_b(NSA_HEADS, NSA_KV_WIDTH), per_b(NSA_HEADS, LANES), per_b(NSA_HEADS, NSA_KV_WIDTH),
                  per_b(NSA_HEADS, 3 * LANES), per_b(1, 4 * NSA_KV_WIDTH), per_b(2 * NSA_KV_WIDTH, WINDOW),
                  per_b(1, 2 * NSA_KV_WIDTH)]
        + _page_specs(2 * NSA_KV_WIDTH, 1)
        + [pl.BlockSpec(bige.shape, lambda b, pt: (0, 0))],
        out_specs=per_b(1, NSA_WIDTH))
    y = pl.pallas_call(
        _dec_sel_kernel, grid_spec=grid_spec,
        out_shape=jax.ShapeDtypeStruct((bd, 1, NSA_WIDTH), F32),
        compiler_params=_cparams(("arbitrary",)),
        name="dec_sel_win",
    )(rows, q8, unsel, ocmp, g3, new_nsa.reshape(bd, 1, -1), win_state, new_win.reshape(bd, 1, -1),
      *([cache2d] * N_PAGES), bige)
    return y.reshape(bd, NSA_WIDTH)


def _dec_consts():
    cmap2 = _nsa_consts(PAST_LEN)[0]
    blk = np.arange(PAST_LEN) // SEL_LEN
    bige = jnp.asarray(np.where(np.arange(LANES)[:, None] == blk[None, :], -MASK_BIG, 0.0), dtype=BF16)
    return cmap2, bige


def _cplx_affine(e1, e2):
    a1r, a1i, b1r, b1i = e1
    a2r, a2i, b2r, b2i = e2
    return (a2r * a1r - a2i * a1i, a2r * a1i + a2i * a1r,
            a2r * b1r - a2i * b1i + b2r, a2r * b1i + a2i * b1r + b2i)


def _s5_branch(u, s0_re, s0_im, a_re, a_im, log_dt, b_re, b_im, c_re, c_im, d):
    bsz, t = u.shape[:2]
    dt = jnp.exp(log_dt)[:, None]
    mag = jnp.exp(a_re * dt)
    lb_re, lb_im = mag * jnp.cos(a_im * dt), mag * jnp.sin(a_im * dt)
    den = a_re * a_re + a_im * a_im
    co_re = ((lb_re - 1.0) * a_re + lb_im * a_im) / den
    co_im = (lb_im * a_re - (lb_re - 1.0) * a_im) / den
    bb_re = co_re[..., None] * b_re - co_im[..., None] * b_im
    bb_im = co_re[..., None] * b_im + co_im[..., None] * b_re
    ug = u.reshape(bsz, t, SSM_GROUPS, SSM_GROUP)
    bu_re = jnp.einsum('btgp,gnp->btgn', ug, bb_re)
    bu_im = jnp.einsum('btgp,gnp->btgn', ug, bb_im)
    bu_re = bu_re.at[:, 0].add(lb_re * s0_re - lb_im * s0_im)
    bu_im = bu_im.at[:, 0].add(lb_re * s0_im + lb_im * s0_re)
    la_re = jnp.broadcast_to(lb_re, bu_re.shape)
    la_im = jnp.broadcast_to(lb_im, bu_im.shape)
    _, _, s_re, s_im = lax.associative_scan(_cplx_affine, (la_re, la_im, bu_re, bu_im), axis=1)
    y = (jnp.einsum('btgn,gpn->btgp', s_re, c_re) - jnp.einsum('btgn,gpn->btgp', s_im, c_im) + d * ug)
    return y.reshape(bsz, t, SSM_WIDTH), s_re[:, -1], s_im[:, -1]


def _sb_core(q, k, v, qpos, kpos):
    z = jnp.einsum('bqhd,bshd->bhqs', q, k) / math.sqrt(HEAD_DIM)
    mask = kpos[None, :] < qpos[:, None]
    log_1m = jnp.where(mask, jax.nn.log_sigmoid(-z), 0.0)
    between = lax.cumsum(log_1m, axis=3, reverse=True) - log_1m
    w = jnp.where(mask, jnp.exp(jax.nn.log_sigmoid(z) + between), 0.0)
    return jnp.einsum('bhqs,bshd->bqhd', w, v)


def _compress(rows, pe, w):
    bsz, length = rows.shape[:2]
    r = CMP_LEN // CMP_STRIDE
    n_chunks = length // CMP_STRIDE
    nc = n_chunks - r + 1
    chunks = rows[:, :n_chunks * CMP_STRIDE].reshape(bsz, n_chunks, CMP_STRIDE, NSA_KV_HEADS, HEAD_DIM)
    wj = w.reshape(r, CMP_STRIDE, HEAD_DIM, HEAD_DIM)
    pj = pe.reshape(r, CMP_STRIDE, HEAD_DIM)
    out = 0.0
    for j in range(r):
        out = out + jnp.einsum('bnsgd,sde->bnge', chunks[:, j:j + nc], wj[j]) + jnp.einsum('sd,sde->e', pj[j], wj[j])
    cend = jnp.arange(nc) * CMP_STRIDE + CMP_LEN - 1
    return out, cend


def _cmp_to_sel(nc, ns):
    cs = np.arange(nc) * CMP_STRIDE
    ss = np.arange(ns) * SEL_LEN
    ov = np.clip(np.minimum(cs[:, None] + CMP_LEN, ss[None, :] + SEL_LEN) - np.maximum(cs[:, None], ss[None, :]), 0, None)
    return jnp.asarray(ov / CMP_LEN, dtype=F32)


def _pad_blocks(a):
    pad = (-a.shape[1]) % SEL_LEN
    return jnp.pad(a, ((0, 0), (0, pad), (0, 0), (0, 0)))


def _nsa_prepare(k_cmp, v_cmp, k_sel, v_sel, w_cmp, pe_cmp):
    kc, cend = _compress(k_cmp, pe_cmp[0], w_cmp[0])
    vc, _ = _compress(v_cmp, pe_cmp[1], w_cmp[1])
    ks, vs = _pad_blocks(k_sel), _pad_blocks(v_sel)
    cmap = _cmp_to_sel(kc.shape[1], ks.shape[1] // SEL_LEN)
    return kc, vc, cend, cmap, ks, vs


def _nsa_core(qr, qn, g, qpos, kc, vc, cend, cmap, ks, vs, kw, vw, kwpos):
    bsz, nq = qr.shape[:2]
    qr = qr.reshape(bsz, nq, NSA_KV_HEADS, NSA_REP, HEAD_DIM)
    qn = qn.reshape(bsz, nq, NSA_KV_HEADS, NSA_REP, HEAD_DIM)
    mc = cend[None, :] <= qpos[:, None]
    sc = jnp.einsum('bqgrd,bcgd->bgrqc', qn, kc)
    pc = jax.nn.softmax(jnp.where(mc, sc, NEG), axis=-1) * mc
    o_cmp = jnp.einsum('bgrqc,bcgd->bqgrd', pc, vc)
    ns = cmap.shape[1]
    topk = min(SEL_TOPK, ns)
    imp = jnp.einsum('bgrqc,cn->bgqn', pc, cmap)
    blk = jnp.arange(ns)[None, :]
    cur = (qpos // SEL_LEN)[:, None]
    imp = jnp.where(blk == cur, FORCE, jnp.where(blk < cur, imp, -1.0))
    _, idx = lax.top_k(imp, topk)
    ksb = ks.reshape(bsz, ns, SEL_LEN, NSA_KV_HEADS, HEAD_DIM).transpose(0, 3, 1, 2, 4)
    vsb = vs.reshape(bsz, ns, SEL_LEN, NSA_KV_HEADS, HEAD_DIM).transpose(0, 3, 1, 2, 4)
    bi = jnp.arange(bsz)[:, None, None, None]
    gi = jnp.arange(NSA_KV_HEADS)[None, :, None, None]
    kg, vg = ksb[bi, gi, idx], vsb[bi, gi, idx]
    spos = idx[..., None] * SEL_LEN + jnp.arange(SEL_LEN)
    ms = spos <= qpos[None, None, :, None, None]
    ss = jnp.einsum('bqgrd,bgqkld->bgrqkl', qr, kg)
    ss = jnp.where(ms[:, :, None], ss, NEG).reshape(bsz, NSA_KV_HEADS, NSA_REP, nq, topk * SEL_LEN)
    ps = jax.nn.softmax(ss, axis=-1).reshape(bsz, NSA_KV_HEADS, NSA_REP, nq, topk, SEL_LEN)
    o_sel = jnp.einsum('bgrqkl,bgqkld->bqgrd', ps, vg)
    mw = (kwpos[None, :] <= qpos[:, None]) & (kwpos[None, :] > qpos[:, None] - WINDOW) & (kwpos[None, :] >= 0)
    sw = jnp.einsum('bqgrd,bwgd->bgrqw', qr, kw)
    pw = jax.nn.softmax(jnp.where(mw, sw, NEG), axis=-1)
    o_win = jnp.einsum('bgrqw,bwgd->bqgrd', pw, vw)
    g = g.reshape(bsz, nq, NSA_KV_HEADS, NSA_REP, 3)
    o = g[..., 0:1] * o_cmp + g[..., 1:2] * o_sel + g[..., 2:3] * o_win
    return o.reshape(bsz, nq, NSA_WIDTH)


def _rope_tables(pos):
    half = HEAD_DIM // 2
    freqs = ROPE_THETA ** (-jnp.arange(half, dtype=F32) / half)
    ang = pos.astype(F32)[:, None] * freqs
    cos, sin = jnp.cos(ang), jnp.sin(ang)
    cos2 = jnp.concatenate([cos, cos], axis=-1)
    sin2 = jnp.concatenate([-sin, sin], axis=-1)
    return jnp.tile(cos2, (1, LANES // HEAD_DIM)), jnp.tile(sin2, (1, LANES // HEAD_DIM))


def _pack_w_in(w):
    pad = jnp.zeros((D_MODEL, LANES - 3 * NSA_HEADS), w.dtype)
    return jnp.concatenate([w[:, :2304], w[:, 2304:2328], pad, w[:, 2328:]], axis=1).astype(BF16)


def kernel(x_prompt, x_sample, cache_sb, cache_nsa, state_win, state_ssm, page_table, c_prompt, c_sample,
           w_ada, b_ada, norm_mix, norm_ffn, w_in, ssm_a_re, ssm_a_im, ssm_log_dt, ssm_b_re, ssm_b_im,
           ssm_c_re, ssm_c_im, ssm_d, w_glu, b_glu, nsa_w_cmp, nsa_pe_cmp, w_branch, w_out,
           w_grp, b_grp, w_rt, b_rt, w_e_gate, w_e_up, w_e_down, final_norm):
    bp, t = x_prompt.shape[:2]
    bd, s = x_sample.shape[:2]
    np_tok = bp * t
    win_len = state_win.shape[2]
    tm_p = 256
    tiles_pb = t // tm_p
    tm_f = 1024
    tiles_fb = t // tm_f

    xp = x_prompt.reshape(np_tok, D_MODEL)
    xs = x_sample.reshape(bd * s, D_MODEL)
    cos_p, sin_p = _rope_tables(jnp.arange(t))
    cos_s, sin_s = _rope_tables(jnp.full((bd,), PAST_LEN))
    pos_s = PAST_LEN + jnp.arange(s)
    nsa_consts = _nsa_consts(t)
    dec_cmap2, dec_bige = _dec_consts()
    n_pool = cache_sb.shape[1]
    cache_sb2d = _pages_t(cache_sb)
    cache_nsa2d = _pages_t(cache_nsa)
    win_t = state_win.transpose(0, 1, 3, 4, 5, 2).reshape(DEPTH, bd, 2 * NSA_KV_WIDTH, win_len)

    sb_p, sb_s, nsa_p, nsa_s, win_p, win_s, ssm_p, ssm_s = [], [], [], [], [], [], [], []
    for l in range(DEPTH):
        w_in_l = _pack_w_in(w_in[l])
        wb_l = w_branch[l].astype(BF16)
        wo_l = w_out[l].astype(BF16)
        wr_l = jnp.concatenate([w_grp[l], w_rt[l], jnp.zeros((D_MODEL, LANES - 20), F32)], axis=1)
        br_l = jnp.concatenate([b_grp[l], b_rt[l], jnp.zeros((LANES - 20,), F32)])[None, :]
        wg_l, wu_l, wd_l = w_e_gate[l].astype(BF16), w_e_up[l].astype(BF16), w_e_down[l].astype(BF16)
        gn_mix = norm_mix[l][None, :]
        gn_ffn = norm_ffn[l][None, :]
        ssm_l = (ssm_a_re[l], ssm_a_im[l], ssm_log_dt[l], ssm_b_re[l], ssm_b_im[l], ssm_c_re[l], ssm_c_im[l],
                 ssm_d[l])
        wglu_l = w_glu[l].astype(BF16)
        bglu_l = b_glu[l][None, :]

        mp = jnp.split(jax.nn.silu(c_prompt) @ w_ada[l] + b_ada[l], 6, axis=-1)
        mp = [m[:, None, :] for m in mp]
        u, sq, skv, qn, qr, nsa, win, ng, g = _inproj(
            xp, mp[0], mp[1], gn_mix, cos_p, sin_p, w_in_l, tm_p,
            lambda i: i // tiles_pb, lambda i: i % tiles_pb)
        y_a, sf = _s5_prompt(u.reshape(bp, t, SSM_WIDTH), _s5_mats(*ssm_l[:7]), ssm_d[l])
        y_b = _sb_prompt(sq, skv, bp, t)
        wbig, pe8 = _cmp_weights(nsa_w_cmp[l], nsa_pe_cmp[l])
        y_c = _nsa_prompt(qn, qr, nsa, win, ng, wbig, pe8, nsa_consts, bp, t)
        xp = _merge(xp, mp[2], y_a.reshape(np_tok, SSM_WIDTH), y_b, y_c, g, wglu_l, bglu_l, wb_l, wo_l,
                    tm_p, lambda i: i // tiles_pb)
        xp = _ffn(xp, mp[3], mp[4], mp[5], gn_ffn, wr_l, br_l, wg_l, wu_l, wd_l, tm_f, lambda i: i // tiles_fb)
        sb_p.append(skv.reshape(bp, t, 2, SB_HEADS, HEAD_DIM))
        nsa_p.append(nsa.reshape(bp, t, 4, NSA_KV_HEADS, HEAD_DIM))
        win_p.append(win.reshape(bp, t, 2, NSA_KV_HEADS, HEAD_DIM)[:, t - min(WINDOW, t):])
        ssm_p.append(sf)

        ms = jnp.split(jax.nn.silu(c_sample) @ w_ada[l] + b_ada[l], 6, axis=-1)
        ms = [m[None, :, :] for m in ms]
        u, sq, skv, qn, qr, nsa, win, ng, g = _inproj(
            xs, ms[0], ms[1], gn_mix, cos_s, sin_s, w_in_l, bd, lambda i: 0, lambda i: 0)
        y_a, sr, si = _s5_branch(u.reshape(bd, s, SSM_WIDTH), state_ssm[l][:, 0], state_ssm[l][:, 1], *ssm_l)
        new_sb = skv.reshape(bd, s, 2, SB_HEADS, HEAD_DIM)
        rows_l = page_table + l * n_pool
        y_b = _dec_sb(rows_l, sq, cache_sb2d)
        new_nsa = nsa.reshape(bd, s, 4, NSA_KV_HEADS, HEAD_DIM)
        ocmp, imp = _dec_cmp(rows_l, qn, cache_nsa2d, wbig, pe8, dec_cmap2)
        unsel = _dec_topk(imp.reshape(bd * NSA_HEADS, LANES)).reshape(bd, NSA_HEADS, LANES)
        new_win = win.reshape(bd, s, 2, NSA_KV_HEADS, HEAD_DIM)
        winc = jnp.concatenate([state_win[l], new_win], axis=1)
        y_c = _dec_sel(rows_l, qr, unsel, ocmp, ng, nsa, win_t[l], win, cache_nsa2d, dec_bige)
        xs = _merge(xs, ms[2], y_a.reshape(bd * s, SSM_WIDTH), y_b, y_c, g, wglu_l, bglu_l, wb_l, wo_l,
                    bd, lambda i: 0)
        xs = _ffn(xs, ms[3], ms[4], ms[5], gn_ffn, wr_l, br_l, wg_l, wu_l, wd_l, bd, lambda i: 0)
        sb_s.append(new_sb)
        nsa_s.append(new_nsa)
        win_s.append(winc[:, s:])
        ssm_s.append(jnp.stack([sr, si], axis=1))

    fn = final_norm[None, :]
    y_prompt = _final_norm(xp, fn, tm_f).reshape(bp, t, D_MODEL)
    y_sample = _final_norm(xs, fn, bd).reshape(bd, s, D_MODEL)
    return (y_prompt, y_sample,
            jnp.stack(sb_p), jnp.stack(sb_s),
            jnp.stack(nsa_p), jnp.stack(nsa_s),
            jnp.stack(win_p), jnp.stack(win_s),
            jnp.stack(ssm_p), jnp.stack(ssm_s))
```

```python
import functools
import math

import jax
import jax.numpy as jnp
import numpy as np
from jax import lax
from jax.experimental import pallas as pl
from jax.experimental.pallas import tpu as pltpu

F32 = jnp.float32
BF16 = jnp.bfloat16

D_MODEL = 1024
DEPTH = 4
PAST_LEN = 2048
HEAD_DIM = 64
SSM_WIDTH = 256
SSM_GROUP = 16
SSM_GROUPS = 16
SSM_STATE = 64
SB_HEADS = 4
SB_WIDTH = 256
NSA_HEADS = 8
NSA_KV_HEADS = 2
NSA_REP = 4
NSA_WIDTH = 512
NSA_KV_WIDTH = 128
CMP_LEN = 32
CMP_STRIDE = 16
SEL_LEN = 64
SEL_TOPK = 16
WINDOW = 512
N_BRANCH = 3
Q_BLOCK = 128
ROPE_THETA = 10000.0
MOE_GROUPS = 4
EXPERTS_PER_GROUP = 4
N_EXPERTS = 16
EXPERT_FF = 256
EPS = 1e-6
NEG = -1e30
FORCE = 1e9

LANES = 128
VMEM_LIMIT = 56 * 1024 * 1024

_C_U = 0
_C_SQ = 256
_C_SKV = 512
_C_NQ = 1024
_C_NKV = 1536
_C_NG = 2304
_C_MG = 2432
_C_END = 5504


def _cparams(sem):
    return pltpu.CompilerParams(dimension_semantics=sem, vmem_limit_bytes=VMEM_LIMIT)


def _rope_slab(v, cos, sin_signed):
    lane = lax.broadcasted_iota(jnp.int32, v.shape, 1)
    first = (lane % HEAD_DIM) < (HEAD_DIM // 2)
    swapped = jnp.where(first, pltpu.roll(v, LANES - HEAD_DIM // 2, 1), pltpu.roll(v, HEAD_DIM // 2, 1))
    return v * cos + swapped * sin_signed


def _inproj_kernel(x_ref, shift_ref, scale_ref, gn_ref, cos_ref, sin_ref, w_ref,
                   u_ref, sq_ref, skv_ref, qn_ref, qr_ref, nsa_ref, win_ref, ng_ref, g_ref):
    x = x_ref[...]
    ms = jnp.mean(x * x, axis=-1, keepdims=True)
    h = x * lax.rsqrt(ms + EPS) * gn_ref[...]
    h = h * (1.0 + scale_ref[0]) + shift_ref[0]
    hb = h.astype(BF16)

    def mm(lo, hi):
        return jnp.dot(hb, w_ref[:, lo:hi], preferred_element_type=F32)

    cos = cos_ref[...]
    sin = sin_ref[...]
    u_ref[...] = mm(_C_U, _C_SQ)
    sq_ref[...] = mm(_C_SQ, _C_SKV)
    skv_ref[...] = mm(_C_SKV, _C_NQ)
    q = mm(_C_NQ, _C_NKV) * (1.0 / math.sqrt(HEAD_DIM))
    qn_ref[...] = q
    for s in range(NSA_WIDTH // LANES):
        qr_ref[:, s * LANES:(s + 1) * LANES] = _rope_slab(q[:, s * LANES:(s + 1) * LANES], cos, sin)
    kv = mm(_C_NKV, _C_NG)
    nsa_ref[:, 0:256] = kv[:, 0:256]
    nsa_ref[:, 256:384] = _rope_slab(kv[:, 256:384], cos, sin)
    nsa_ref[:, 384:512] = kv[:, 384:512]
    win_ref[:, 0:128] = _rope_slab(kv[:, 512:640], cos, sin)
    win_ref[:, 128:256] = kv[:, 640:768]
    ng_ref[...] = jax.nn.sigmoid(mm(_C_NG, _C_MG))
    g_ref[...] = jax.nn.sigmoid(mm(_C_MG, _C_END))


def _inproj(x, shift, scale, gn, cos, sin, w, tm, mod_map, pos_map):
    n = x.shape[0]
    mrows = shift.shape[1]
    row = lambda width: pl.BlockSpec((tm, width), lambda i: (i, 0))
    widths = (256, 256, 512, 512, 512, 512, 256, 128, 3072)
    return pl.pallas_call(
        _inproj_kernel,
        grid=(n // tm,),
        in_specs=[
            row(D_MODEL),
            pl.BlockSpec((1, mrows, D_MODEL), lambda i: (mod_map(i), 0, 0)),
            pl.BlockSpec((1, mrows, D_MODEL), lambda i: (mod_map(i), 0, 0)),
            pl.BlockSpec((1, D_MODEL), lambda i: (0, 0)),
            pl.BlockSpec((tm, LANES), lambda i: (pos_map(i), 0)),
            pl.BlockSpec((tm, LANES), lambda i: (pos_map(i), 0)),
            pl.BlockSpec((D_MODEL, _C_END), lambda i: (0, 0)),
        ],
        out_specs=[row(wd) for wd in widths],
        out_shape=[jax.ShapeDtypeStruct((n, wd), F32) for wd in widths],
        compiler_params=_cparams(("parallel",)),
        name="inproj",
    )(x, shift, scale, gn, cos, sin, w)


def _merge_kernel(x_ref, gate_ref, ya_ref, yb_ref, yc_ref, g_ref, wglu_ref, bglu_ref, wb_ref, wo_ref, o_ref):
    def mm(a, w):
        return jnp.dot(a.astype(BF16), w, preferred_element_type=F32)

    ya = jax.nn.gelu(ya_ref[...])
    ya = ya * jax.nn.sigmoid(mm(ya, wglu_ref[...]) + bglu_ref[...])
    br_a = mm(ya, wb_ref[0:SSM_WIDTH, :])
    br_b = mm(yb_ref[...], wb_ref[SSM_WIDTH:SSM_WIDTH + SB_WIDTH, :])
    br_c = mm(yc_ref[...], wb_ref[SSM_WIDTH + SB_WIDTH:, :])
    merged = (g_ref[:, 0:D_MODEL] * br_a + g_ref[:, D_MODEL:2 * D_MODEL] * br_b
              + g_ref[:, 2 * D_MODEL:] * br_c)
    out = mm(merged, wo_ref[...])
    o_ref[...] = x_ref[...] + gate_ref[0] * out


def _merge(x, gate, ya, yb, yc, g, wglu, bglu, wb, wo, tm, mod_map):
    n = x.shape[0]
    mrows = gate.shape[1]
    row = lambda width: pl.BlockSpec((tm, width), lambda i: (i, 0))
    return pl.pallas_call(
        _merge_kernel,
        grid=(n // tm,),
        in_specs=[
            row(D_MODEL),
            pl.BlockSpec((1, mrows, D_MODEL), lambda i: (mod_map(i), 0, 0)),
            row(SSM_WIDTH), row(SB_WIDTH), row(NSA_WIDTH), row(N_BRANCH * D_MODEL),
            pl.BlockSpec((SSM_WIDTH, SSM_WIDTH), lambda i: (0, 0)),
            pl.BlockSpec((1, SSM_WIDTH), lambda i: (0, 0)),
            pl.BlockSpec((D_MODEL, D_MODEL), lambda i: (0, 0)),
            pl.BlockSpec((D_MODEL, D_MODEL), lambda i: (0, 0)),
        ],
        out_specs=row(D_MODEL),
        out_shape=jax.ShapeDtypeStruct((n, D_MODEL), F32),
        compiler_params=_cparams(("parallel",)),
        name="merge",
    )(x, gate, ya, yb, yc, g, wglu, bglu, wb, wo)


def _route(logits):
    lane = lax.broadcasted_iota(jnp.int32, logits.shape, 1)
    big = jnp.int32(1 << 20)
    is_g = lane < MOE_GROUPS
    gl = jnp.where(is_g, logits, NEG)
    gmax = jnp.max(gl, axis=-1, keepdims=True)
    p_sel = 1.0 / jnp.sum(jnp.where(is_g, jnp.exp(gl - gmax), 0.0), axis=-1, keepdims=True)
    g_sel = jnp.min(jnp.where(is_g & (gl == gmax), lane, big), axis=-1, keepdims=True)
    e_idx = lane - MOE_GROUPS
    in_grp = (e_idx >= g_sel * EXPERTS_PER_GROUP) & (e_idx < (g_sel + 1) * EXPERTS_PER_GROUP)
    el = jnp.where(in_grp, logits, NEG)
    m1 = jnp.max(el, axis=-1, keepdims=True)
    i1 = jnp.min(jnp.where(in_grp & (el == m1), lane, big), axis=-1, keepdims=True)
    el2 = jnp.where(lane == i1, NEG, el)
    m2 = jnp.max(el2, axis=-1, keepdims=True)
    i2 = jnp.min(jnp.where(in_grp & (lane != i1) & (el2 == m2), lane, big), axis=-1, keepdims=True)
    e2 = jnp.exp(m2 - m1)
    w1 = p_sel / (1.0 + e2)
    w2 = p_sel * e2 / (1.0 + e2)
    return jnp.where(lane == i1, w1, jnp.where(lane == i2, w2, 0.0))


def _ffn_kernel(x_ref, shift_ref, scale_ref, gate_ref, gn_ref, wr_ref, br_ref, wg_ref, wu_ref, wd_ref,
                o_ref, h_scr, comb_scr, acc_scr):
    e = pl.program_id(1)

    @pl.when(e == 0)
    def _():
        x = x_ref[...]
        ms = jnp.mean(x * x, axis=-1, keepdims=True)
        h = x * lax.rsqrt(ms + EPS) * gn_ref[...]
        h = h * (1.0 + scale_ref[0]) + shift_ref[0]
        logits = jnp.dot(h, wr_ref[...], preferred_element_type=F32,
                         precision=lax.Precision.HIGHEST) + br_ref[...]
        comb_scr[...] = _route(logits)
        h_scr[...] = h.astype(BF16)
        acc_scr[...] = jnp.zeros_like(acc_scr)

    hb = h_scr[...]
    a = jnp.dot(hb, wg_ref[0], preferred_element_type=F32)
    b = jnp.dot(hb, wu_ref[0], preferred_element_type=F32)
    comb = comb_scr[...]
    lane = lax.broadcasted_iota(jnp.int32, comb.shape, 1)
    c = jnp.sum(jnp.where(lane == e + MOE_GROUPS, comb, 0.0), axis=-1, keepdims=True)
    hid = (a * jax.nn.sigmoid(a)) * b * c
    acc_scr[...] += jnp.dot(hid.astype(BF16), wd_ref[0], preferred_element_type=F32)

    @pl.when(e == N_EXPERTS - 1)
    def _():
        o_ref[...] = x_ref[...] + gate_ref[0] * acc_scr[...]


def _ffn(x, shift, scale, gate, gn, wr, br, wg, wu, wd, tm, mod_map):
    n = x.shape[0]
    mrows = shift.shape[1]
    mod = pl.BlockSpec((1, mrows, D_MODEL), lambda i, e: (mod_map(i), 0, 0))
    return pl.pallas_call(
        _ffn_kernel,
        grid=(n // tm, N_EXPERTS),
        in_specs=[
            pl.BlockSpec((tm, D_MODEL), lambda i, e: (i, 0)),
            mod, mod, mod,
            pl.BlockSpec((1, D_MODEL), lambda i, e: (0, 0)),
            pl.BlockSpec((D_MODEL, LANES), lambda i, e: (0, 0)),
            pl.BlockSpec((1, LANES), lambda i, e: (0, 0)),
            pl.BlockSpec((1, D_MODEL, EXPERT_FF), lambda i, e: (e, 0, 0)),
            pl.BlockSpec((1, D_MODEL, EXPERT_FF), lambda i, e: (e, 0, 0)),
            pl.BlockSpec((1, EXPERT_FF, D_MODEL), lambda i, e: (e, 0, 0)),
        ],
        out_specs=pl.BlockSpec((tm, D_MODEL), lambda i, e: (i, 0)),
        out_shape=jax.ShapeDtypeStruct((n, D_MODEL), F32),
        scratch_shapes=[pltpu.VMEM((tm, D_MODEL), BF16), pltpu.VMEM((tm, LANES), F32),
                        pltpu.VMEM((tm, D_MODEL), F32)],
        compiler_params=_cparams(("parallel", "arbitrary")),
        name="ffn_moe",
    )(x, shift, scale, gate, gn, wr, br, wg, wu, wd)


def _final_norm_kernel(x_ref, g_ref, o_ref):
    x = x_ref[...]
    o_ref[...] = x * lax.rsqrt(jnp.mean(x * x, axis=-1, keepdims=True) + EPS) * g_ref[...]


def _final_norm(x, g, tm):
    n = x.shape[0]
    return pl.pallas_call(
        _final_norm_kernel,
        grid=(n // tm,),
        in_specs=[pl.BlockSpec((tm, D_MODEL), lambda i: (i, 0)), pl.BlockSpec((1, D_MODEL), lambda i: (0, 0))],
        out_specs=pl.BlockSpec((tm, D_MODEL), lambda i: (i, 0)),
        out_shape=jax.ShapeDtypeStruct((n, D_MODEL), F32),
        compiler_params=_cparams(("parallel",)),
        name="final_norm",
    )(x, g)


S5_CHUNK = 64
S5_LW = S5_CHUNK * SSM_GROUP


def _s5_mats(a_re, a_im, log_dt, b_re, b_im, c_re, c_im):
    L = S5_CHUNK
    hp = lax.Precision.HIGHEST
    dt = jnp.exp(log_dt)[:, None]
    k = jnp.arange(L + 1, dtype=F32)[:, None, None]
    mag = jnp.exp(a_re * dt * k)
    ang = a_im * dt * k
    lk_re, lk_im = mag * jnp.cos(ang), mag * jnp.sin(ang)
    lb_re, lb_im = lk_re[1], lk_im[1]
    den = a_re * a_re + a_im * a_im
    co_re = ((lb_re - 1.0) * a_re + lb_im * a_im) / den
    co_im = (lb_im * a_re - (lb_re - 1.0) * a_im) / den
    bb_re = co_re[..., None] * b_re - co_im[..., None] * b_im
    bb_im = co_re[..., None] * b_im + co_im[..., None] * b_re
    cl_re = c_re[None] * lk_re[:, :, None, :] - c_im[None] * lk_im[:, :, None, :]
    cl_im = c_re[None] * lk_im[:, :, None, :] + c_im[None] * lk_re[:, :, None, :]
    kk = (jnp.einsum('kgpn,gnq->kgpq', cl_re[:L], bb_re, precision=hp)
          - jnp.einsum('kgpn,gnq->kgpq', cl_im[:L], bb_im, precision=hp))
    kq = kk.transpose(1, 3, 0, 2).reshape(SSM_GROUPS, SSM_GROUP, S5_LW).astype(BF16)
    kp = jnp.concatenate([jnp.zeros_like(kq), kq], axis=-1)
    tmat = jnp.stack([kp[:, :, S5_LW - SSM_GROUP * i:2 * S5_LW - SSM_GROUP * i] for i in range(L)], axis=1)
    tmat = tmat.reshape(SSM_GROUPS, S5_LW, S5_LW)
    rev = L - 1 - jnp.arange(L)
    be_re = lk_re[rev][:, :, :, None] * bb_re[None] - lk_im[rev][:, :, :, None] * bb_im[None]
    be_im = lk_re[rev][:, :, :, None] * bb_im[None] + lk_im[rev][:, :, :, None] * bb_re[None]
    bmat = jnp.concatenate([be_re.transpose(1, 0, 3, 2).reshape(SSM_GROUPS, S5_LW, SSM_STATE),
                            be_im.transpose(1, 0, 3, 2).reshape(SSM_GROUPS, S5_LW, SSM_STATE)], axis=-1)
    cm_re = cl_re[1:].transpose(1, 3, 0, 2).reshape(SSM_GROUPS, SSM_STATE, S5_LW)
    cm_im = cl_im[1:].transpose(1, 3, 0, 2).reshape(SSM_GROUPS, SSM_STATE, S5_LW)
    cmat = jnp.concatenate([cm_re, -cm_im], axis=1)
    lbl = jnp.stack([lk_re[L], lk_im[L]], axis=1)
    return tmat.astype(BF16), bmat.astype(BF16), cmat.astype(BF16), lbl


def _s5_kernel(x_ref, t_ref, b_ref, c_ref, lbl_ref, d_ref, y_ref, sf_ref, er_scr, ei_scr, pr_scr, pi_scr):
    nb, _, nch, _ = x_ref.shape
    a_re = lbl_ref[0, 0:1, :]
    a_im = lbl_ref[0, 1:2, :]
    for b in range(nb):
        x = x_ref[b, 0]
        xb = x.astype(BF16)
        e = jnp.dot(xb, b_ref[0], preferred_element_type=F32)
        er_scr[...] = e[:, :SSM_STATE]
        ei_scr[...] = e[:, SSM_STATE:]

        def step(c, carry):
            s_re, s_im = carry
            pr_scr[pl.ds(c, 1), :] = s_re
            pi_scr[pl.ds(c, 1), :] = s_im
            n_re = a_re * s_re - a_im * s_im + er_scr[pl.ds(c, 1), :]
            n_im = a_re * s_im + a_im * s_re + ei_scr[pl.ds(c, 1), :]
            return n_re, n_im

        zero = jnp.zeros((1, SSM_STATE), F32)
        s_re, s_im = lax.fori_loop(0, nch, step, (zero, zero))
        sprev = jnp.concatenate([pr_scr[...], pi_scr[...]], axis=1).astype(BF16)
        y = (jnp.dot(xb, t_ref[0], preferred_element_type=F32)
             + jnp.dot(sprev, c_ref[0], preferred_element_type=F32) + d_ref[0] * x)
        y_ref[b, 0] = y
        sf_ref[b, 0] = jnp.concatenate([s_re, s_im], axis=1)


def _s5_prompt(u, mats, d):
    tmat, bmat, cmat, lbl = mats
    bsz, t = u.shape[:2]
    nch = t // S5_CHUNK
    x = u.reshape(bsz, nch, S5_CHUNK, SSM_GROUPS, SSM_GROUP).transpose(0, 3, 1, 2, 4).reshape(
        bsz, SSM_GROUPS, nch, S5_LW)
    dt = jnp.tile(d, (1, S5_CHUNK))[:, None, :]
    y, sf = pl.pallas_call(
        _s5_kernel,
        grid=(SSM_GROUPS,),
        in_specs=[
            pl.BlockSpec((bsz, 1, nch, S5_LW), lambda g: (0, g, 0, 0)),
            pl.BlockSpec((1, S5_LW, S5_LW), lambda g: (g, 0, 0)),
            pl.BlockSpec((1, S5_LW, 2 * SSM_STATE), lambda g: (g, 0, 0)),
            pl.BlockSpec((1, 2 * SSM_STATE, S5_LW), lambda g: (g, 0, 0)),
            pl.BlockSpec((1, 2, SSM_STATE), lambda g: (g, 0, 0)),
            pl.BlockSpec((1, 1, S5_LW), lambda g: (g, 0, 0)),
        ],
        out_specs=[pl.BlockSpec((bsz, 1, nch, S5_LW), lambda g: (0, g, 0, 0)),
                   pl.BlockSpec((bsz, 1, 1, 2 * SSM_STATE), lambda g: (0, g, 0, 0))],
        out_shape=[jax.ShapeDtypeStruct((bsz, SSM_GROUPS, nch, S5_LW), F32),
                   jax.ShapeDtypeStruct((bsz, SSM_GROUPS, 1, 2 * SSM_STATE), F32)],
        scratch_shapes=[pltpu.VMEM((nch, SSM_STATE), F32)] * 4,
        compiler_params=_cparams(("arbitrary",)),
        name="s5_prompt",
    )(x, tmat, bmat, cmat, lbl, dt)
    y = y.reshape(bsz, SSM_GROUPS, nch, S5_CHUNK, SSM_GROUP).transpose(0, 2, 3, 1, 4).reshape(bsz, t, SSM_WIDTH)
    sf = sf.reshape(bsz, SSM_GROUPS, 2, SSM_STATE).transpose(0, 2, 1, 3)
    return y, sf


SB_TK = 128
SB_TQ = 256
SB_PER = SB_TQ // SB_TK


def _sb_tri():
    j = np.arange(SB_TK)[:, None]
    s = np.arange(SB_TK)[None, :]
    return jnp.asarray(-(s > j).astype(np.float32), dtype=BF16)


def _sb_kernel(qt_ref, k_ref, vt_ref, tri_ref, o_ref, acc_scr, c_scr):
    qi = pl.program_id(1)
    tri = tri_ref[...]
    kidx = lax.broadcasted_iota(jnp.int32, (SB_TK, SB_TQ), 0)
    qidx = lax.broadcasted_iota(jnp.int32, (SB_TK, SB_TQ), 1)

    def softplus_tile(h, j, mask):
        kt = k_ref[0, h, pl.ds(pl.multiple_of(j * SB_TK, SB_TK), SB_TK), :]
        z = jnp.dot(kt, qt_ref[0, h], preferred_element_type=F32)
        sp = jnp.maximum(z, 0.0) + jnp.log(1.0 + jnp.exp(-jnp.abs(z)))
        if mask is not None:
            sp = jnp.where(mask, sp, 0.0)
        return z, sp

    def weights(z, sp, c, mask):
        r = jnp.dot(tri, sp.astype(BF16), preferred_element_type=F32)
        w = jnp.exp(z - sp + r - c)
        if mask is not None:
            w = jnp.where(mask, w, 0.0)
        return w.astype(BF16)

    def two_tiles(j_hi, masks):
        heads = range(SB_HEADS)
        zs1 = [softplus_tile(h, j_hi, masks[0]) for h in heads]
        zs0 = [softplus_tile(h, j_hi - 1, masks[1]) for h in heads]
        cs = [c_scr[h] for h in heads]
        t1 = [jnp.sum(zs1[h][1], axis=0, keepdims=True) for h in heads]
        w1 = [weights(zs1[h][0], zs1[h][1], cs[h], masks[0]) for h in heads]
        w0 = [weights(zs0[h][0], zs0[h][1], cs[h] + t1[h], masks[1]) for h in heads]
        for h in heads:
            acc_scr[h] += (jnp.dot(vt_ref[0, h, j_hi], w1[h], preferred_element_type=F32)
                           + jnp.dot(vt_ref[0, h, j_hi - 1], w0[h], preferred_element_type=F32))
            c_scr[h] = cs[h] + t1[h] + jnp.sum(zs0[h][1], axis=0, keepdims=True)

    acc_scr[...] = jnp.zeros_like(acc_scr)
    c_scr[...] = jnp.zeros_like(c_scr)
    two_tiles(SB_PER * qi + 1, (kidx + SB_TK < qidx, kidx < qidx))

    def body(p, carry):
        two_tiles(SB_PER * (qi - 1 - p) + 1, (None, None))
        return carry

    lax.fori_loop(0, qi, body, 0)
    o_ref[0] = acc_scr[...]


def _sb_prompt(sq, skv, bsz, t):
    assert SB_PER == 2
    nt = t // SB_TK
    nq = t // SB_TQ
    qt = (sq * (1.0 / math.sqrt(HEAD_DIM))).reshape(bsz, t, SB_HEADS, HEAD_DIM).transpose(0, 2, 3, 1).astype(BF16)
    k = skv[:, :SB_WIDTH].reshape(bsz, t, SB_HEADS, HEAD_DIM).transpose(0, 2, 1, 3).astype(BF16)
    vt = skv[:, SB_WIDTH:].reshape(bsz, nt, SB_TK, SB_HEADS, HEAD_DIM).transpose(0, 3, 1, 4, 2).astype(BF16)
    yt = pl.pallas_call(
        _sb_kernel,
        grid=(bsz, nq),
        in_specs=[
            pl.BlockSpec((1, SB_HEADS, HEAD_DIM, SB_TQ), lambda b, i: (b, 0, 0, i)),
            pl.BlockSpec((1, SB_HEADS, t, HEAD_DIM), lambda b, i: (b, 0, 0, 0)),
            pl.BlockSpec((1, SB_HEADS, nt, HEAD_DIM, SB_TK), lambda b, i: (b, 0, 0, 0, 0)),
            pl.BlockSpec((SB_TK, SB_TK), lambda b, i: (0, 0)),
        ],
        out_specs=pl.BlockSpec((1, SB_HEADS, HEAD_DIM, SB_TQ), lambda b, i: (b, 0, 0, i)),
        out_shape=jax.ShapeDtypeStruct((bsz, SB_HEADS, HEAD_DIM, t), F32),
        scratch_shapes=[pltpu.VMEM((SB_HEADS, HEAD_DIM, SB_TQ), F32), pltpu.VMEM((SB_HEADS, 1, SB_TQ), F32)],
        compiler_params=_cparams(("parallel", "arbitrary")),
        name="sb_prompt",
    )(qt, k, vt, _sb_tri())
    return yt.transpose(0, 3, 1, 2).reshape(bsz * t, SB_WIDTH)


NSA_TQ = 128
NSA_TQC = 512
NSA_KC = 1024
MASK_BIG = 2.0 ** 100
CMP_CHUNK_W = CMP_STRIDE * 2 * NSA_KV_WIDTH


def _cmp_weights(w_cmp, pe_cmp):
    r = CMP_LEN // CMP_STRIDE
    wj = w_cmp.reshape(2, r, CMP_STRIDE, HEAD_DIM, HEAD_DIM)
    eye = jnp.eye(2 * NSA_KV_HEADS, dtype=F32).reshape(2, NSA_KV_HEADS, 2, NSA_KV_HEADS)
    wb = jnp.einsum('kjsde,kgmh->jskgdmhe', wj, eye)
    wbig = wb.reshape(r, CMP_CHUNK_W, 2 * NSA_KV_WIDTH).transpose(1, 0, 2).reshape(CMP_CHUNK_W, r * 2 * NSA_KV_WIDTH)
    pj = pe_cmp.reshape(2, r, CMP_STRIDE, HEAD_DIM)
    pe = jnp.broadcast_to(pj.transpose(1, 2, 0, 3)[:, :, :, None, :],
                          (r, CMP_STRIDE, 2, NSA_KV_HEADS, HEAD_DIM)).reshape(r, CMP_CHUNK_W)
    pe8 = jnp.concatenate([pe, jnp.zeros((8 - r, CMP_CHUNK_W), F32)], axis=0)
    return wbig.astype(BF16), pe8.astype(BF16)


def _compress_kernel(x_ref, w_ref, pe_ref, o_ref):
    w = w_ref[...]
    width = 2 * NSA_KV_WIDTH
    a = jnp.dot(x_ref[0].astype(BF16), w, preferred_element_type=F32)
    pb = jnp.dot(pe_ref[...], w, preferred_element_type=F32)
    bias = pb[0:1, :width] + pb[1:2, width:]
    nch = a.shape[0]
    a1 = pltpu.roll(a[:, width:], nch - 1, 0)
    o_ref[0] = a[:, :width] + a1 + bias


def _nsa_compress(xc, wbig, pe8):
    bsz, nch, _ = xc.shape
    return pl.pallas_call(
        _compress_kernel,
        grid=(bsz,),
        in_specs=[pl.BlockSpec((1, nch, CMP_CHUNK_W), lambda b: (b, 0, 0)),
                  pl.BlockSpec(wbig.shape, lambda b: (0, 0)),
                  pl.BlockSpec(pe8.shape, lambda b: (0, 0))],
        out_specs=pl.BlockSpec((1, nch, 2 * NSA_KV_WIDTH), lambda b: (b, 0, 0)),
        out_shape=jax.ShapeDtypeStruct((bsz, nch, 2 * NSA_KV_WIDTH), F32),
        compiler_params=_cparams(("parallel",)),
        name="nsa_compress",
    )(xc, wbig, pe8)


def _split_hi_lo(x):
    hi = x.astype(BF16)
    lo = (x - hi.astype(F32)).astype(BF16)
    return jnp.concatenate([hi, lo], axis=1)


def _topk_unselected(imp, cur, axis=1):
    blk = lax.broadcasted_iota(jnp.int32, imp.shape, axis)
    imp = jnp.where(blk == cur, FORCE, jnp.where(blk < cur, imp, -1.0))
    unsel = jnp.ones(imp.shape, F32)
    for _ in range(SEL_TOPK):
        m = jnp.max(imp, axis=axis, keepdims=True)
        idx = jnp.min(jnp.where(imp == m, blk, LANES), axis=axis, keepdims=True)
        hit = blk == idx
        unsel = jnp.where(hit, 0.0, unsel)
        imp = jnp.where(hit, -3.0e38, imp)
    return jnp.where(blk <= cur, unsel, 1.0)


def _nsa_cmp_kernel(qt_ref, kc_ref, vct_ref, cmap_ref, ocmp_ref, unsel_ref):
    tq = NSA_TQC
    ncp = kc_ref.shape[2]
    q0 = pl.program_id(2) * tq
    qpos = q0 + lax.broadcasted_iota(jnp.int32, (1, tq), 1)
    cidx = lax.broadcasted_iota(jnp.int32, (ncp, tq), 0)
    mc = (cidx * CMP_STRIDE + (CMP_LEN - 1) <= qpos) & (cidx < ncp - 1)
    kc = kc_ref[0, 0]
    vct = vct_ref[0, 0]
    psum = jnp.zeros((ncp, tq), F32)
    outs = []
    for r in range(NSA_REP):
        s = jnp.where(mc, jnp.dot(kc, qt_ref[0, 0, 0, :, r * tq:(r + 1) * tq], preferred_element_type=F32), NEG)
        e = jnp.exp(s - jnp.max(s, axis=0, keepdims=True))
        p = jnp.where(mc, e / jnp.sum(e, axis=0, keepdims=True), 0.0)
        outs.append(jnp.dot(vct, p.astype(BF16), preferred_element_type=F32))
        psum = psum + p
    for i in range(tq // NSA_TQ):
        ocmp_ref[0, 0, i] = jnp.concatenate([o[:, i * NSA_TQ:(i + 1) * NSA_TQ] for o in outs], axis=1)
    ph = psum.astype(BF16)
    pl_ = (psum - ph.astype(F32)).astype(BF16)
    imp = jnp.dot(cmap_ref[...], jnp.concatenate([ph, pl_], axis=0), preferred_element_type=F32)
    unsel = _topk_unselected(imp, qpos // SEL_LEN, axis=0)
    for i in range(tq // NSA_TQ):
        unsel_ref[0, i] = unsel[:, i * NSA_TQ:(i + 1) * NSA_TQ]


def _nsa_cmp(qnt, kc, vct, cmap2t, bsz, t):
    nqt = t // NSA_TQC
    per = NSA_TQC // NSA_TQ
    ncp = kc.shape[2]
    return pl.pallas_call(
        _nsa_cmp_kernel,
        grid=(bsz, NSA_KV_HEADS, nqt),
        in_specs=[
            pl.BlockSpec((1, 1, 1, HEAD_DIM, NSA_REP * NSA_TQC), lambda b, g, i: (b, g, i, 0, 0)),
            pl.BlockSpec((1, 1, ncp, HEAD_DIM), lambda b, g, i: (b, g, 0, 0)),
            pl.BlockSpec((1, 1, HEAD_DIM, ncp), lambda b, g, i: (b, g, 0, 0)),
            pl.BlockSpec(cmap2t.shape, lambda b, g, i: (0, 0)),
        ],
        out_specs=[pl.BlockSpec((1, 1, per, HEAD_DIM, NSA_COLS), lambda b, g, i: (b, g, i, 0, 0)),
                   pl.BlockSpec((1, per, LANES, NSA_TQ), lambda b, g, i: (g, b * nqt + i, 0, 0))],
        out_shape=[jax.ShapeDtypeStruct((bsz, NSA_KV_HEADS, t // NSA_TQ, HEAD_DIM, NSA_COLS), F32),
                   jax.ShapeDtypeStruct((NSA_KV_HEADS, bsz * t // NSA_TQ, LANES, NSA_TQ), F32)],
        compiler_params=_cparams(("parallel", "parallel", "arbitrary")),
        name="nsa_cmp_select",
    )(qnt, kc, vct, cmap2t)


NSA_COLS = NSA_REP * NSA_TQ
NSA_KAUG = HEAD_DIM + LANES


def _nsa_selt_kernel(qt_ref, unsel_ref, ocmp_ref, gate_ref, ksa_ref, vst_ref, vsd_ref, kw_ref, vwt_ref, gexp_ref,
                     y_ref, acc_scr):
    tq = NSA_TQ
    qi = pl.program_id(2)
    qt = qt_ref[0, 0, 0]
    unsel = unsel_ref[0, 0]
    krow = lax.broadcasted_iota(jnp.int32, (tq, tq), 0)
    qcol = lax.broadcasted_iota(jnp.int32, (tq, tq), 1)

    def cols4(x):
        return jnp.concatenate([x] * NSA_REP, axis=1)

    u0 = unsel_ref[0, 0, pl.ds(2 * qi, 1), :]
    u1 = unsel_ref[0, 0, pl.ds(2 * qi + 1, 1), :]
    ud = jnp.where(krow < SEL_LEN, u0, u1)
    bias_d = jnp.where((ud < 0.5) & (krow <= qcol), 0.0, -MASK_BIG)
    kd = ksa_ref[0, 0, pl.ds(pl.multiple_of(qi * tq, tq), tq), 0:HEAD_DIM]
    s = jnp.dot(kd, qt, preferred_element_type=F32) + cols4(bias_d)
    m0 = jnp.max(s, axis=0, keepdims=True)
    p = jnp.exp(s - m0)
    l0 = jnp.sum(p, axis=0, keepdims=True)
    acc_scr[...] = jnp.dot(vsd_ref[0, 0, qi], p.astype(BF16), preferred_element_type=F32)

    unsel_past = jnp.where(krow >= 2 * qi, 1.0, unsel).astype(BF16)
    rhs = jnp.concatenate([qt, cols4(unsel_past)], axis=0)

    def chunk(c, carry):
        m_old, l_old = carry
        ka = ksa_ref[0, 0, pl.ds(pl.multiple_of(c * NSA_KC, NSA_KC), NSA_KC), :]
        s = jnp.dot(ka, rhs, preferred_element_type=F32)
        m_new = jnp.maximum(m_old, jnp.max(s, axis=0, keepdims=True))
        alpha = jnp.exp(m_old - m_new)
        p = jnp.exp(s - m_new)
        acc_scr[...] = acc_scr[...] * alpha + jnp.dot(vst_ref[0, 0, c], p.astype(BF16),
                                                      preferred_element_type=F32)
        return m_new, l_old * alpha + jnp.sum(p, axis=0, keepdims=True)

    _, l_sel = lax.fori_loop(0, (qi * tq + NSA_KC - 1) // NSA_KC, chunk, (m0, l0))
    o_sel_t = acc_scr[...] / l_sel

    ntw = (WINDOW + tq) // tq
    t0 = jnp.maximum(qi - WINDOW // tq, 0)
    kw = kw_ref[0, 0, pl.ds(pl.multiple_of(t0 * tq, tq), ntw * tq), :]
    kpos = t0 * tq + lax.broadcasted_iota(jnp.int32, (ntw * tq, tq), 0)
    qpw = qi * tq + lax.broadcasted_iota(jnp.int32, (ntw * tq, tq), 1)
    bias_w = jnp.where((kpos <= qpw) & (kpos > qpw - WINDOW), 0.0, NEG)
    sw = jnp.dot(kw, qt, preferred_element_type=F32) + cols4(bias_w)
    pw = jnp.exp(sw - jnp.max(sw, axis=0, keepdims=True))
    lw = jnp.sum(pw, axis=0, keepdims=True)
    pwb = pw.astype(BF16)
    o_win_t = jnp.zeros((HEAD_DIM, NSA_COLS), F32)
    for i in range(ntw):
        o_win_t = o_win_t + jnp.dot(vwt_ref[0, 0, t0 + i], pwb[i * tq:(i + 1) * tq], preferred_element_type=F32)
    o_win_t = o_win_t / lw

    o_cmp_t = ocmp_ref[0, 0, 0]

    def cols(x, r):
        return x[:, r * tq:(r + 1) * tq]

    both = [jnp.concatenate([cols(o_sel_t, r), cols(o_win_t, r)], axis=0).T for r in range(NSA_REP)]
    cmp2 = [jnp.concatenate([cols(o_cmp_t, r), cols(o_cmp_t, r + 1)], axis=0).T for r in range(0, NSA_REP, 2)]
    o_sel = jnp.concatenate([x[:, :HEAD_DIM] for x in both], axis=1)
    o_win = jnp.concatenate([x[:, HEAD_DIM:] for x in both], axis=1)
    o_cmp = jnp.concatenate(cmp2, axis=1)
    ge = jnp.dot(_split_hi_lo(gate_ref[...]), gexp_ref[0], preferred_element_type=F32)
    w = NSA_REP * HEAD_DIM
    y_ref[...] = ge[:, 0:w] * o_cmp + ge[:, w:2 * w] * o_sel + ge[:, 2 * w:3 * w] * o_win


def _nsa_selt(qt, unsel_t, ocmp, gates, ksa, vst, vsd, kw, vwt, gexp, bsz, t):
    nq = t // NSA_TQ
    w = NSA_REP * HEAD_DIM
    full = lambda a: pl.BlockSpec((1, 1) + a.shape[2:], lambda b, g, i: (b, g) + (0,) * (a.ndim - 2))
    return pl.pallas_call(
        _nsa_selt_kernel,
        grid=(bsz, NSA_KV_HEADS, nq),
        in_specs=[
            pl.BlockSpec((1, 1, 1, HEAD_DIM, NSA_COLS), lambda b, g, i: (b, g, i, 0, 0)),
            pl.BlockSpec((1, 1, LANES, NSA_TQ), lambda b, g, i: (g, b * nq + i, 0, 0)),
            pl.BlockSpec((1, 1, 1, HEAD_DIM, NSA_COLS), lambda b, g, i: (b, g, i, 0, 0)),
            pl.BlockSpec((NSA_TQ, LANES), lambda b, g, i: (b * nq + i, 0)),
            full(ksa), full(vst), full(vsd), full(kw), full(vwt),
            pl.BlockSpec((1,) + gexp.shape[1:], lambda b, g, i: (g, 0, 0)),
        ],
        out_specs=pl.BlockSpec((NSA_TQ, w), lambda b, g, i: (b * nq + i, g)),
        out_shape=jax.ShapeDtypeStruct((bsz * t, NSA_WIDTH), F32),
        scratch_shapes=[pltpu.VMEM((HEAD_DIM, NSA_COLS), F32)],
        compiler_params=_cparams(("parallel", "parallel", "arbitrary")),
        name="nsa_sel_win",
    )(qt, unsel_t, ocmp, gates, ksa, vst, vsd, kw, vwt, gexp)


def _nsa_consts(t):
    ncp = t // CMP_STRIDE
    ns = t // SEL_LEN
    cs = np.arange(ncp) * CMP_STRIDE
    ss = np.arange(LANES) * SEL_LEN
    ov = np.clip(np.minimum(cs[:, None] + CMP_LEN, ss[None, :] + SEL_LEN) - np.maximum(cs[:, None], ss[None, :]), 0, None)
    cmap = (ov / CMP_LEN).astype(np.float32)
    cmap[ncp - 1:, :] = 0.0
    cmap[:, ns:] = 0.0
    cmap2 = jnp.asarray(np.concatenate([cmap, cmap], axis=0), dtype=BF16)
    key_blk = (np.arange(t) // SEL_LEN)[:, None]
    bige = jnp.asarray(np.where(np.arange(LANES)[None, :] == key_blk, -MASK_BIG, 0.0), dtype=BF16)
    ge = np.zeros((NSA_KV_HEADS, LANES, 3 * NSA_REP * HEAD_DIM), np.float32)
    for g in range(NSA_KV_HEADS):
        for r in range(NSA_REP):
            for j in range(3):
                c0 = j * NSA_REP * HEAD_DIM + r * HEAD_DIM
                ge[g, (g * NSA_REP + r) * 3 + j, c0:c0 + HEAD_DIM] = 1.0
    gexp = jnp.asarray(np.concatenate([ge, ge], axis=1), dtype=BF16)
    return cmap2, bige, gexp, cmap2.T


def _nsa_prompt(qn, qr, nsa, win, gates, wbig, pe8, consts, bsz, t):
    _, bige, gexp, cmap2t = consts
    ncp = t // CMP_STRIDE
    xc = nsa[:, :2 * NSA_KV_WIDTH].reshape(bsz, ncp, CMP_CHUNK_W)
    kcv = _nsa_compress(xc, wbig, pe8).reshape(bsz, ncp, 2, NSA_KV_HEADS, HEAD_DIM)
    kc = kcv[:, :, 0].transpose(0, 2, 1, 3).astype(BF16)
    vct = kcv[:, :, 1].transpose(0, 2, 3, 1).astype(BF16)
    nqt = t // NSA_TQC
    qnt = qn.reshape(bsz, nqt, NSA_TQC, NSA_KV_HEADS, NSA_REP, HEAD_DIM).transpose(0, 3, 1, 5, 4, 2)
    qnt = qnt.reshape(bsz, NSA_KV_HEADS, nqt, HEAD_DIM, NSA_REP * NSA_TQC).astype(BF16)
    ocmp, unsel_t = _nsa_cmp(qnt, kc, vct, cmap2t, bsz, t)
    nq = t // NSA_TQ
    qt = qr.reshape(bsz, nq, NSA_TQ, NSA_KV_HEADS, NSA_REP, HEAD_DIM).transpose(0, 3, 1, 5, 4, 2)
    qt = qt.reshape(bsz, NSA_KV_HEADS, nq, HEAD_DIM, NSA_COLS).astype(BF16)

    def rows_major(x):
        return x.reshape(bsz, t, NSA_KV_HEADS, HEAD_DIM).transpose(0, 2, 1, 3).astype(BF16)

    def tiles_t(x, tk):
        return x.reshape(bsz, t // tk, tk, NSA_KV_HEADS, HEAD_DIM).transpose(0, 3, 1, 4, 2).astype(BF16)

    ksa = jnp.concatenate([rows_major(nsa[:, 256:384]),
                           jnp.broadcast_to(bige, (bsz, NSA_KV_HEADS, t, LANES))], axis=-1)
    vst = tiles_t(nsa[:, 384:512], NSA_KC)
    vsd = tiles_t(nsa[:, 384:512], NSA_TQ)
    kw = rows_major(win[:, :128])
    vwt = tiles_t(win[:, 128:], NSA_TQ)
    return _nsa_selt(qt, unsel_t, ocmp, gates, ksa, vst, vsd, kw, vwt, gexp, bsz, t)


PAGE = 128
N_PAGES = PAST_LEN // PAGE
HROWS = 8
NT_DIMS = (((1,), (1,)), ((), ()))


def _sb_tri_rows():
    s = np.arange(PAGE)[:, None]
    j = np.arange(2 * PAGE)[None, :]
    u = -((s > j) | (j >= PAGE)).astype(np.float32)
    return jnp.asarray(np.concatenate([u, u], axis=0), dtype=BF16)


def _pages_t(cache):
    dp, npool = cache.shape[:2]
    return cache.transpose(0, 1, 3, 4, 5, 2).reshape(dp * npool, -1, PAGE)


def _page_specs(feat, blk):
    return [pl.BlockSpec((1, feat, PAGE), lambda b, pt, p=p: (pt[b, p], blk, 0)) for p in range(N_PAGES)]


def _dec_sb_kernel(pt_ref, q_ref, *rest):
    pages = rest[:N_PAGES]
    tri_ref, dmask_ref, y_ref = rest[N_PAGES:]
    q8 = q_ref[0]
    z = jnp.concatenate(
        [jnp.dot(q8, pg[0, 0:SB_WIDTH, :].astype(BF16), preferred_element_type=F32) for pg in pages],
        axis=0)
    sp = jnp.maximum(z, 0.0) + jnp.log(1.0 + jnp.exp(-jnp.abs(z)))
    rt = jnp.dot(_split_hi_lo(sp), tri_ref[...], preferred_element_type=F32)
    cs = [None] * N_PAGES
    c = jnp.zeros((HROWS, PAGE), F32)
    for p in range(N_PAGES - 1, -1, -1):
        cs[p] = c
        c = c + rt[p * HROWS:(p + 1) * HROWS, PAGE:]
    w = jnp.exp(z - sp + rt[:, :PAGE] + jnp.concatenate(cs, axis=0)).astype(BF16)
    y8 = jnp.zeros((HROWS, SB_WIDTH), F32)
    for p, pg in enumerate(pages):
        y8 = y8 + lax.dot_general(w[p * HROWS:(p + 1) * HROWS], pg[0, SB_WIDTH:, :].astype(BF16), NT_DIMS,
                                  preferred_element_type=F32)
    y_ref[0] = jnp.sum(y8 * dmask_ref[...], axis=0, keepdims=True)


def _dec_sb(rows, sq, cache2d):
    bd = sq.shape[0]
    hm = np.zeros((HROWS, SB_WIDTH), np.float32)
    for h in range(SB_HEADS):
        hm[h, h * HEAD_DIM:(h + 1) * HEAD_DIM] = 1.0
    q8 = ((sq * (1.0 / math.sqrt(HEAD_DIM)))[:, None, :] * hm[None]).astype(BF16)
    grid_spec = pltpu.PrefetchScalarGridSpec(
        num_scalar_prefetch=1, grid=(bd,),
        in_specs=[pl.BlockSpec((1, HROWS, SB_WIDTH), lambda b, pt: (b, 0, 0))] + _page_specs(2 * SB_WIDTH, 0)
        + [pl.BlockSpec((2 * PAGE, 2 * PAGE), lambda b, pt: (0, 0)),
           pl.BlockSpec((HROWS, SB_WIDTH), lambda b, pt: (0, 0))],
        out_specs=pl.BlockSpec((1, 1, SB_WIDTH), lambda b, pt: (b, 0, 0)))
    y = pl.pallas_call(
        _dec_sb_kernel, grid_spec=grid_spec,
        out_shape=jax.ShapeDtypeStruct((bd, 1, SB_WIDTH), F32),
        compiler_params=_cparams(("arbitrary",)),
        name="dec_sb",
    )(rows, q8, *([cache2d] * N_PAGES), _sb_tri_rows(), jnp.asarray(hm))
    return y.reshape(bd, SB_WIDTH)


def _dec_cmp_kernel(pt_ref, q_ref, *rest):
    pages = rest[:N_PAGES]
    w_ref, pe_ref, cmap_ref, perm_ref, o_ref, imp_ref = rest[N_PAGES:]
    width = 2 * NSA_KV_WIDTH
    cpp = PAGE // CMP_STRIDE
    ncp = N_PAGES * cpp
    perm = perm_ref[...]
    pp = [lax.dot_general(perm, pg[0].astype(BF16), NT_DIMS, preferred_element_type=F32) for pg in pages]
    a = jnp.zeros((ncp, 2 * width), F32)
    for s in range(CMP_STRIDE):
        xs = jnp.concatenate([x[s * cpp:(s + 1) * cpp] for x in pp], axis=0)
        a = a + jnp.dot(xs.astype(BF16), w_ref[s * width:(s + 1) * width, :], preferred_element_type=F32)
    pb = jnp.dot(pe_ref[...], w_ref[...], preferred_element_type=F32)
    bias = pb[0:1, :width] + pb[1:2, width:]
    kcv = a[:, :width] + pltpu.roll(a[:, width:], ncp - 1, 0) + bias
    q8 = q_ref[0]
    s8 = lax.dot_general(q8, kcv[:, :NSA_KV_WIDTH].astype(BF16), NT_DIMS, preferred_element_type=F32)
    valid = lax.broadcasted_iota(jnp.int32, s8.shape, 1) < ncp - 1
    s8 = jnp.where(valid, s8, NEG)
    e = jnp.exp(s8 - jnp.max(s8, axis=-1, keepdims=True))
    p = jnp.where(valid, e / jnp.sum(e, axis=-1, keepdims=True), 0.0)
    o_ref[0] = jnp.dot(p.astype(BF16), kcv[:, NSA_KV_WIDTH:].astype(BF16), preferred_element_type=F32)
    row = lax.broadcasted_iota(jnp.int32, p.shape, 0)
    psum = jnp.where(row < NSA_REP, jnp.sum(p[:NSA_REP], axis=0, keepdims=True),
                     jnp.sum(p[NSA_REP:], axis=0, keepdims=True))
    imp_ref[0] = jnp.dot(_split_hi_lo(psum), cmap_ref[...], preferred_element_type=F32)


def _head_rows(x, bd):
    x = x.reshape(bd, NSA_KV_HEADS, NSA_REP, 1, HEAD_DIM)
    eye = jnp.eye(NSA_KV_HEADS, dtype=x.dtype).reshape(1, NSA_KV_HEADS, 1, NSA_KV_HEADS, 1)
    return (x * eye).reshape(bd, NSA_HEADS, NSA_KV_WIDTH)


def _dec_cmp(rows, qn, cache2d, wbig, pe8, cmap2):
    bd = qn.shape[0]
    q8 = _head_rows(qn, bd).astype(BF16)
    cpp = PAGE // CMP_STRIDE
    pm = np.zeros((PAGE, PAGE), np.float32)
    for s in range(CMP_STRIDE):
        for n in range(cpp):
            pm[s * cpp + n, CMP_STRIDE * n + s] = 1.0
    perm = jnp.asarray(pm, dtype=BF16)
    grid_spec = pltpu.PrefetchScalarGridSpec(
        num_scalar_prefetch=1, grid=(bd,),
        in_specs=[pl.BlockSpec((1, NSA_HEADS, NSA_KV_WIDTH), lambda b, pt: (b, 0, 0))]
        + _page_specs(2 * NSA_KV_WIDTH, 0)
        + [pl.BlockSpec(wbig.shape, lambda b, pt: (0, 0)), pl.BlockSpec(pe8.shape, lambda b, pt: (0, 0)),
           pl.BlockSpec(cmap2.shape, lambda b, pt: (0, 0)), pl.BlockSpec((PAGE, PAGE), lambda b, pt: (0, 0))],
        out_specs=[pl.BlockSpec((1, NSA_HEADS, NSA_KV_WIDTH), lambda b, pt: (b, 0, 0)),
                   pl.BlockSpec((1, NSA_HEADS, LANES), lambda b, pt: (b, 0, 0))])
    return pl.pallas_call(
        _dec_cmp_kernel, grid_spec=grid_spec,
        out_shape=[jax.ShapeDtypeStruct((bd, NSA_HEADS, NSA_KV_WIDTH), F32),
                   jax.ShapeDtypeStruct((bd, NSA_HEADS, LANES), F32)],
        compiler_params=_cparams(("arbitrary",)),
        name="dec_cmp",
    )(rows, q8, *([cache2d] * N_PAGES), wbig, pe8, cmap2, perm)


def _dec_topk_kernel(imp_ref, unsel_ref):
    unsel_ref[...] = _topk_unselected(imp_ref[...], PAST_LEN // SEL_LEN)


def _dec_topk(imp):
    return pl.pallas_call(
        _dec_topk_kernel,
        out_shape=jax.ShapeDtypeStruct(imp.shape, F32),
        name="dec_topk",
    )(imp)


def _dec_sel_kernel(pt_ref, q_ref, unsel_ref, ocmp_ref, gate_ref, new_ref, win_ref, nwin_ref, *rest):
    pages = rest[:N_PAGES]
    bige_ref, y_ref = rest[N_PAGES:]
    q8 = q_ref[0]
    q8f = q8.astype(F32)

    def new_key_score(krow):
        return jnp.sum(q8f * krow.astype(BF16).astype(F32), axis=-1, keepdims=True)

    def attend(s3, s_new, vals_t, v_new):
        m = jnp.maximum(jnp.max(jnp.max(s3, axis=0), axis=-1, keepdims=True), s_new)
        p3 = jnp.exp(s3 - m[None])
        p_new = jnp.exp(s_new - m)
        den = jnp.sum(jnp.sum(p3, axis=0), axis=-1, keepdims=True) + p_new
        o = p_new * v_new.astype(BF16).astype(F32)
        for i, vt in enumerate(vals_t):
            o = o + lax.dot_general(p3[i].astype(BF16), vt, NT_DIMS, preferred_element_type=F32)
        return o / den

    bias = jnp.dot(unsel_ref[0].astype(BF16), bige_ref[...], preferred_element_type=F32)
    s_sel = jnp.stack(
        [jnp.dot(q8, pg[0, 0:NSA_KV_WIDTH, :].astype(BF16), preferred_element_type=F32)
         + bias[:, p * PAGE:(p + 1) * PAGE] for p, pg in enumerate(pages)], axis=0)
    new = new_ref[0]
    o_sel = attend(s_sel, new_key_score(new[:, 2 * NSA_KV_WIDTH:3 * NSA_KV_WIDTH]),
                   [pg[0, NSA_KV_WIDTH:, :].astype(BF16) for pg in pages], new[:, 3 * NSA_KV_WIDTH:])

    nt = WINDOW // PAGE
    kpos0 = lax.broadcasted_iota(jnp.int32, (HROWS, PAGE), 1) == 0
    s_win = []
    for i in range(nt):
        s = jnp.dot(q8, win_ref[0, 0:NSA_KV_WIDTH, i * PAGE:(i + 1) * PAGE].astype(BF16),
                    preferred_element_type=F32)
        s_win.append(jnp.where(kpos0, NEG, s) if i == 0 else s)
    nwin = nwin_ref[0]
    o_win = attend(jnp.stack(s_win, axis=0), new_key_score(nwin[:, :NSA_KV_WIDTH]),
                   [win_ref[0, NSA_KV_WIDTH:, i * PAGE:(i + 1) * PAGE].astype(BF16) for i in range(nt)],
                   nwin[:, NSA_KV_WIDTH:])

    g = gate_ref[0]
    y8 = g[:, 0:LANES] * ocmp_ref[0] + g[:, LANES:2 * LANES] * o_sel + g[:, 2 * LANES:] * o_win
    for h in range(NSA_HEADS):
        c0 = (h // NSA_REP) * HEAD_DIM
        y_ref[0, :, h * HEAD_DIM:(h + 1) * HEAD_DIM] = y8[h:h + 1, c0:c0 + HEAD_DIM]


def _dec_sel(rows, qr, unsel, ocmp, gates, new_nsa, win_state, new_win, cache2d, bige):
    bd = qr.shape[0]
    q8 = _head_rows(qr, bd).astype(BF16)
    g3 = jnp.broadcast_to(gates[:, :3 * NSA_HEADS].reshape(bd, NSA_HEADS, 3, 1),
                          (bd, NSA_HEADS, 3, LANES)).reshape(bd, NSA_HEADS, 3 * LANES)
    per_b = lambda *shape: pl.BlockSpec((1,) + shape, lambda b, pt: (b,) + (0,) * len(shape))
    grid_spec = pltpu.PrefetchScalarGridSpec(
        num_scalar_prefetch=1, grid=(bd,),
        in_specs=[per_b(NSA_HEADS, NSA_KV_WIDTH), per_b(NSA_HEADS, LANES), per_b(NSA_HEADS, NSA_KV_WIDTH),
                  per_b(NSA_HEADS, 3 * LANES), per_b(1, 4 * NSA_KV_WIDTH), per_b(2 * NSA_KV_WIDTH, WINDOW),
                  per_b(1, 2 * NSA_KV_WIDTH)]
        + _page_specs(2 * NSA_KV_WIDTH, 1)
        + [pl.BlockSpec(bige.shape, lambda b, pt: (0, 0))],
        out_specs=per_b(1, NSA_WIDTH))
    y = pl.pallas_call(
        _dec_sel_kernel, grid_spec=grid_spec,
        out_shape=jax.ShapeDtypeStruct((bd, 1, NSA_WIDTH), F32),
        compiler_params=_cparams(("arbitrary",)),
        name="dec_sel_win",
    )(rows, q8, unsel, ocmp, g3, new_nsa.reshape(bd, 1, -1), win_state, new_win.reshape(bd, 1, -1),
      *([cache2d] * N_PAGES), bige)
    return y.reshape(bd, NSA_WIDTH)


def _dec_consts():
    cmap2 = _nsa_consts(PAST_LEN)[0]
    blk = np.arange(PAST_LEN) // SEL_LEN
    bige = jnp.asarray(np.where(np.arange(LANES)[:, None] == blk[None, :], -MASK_BIG, 0.0), dtype=BF16)
    return cmap2, bige


def _cplx_affine(e1, e2):
    a1r, a1i, b1r, b1i = e1
    a2r, a2i, b2r, b2i = e2
    return (a2r * a1r - a2i * a1i, a2r * a1i + a2i * a1r,
            a2r * b1r - a2i * b1i + b2r, a2r * b1i + a2i * b1r + b2i)


def _s5_branch(u, s0_re, s0_im, a_re, a_im, log_dt, b_re, b_im, c_re, c_im, d):
    bsz, t = u.shape[:2]
    dt = jnp.exp(log_dt)[:, None]
    mag = jnp.exp(a_re * dt)
    lb_re, lb_im = mag * jnp.cos(a_im * dt), mag * jnp.sin(a_im * dt)
    den = a_re * a_re + a_im * a_im
    co_re = ((lb_re - 1.0) * a_re + lb_im * a_im) / den
    co_im = (lb_im * a_re - (lb_re - 1.0) * a_im) / den
    bb_re = co_re[..., None] * b_re - co_im[..., None] * b_im
    bb_im = co_re[..., None] * b_im + co_im[..., None] * b_re
    ug = u.reshape(bsz, t, SSM_GROUPS, SSM_GROUP)
    bu_re = jnp.einsum('btgp,gnp->btgn', ug, bb_re)
    bu_im = jnp.einsum('btgp,gnp->btgn', ug, bb_im)
    bu_re = bu_re.at[:, 0].add(lb_re * s0_re - lb_im * s0_im)
    bu_im = bu_im.at[:, 0].add(lb_re * s0_im + lb_im * s0_re)
    la_re = jnp.broadcast_to(lb_re, bu_re.shape)
    la_im = jnp.broadcast_to(lb_im, bu_im.shape)
    _, _, s_re, s_im = lax.associative_scan(_cplx_affine, (la_re, la_im, bu_re, bu_im), axis=1)
    y = (jnp.einsum('btgn,gpn->btgp', s_re, c_re) - jnp.einsum('btgn,gpn->btgp', s_im, c_im) + d * ug)
    return y.reshape(bsz, t, SSM_WIDTH), s_re[:, -1], s_im[:, -1]


def _sb_core(q, k, v, qpos, kpos):
    z = jnp.einsum('bqhd,bshd->bhqs', q, k) / math.sqrt(HEAD_DIM)
    mask = kpos[None, :] < qpos[:, None]
    log_1m = jnp.where(mask, jax.nn.log_sigmoid(-z), 0.0)
    between = lax.cumsum(log_1m, axis=3, reverse=True) - log_1m
    w = jnp.where(mask, jnp.exp(jax.nn.log_sigmoid(z) + between), 0.0)
    return jnp.einsum('bhqs,bshd->bqhd', w, v)


def _compress(rows, pe, w):
    bsz, length = rows.shape[:2]
    r = CMP_LEN // CMP_STRIDE
    n_chunks = length // CMP_STRIDE
    nc = n_chunks - r + 1
    chunks = rows[:, :n_chunks * CMP_STRIDE].reshape(bsz, n_chunks, CMP_STRIDE, NSA_KV_HEADS, HEAD_DIM)
    wj = w.reshape(r, CMP_STRIDE, HEAD_DIM, HEAD_DIM)
    pj = pe.reshape(r, CMP_STRIDE, HEAD_DIM)
    out = 0.0
    for j in range(r):
        out = out + jnp.einsum('bnsgd,sde->bnge', chunks[:, j:j + nc], wj[j]) + jnp.einsum('sd,sde->e', pj[j], wj[j])
    cend = jnp.arange(nc) * CMP_STRIDE + CMP_LEN - 1
    return out, cend


def _cmp_to_sel(nc, ns):
    cs = np.arange(nc) * CMP_STRIDE
    ss = np.arange(ns) * SEL_LEN
    ov = np.clip(np.minimum(cs[:, None] + CMP_LEN, ss[None, :] + SEL_LEN) - np.maximum(cs[:, None], ss[None, :]), 0, None)
    return jnp.asarray(ov / CMP_LEN, dtype=F32)


def _pad_blocks(a):
    pad = (-a.shape[1]) % SEL_LEN
    return jnp.pad(a, ((0, 0), (0, pad), (0, 0), (0, 0)))


def _nsa_prepare(k_cmp, v_cmp, k_sel, v_sel, w_cmp, pe_cmp):
    kc, cend = _compress(k_cmp, pe_cmp[0], w_cmp[0])
    vc, _ = _compress(v_cmp, pe_cmp[1], w_cmp[1])
    ks, vs = _pad_blocks(k_sel), _pad_blocks(v_sel)
    cmap = _cmp_to_sel(kc.shape[1], ks.shape[1] // SEL_LEN)
    return kc, vc, cend, cmap, ks, vs


def _nsa_core(qr, qn, g, qpos, kc, vc, cend, cmap, ks, vs, kw, vw, kwpos):
    bsz, nq = qr.shape[:2]
    qr = qr.reshape(bsz, nq, NSA_KV_HEADS, NSA_REP, HEAD_DIM)
    qn = qn.reshape(bsz, nq, NSA_KV_HEADS, NSA_REP, HEAD_DIM)
    mc = cend[None, :] <= qpos[:, None]
    sc = jnp.einsum('bqgrd,bcgd->bgrqc', qn, kc)
    pc = jax.nn.softmax(jnp.where(mc, sc, NEG), axis=-1) * mc
    o_cmp = jnp.einsum('bgrqc,bcgd->bqgrd', pc, vc)
    ns = cmap.shape[1]
    topk = min(SEL_TOPK, ns)
    imp = jnp.einsum('bgrqc,cn->bgqn', pc, cmap)
    blk = jnp.arange(ns)[None, :]
    cur = (qpos // SEL_LEN)[:, None]
    imp = jnp.where(blk == cur, FORCE, jnp.where(blk < cur, imp, -1.0))
    _, idx = lax.top_k(imp, topk)
    ksb = ks.reshape(bsz, ns, SEL_LEN, NSA_KV_HEADS, HEAD_DIM).transpose(0, 3, 1, 2, 4)
    vsb = vs.reshape(bsz, ns, SEL_LEN, NSA_KV_HEADS, HEAD_DIM).transpose(0, 3, 1, 2, 4)
    bi = jnp.arange(bsz)[:, None, None, None]
    gi = jnp.arange(NSA_KV_HEADS)[None, :, None, None]
    kg, vg = ksb[bi, gi, idx], vsb[bi, gi, idx]
    spos = idx[..., None] * SEL_LEN + jnp.arange(SEL_LEN)
    ms = spos <= qpos[None, None, :, None, None]
    ss = jnp.einsum('bqgrd,bgqkld->bgrqkl', qr, kg)
    ss = jnp.where(ms[:, :, None], ss, NEG).reshape(bsz, NSA_KV_HEADS, NSA_REP, nq, topk * SEL_LEN)
    ps = jax.nn.softmax(ss, axis=-1).reshape(bsz, NSA_KV_HEADS, NSA_REP, nq, topk, SEL_LEN)
    o_sel = jnp.einsum('bgrqkl,bgqkld->bqgrd', ps, vg)
    mw = (kwpos[None, :] <= qpos[:, None]) & (kwpos[None, :] > qpos[:, None] - WINDOW) & (kwpos[None, :] >= 0)
    sw = jnp.einsum('bqgrd,bwgd->bgrqw', qr, kw)
    pw = jax.nn.softmax(jnp.where(mw, sw, NEG), axis=-1)
    o_win = jnp.einsum('bgrqw,bwgd->bqgrd', pw, vw)
    g = g.reshape(bsz, nq, NSA_KV_HEADS, NSA_REP, 3)
    o = g[..., 0:1] * o_cmp + g[..., 1:2] * o_sel + g[..., 2:3] * o_win
    return o.reshape(bsz, nq, NSA_WIDTH)


def _rope_tables(pos):
    half = HEAD_DIM // 2
    freqs = ROPE_THETA ** (-jnp.arange(half, dtype=F32) / half)
    ang = pos.astype(F32)[:, None] * freqs
    cos, sin = jnp.cos(ang), jnp.sin(ang)
    cos2 = jnp.concatenate([cos, cos], axis=-1)
    sin2 = jnp.concatenate([-sin, sin], axis=-1)
    return jnp.tile(cos2, (1, LANES // HEAD_DIM)), jnp.tile(sin2, (1, LANES // HEAD_DIM))


def _pack_w_in(w):
    pad = jnp.zeros((D_MODEL, LANES - 3 * NSA_HEADS), w.dtype)
    return jnp.concatenate([w[:, :2304], w[:, 2304:2328], pad, w[:, 2328:]], axis=1).astype(BF16)


def kernel(x_prompt, x_sample, cache_sb, cache_nsa, state_win, state_ssm, page_table, c_prompt, c_sample,
           w_ada, b_ada, norm_mix, norm_ffn, w_in, ssm_a_re, ssm_a_im, ssm_log_dt, ssm_b_re, ssm_b_im,
           ssm_c_re, ssm_c_im, ssm_d, w_glu, b_glu, nsa_w_cmp, nsa_pe_cmp, w_branch, w_out,
           w_grp, b_grp, w_rt, b_rt, w_e_gate, w_e_up, w_e_down, final_norm):
    bp, t = x_prompt.shape[:2]
    bd, s = x_sample.shape[:2]
    np_tok = bp * t
    win_len = state_win.shape[2]
    tm_p = 256
    tiles_pb = t // tm_p
    tm_f = 1024
    tiles_fb = t // tm_f

    xp = x_prompt.reshape(np_tok, D_MODEL)
    xs = x_sample.reshape(bd * s, D_MODEL)
    cos_p, sin_p = _rope_tables(jnp.arange(t))
    cos_s, sin_s = _rope_tables(jnp.full((bd,), PAST_LEN))
    pos_s = PAST_LEN + jnp.arange(s)
    nsa_consts = _nsa_consts(t)
    dec_cmap2, dec_bige = _dec_consts()
    n_pool = cache_sb.shape[1]
    cache_sb2d = _pages_t(cache_sb)
    cache_nsa2d = _pages_t(cache_nsa)
    win_t = state_win.transpose(0, 1, 3, 4, 5, 2).reshape(DEPTH, bd, 2 * NSA_KV_WIDTH, win_len)

    sb_p, sb_s, nsa_p, nsa_s, win_p, win_s, ssm_p, ssm_s = [], [], [], [], [], [], [], []
    for l in range(DEPTH):
        w_in_l = _pack_w_in(w_in[l])
        wb_l = w_branch[l].astype(BF16)
        wo_l = w_out[l].astype(BF16)
        wr_l = jnp.concatenate([w_grp[l], w_rt[l], jnp.zeros((D_MODEL, LANES - 20), F32)], axis=1)
        br_l = jnp.concatenate([b_grp[l], b_rt[l], jnp.zeros((LANES - 20,), F32)])[None, :]
        wg_l, wu_l, wd_l = w_e_gate[l].astype(BF16), w_e_up[l].astype(BF16), w_e_down[l].astype(BF16)
        gn_mix = norm_mix[l][None, :]
        gn_ffn = norm_ffn[l][None, :]
        ssm_l = (ssm_a_re[l], ssm_a_im[l], ssm_log_dt[l], ssm_b_re[l], ssm_b_im[l], ssm_c_re[l], ssm_c_im[l],
                 ssm_d[l])
        wglu_l = w_glu[l].astype(BF16)
        bglu_l = b_glu[l][None, :]

        mp = jnp.split(jax.nn.silu(c_prompt) @ w_ada[l] + b_ada[l], 6, axis=-1)
        mp = [m[:, None, :] for m in mp]
        u, sq, skv, qn, qr, nsa, win, ng, g = _inproj(
            xp, mp[0], mp[1], gn_mix, cos_p, sin_p, w_in_l, tm_p,
            lambda i: i // tiles_pb, lambda i: i % tiles_pb)
        y_a, sf = _s5_prompt(u.reshape(bp, t, SSM_WIDTH), _s5_mats(*ssm_l[:7]), ssm_d[l])
        y_b = _sb_prompt(sq, skv, bp, t)
        wbig, pe8 = _cmp_weights(nsa_w_cmp[l], nsa_pe_cmp[l])
        y_c = _nsa_prompt(qn, qr, nsa, win, ng, wbig, pe8, nsa_consts, bp, t)
        xp = _merge(xp, mp[2], y_a.reshape(np_tok, SSM_WIDTH), y_b, y_c, g, wglu_l, bglu_l, wb_l, wo_l,
                    tm_p, lambda i: i // tiles_pb)
        xp = _ffn(xp, mp[3], mp[4], mp[5], gn_ffn, wr_l, br_l, wg_l, wu_l, wd_l, tm_f, lambda i: i // tiles_fb)
        sb_p.append(skv.reshape(bp, t, 2, SB_HEADS, HEAD_DIM))
        nsa_p.append(nsa.reshape(bp, t, 4, NSA_KV_HEADS, HEAD_DIM))
        win_p.append(win.reshape(bp, t, 2, NSA_KV_HEADS, HEAD_DIM)[:, t - min(WINDOW, t):])
        ssm_p.append(sf)

        ms = jnp.split(jax.nn.silu(c_sample) @ w_ada[l] + b_ada[l], 6, axis=-1)
        ms = [m[None, :, :] for m in ms]
        u, sq, skv, qn, qr, nsa, win, ng, g = _inproj(
            xs, ms[0], ms[1], gn_mix, cos_s, sin_s, w_in_l, bd, lambda i: 0, lambda i: 0)
        y_a, sr, si = _s5_branch(u.reshape(bd, s, SSM_WIDTH), state_ssm[l][:, 0], state_ssm[l][:, 1], *ssm_l)
        new_sb = skv.reshape(bd, s, 2, SB_HEADS, HEAD_DIM)
        rows_l = page_table + l * n_pool
        y_b = _dec_sb(rows_l, sq, cache_sb2d)
        new_nsa = nsa.reshape(bd, s, 4, NSA_KV_HEADS, HEAD_DIM)
        ocmp, imp = _dec_cmp(rows_l, qn, cache_nsa2d, wbig, pe8, dec_cmap2)
        unsel = _dec_topk(imp.reshape(bd * NSA_HEADS, LANES)).reshape(bd, NSA_HEADS, LANES)
        new_win = win.reshape(bd, s, 2, NSA_KV_HEADS, HEAD_DIM)
        winc = jnp.concatenate([state_win[l], new_win], axis=1)
        y_c = _dec_sel(rows_l, qr, unsel, ocmp, ng, nsa, win_t[l], win, cache_nsa2d, dec_bige)
        xs = _merge(xs, ms[2], y_a.reshape(bd * s, SSM_WIDTH), y_b, y_c, g, wglu_l, bglu_l, wb_l, wo_l,
                    bd, lambda i: 0)
        xs = _ffn(xs, ms[3], ms[4], ms[5], gn_ffn, wr_l, br_l, wg_l, wu_l, wd_l, bd, lambda i: 0)
        sb_s.append(new_sb)
        nsa_s.append(new_nsa)
        win_s.append(winc[:, s:])
        ssm_s.append(jnp.stack([sr, si], axis=1))

    fn = final_norm[None, :]
    y_prompt = _final_norm(xp, fn, tm_f).reshape(bp, t, D_MODEL)
    y_sample = _final_norm(xs, fn, bd).reshape(bd, s, D_MODEL)
    return (y_prompt, y_sample,
            jnp.stack(sb_p), jnp.stack(sb_s),
            jnp.stack(nsa_p), jnp.stack(nsa_s),
            jnp.stack(win_p), jnp.stack(win_s),
            jnp.stack(ssm_p), jnp.stack(ssm_s))
```

```python
import functools
import math

import jax
import jax.numpy as jnp
import numpy as np
from jax import lax
from jax.experimental import pallas as pl
from jax.experimental.pallas import tpu as pltpu

F32 = jnp.float32
BF16 = jnp.bfloat16

D_MODEL = 1024
DEPTH = 4
PAST_LEN = 2048
HEAD_DIM = 64
SSM_WIDTH = 256
SSM_GROUP = 16
SSM_GROUPS = 16
SSM_STATE = 64
SB_HEADS = 4
SB_WIDTH = 256
NSA_HEADS = 8
NSA_KV_HEADS = 2
NSA_REP = 4
NSA_WIDTH = 512
NSA_KV_WIDTH = 128
CMP_LEN = 32
CMP_STRIDE = 16
SEL_LEN = 64
SEL_TOPK = 16
WINDOW = 512
N_BRANCH = 3
Q_BLOCK = 128
ROPE_THETA = 10000.0
MOE_GROUPS = 4
EXPERTS_PER_GROUP = 4
N_EXPERTS = 16
EXPERT_FF = 256
EPS = 1e-6
NEG = -1e30
FORCE = 1e9

LANES = 128
VMEM_LIMIT = 56 * 1024 * 1024

_C_U = 0
_C_SQ = 256
_C_SKV = 512
_C_NQ = 1024
_C_NKV = 1536
_C_NG = 2304
_C_MG = 2432
_C_END = 5504


def _cparams(sem):
    return pltpu.CompilerParams(dimension_semantics=sem, vmem_limit_bytes=VMEM_LIMIT)


def _rope_slab(v, cos, sin_signed):
    lane = lax.broadcasted_iota(jnp.int32, v.shape, 1)
    first = (lane % HEAD_DIM) < (HEAD_DIM // 2)
    swapped = jnp.where(first, pltpu.roll(v, LANES - HEAD_DIM // 2, 1), pltpu.roll(v, HEAD_DIM // 2, 1))
    return v * cos + swapped * sin_signed


def _inproj_kernel(x_ref, shift_ref, scale_ref, gn_ref, cos_ref, sin_ref, w_ref,
                   u_ref, sq_ref, skv_ref, qn_ref, qr_ref, nsa_ref, win_ref, ng_ref, g_ref):
    x = x_ref[...]
    ms = jnp.mean(x * x, axis=-1, keepdims=True)
    h = x * lax.rsqrt(ms + EPS) * gn_ref[...]
    h = h * (1.0 + scale_ref[0]) + shift_ref[0]
    hb = h.astype(BF16)

    def mm(lo, hi):
        return jnp.dot(hb, w_ref[:, lo:hi], preferred_element_type=F32)

    cos = cos_ref[...]
    sin = sin_ref[...]
    u_ref[...] = mm(_C_U, _C_SQ)
    sq_ref[...] = mm(_C_SQ, _C_SKV)
    skv_ref[...] = mm(_C_SKV, _C_NQ)
    q = mm(_C_NQ, _C_NKV) * (1.0 / math.sqrt(HEAD_DIM))
    qn_ref[...] = q
    for s in range(NSA_WIDTH // LANES):
        qr_ref[:, s * LANES:(s + 1) * LANES] = _rope_slab(q[:, s * LANES:(s + 1) * LANES], cos, sin)
    kv = mm(_C_NKV, _C_NG)
    nsa_ref[:, 0:256] = kv[:, 0:256]
    nsa_ref[:, 256:384] = _rope_slab(kv[:, 256:384], cos, sin)
    nsa_ref[:, 384:512] = kv[:, 384:512]
    win_ref[:, 0:128] = _rope_slab(kv[:, 512:640], cos, sin)
    win_ref[:, 128:256] = kv[:, 640:768]
    ng_ref[...] = jax.nn.sigmoid(mm(_C_NG, _C_MG))
    g_ref[...] = jax.nn.sigmoid(mm(_C_MG, _C_END)).astype(g_ref.dtype)


def _inproj(x, shift, scale, gn, cos, sin, w, tm, mod_map, pos_map):
    n = x.shape[0]
    mrows = shift.shape[1]
    row = lambda width: pl.BlockSpec((tm, width), lambda i: (i, 0))
    widths = (256, 256, 512, 512, 512, 512, 256, 128, 3072)
    return pl.pallas_call(
        _inproj_kernel,
        grid=(n // tm,),
        in_specs=[
            row(D_MODEL),
            pl.BlockSpec((1, mrows, D_MODEL), lambda i: (mod_map(i), 0, 0)),
            pl.BlockSpec((1, mrows, D_MODEL), lambda i: (mod_map(i), 0, 0)),
            pl.BlockSpec((1, D_MODEL), lambda i: (0, 0)),
            pl.BlockSpec((tm, LANES), lambda i: (pos_map(i), 0)),
            pl.BlockSpec((tm, LANES), lambda i: (pos_map(i), 0)),
            pl.BlockSpec((D_MODEL, _C_END), lambda i: (0, 0)),
        ],
        out_specs=[row(wd) for wd in widths],
        out_shape=[jax.ShapeDtypeStruct((n, wd), BF16 if k == len(widths) - 1 else F32)
                   for k, wd in enumerate(widths)],
        compiler_params=_cparams(("parallel",)),
        name="inproj",
    )(x, shift, scale, gn, cos, sin, w)


def _merge_kernel(x_ref, gate_ref, ya_ref, yb_ref, yc_ref, g_ref, wglu_ref, bglu_ref, wb_ref, wo_ref, o_ref):
    def mm(a, w):
        return jnp.dot(a.astype(BF16), w, preferred_element_type=F32)

    ya = jax.nn.gelu(ya_ref[...])
    ya = ya * jax.nn.sigmoid(mm(ya, wglu_ref[...]) + bglu_ref[...])
    br_a = mm(ya, wb_ref[0:SSM_WIDTH, :])
    br_b = mm(yb_ref[...], wb_ref[SSM_WIDTH:SSM_WIDTH + SB_WIDTH, :])
    br_c = mm(yc_ref[...], wb_ref[SSM_WIDTH + SB_WIDTH:, :])
    merged = (g_ref[:, 0:D_MODEL] * br_a + g_ref[:, D_MODEL:2 * D_MODEL] * br_b
              + g_ref[:, 2 * D_MODEL:] * br_c)
    out = mm(merged, wo_ref[...])
    o_ref[...] = x_ref[...] + gate_ref[0] * out


def _merge(x, gate, ya, yb, yc, g, wglu, bglu, wb, wo, tm, mod_map):
    n = x.shape[0]
    mrows = gate.shape[1]
    row = lambda width: pl.BlockSpec((tm, width), lambda i: (i, 0))
    return pl.pallas_call(
        _merge_kernel,
        grid=(n // tm,),
        in_specs=[
            row(D_MODEL),
            pl.BlockSpec((1, mrows, D_MODEL), lambda i: (mod_map(i), 0, 0)),
            row(SSM_WIDTH), row(SB_WIDTH), row(NSA_WIDTH), row(N_BRANCH * D_MODEL),
            pl.BlockSpec((SSM_WIDTH, SSM_WIDTH), lambda i: (0, 0)),
            pl.BlockSpec((1, SSM_WIDTH), lambda i: (0, 0)),
            pl.BlockSpec((D_MODEL, D_MODEL), lambda i: (0, 0)),
            pl.BlockSpec((D_MODEL, D_MODEL), lambda i: (0, 0)),
        ],
        out_specs=row(D_MODEL),
        out_shape=jax.ShapeDtypeStruct((n, D_MODEL), F32),
        compiler_params=_cparams(("parallel",)),
        name="merge",
    )(x, gate, ya, yb, yc, g, wglu, bglu, wb, wo)


def _route(logits):
    lane = lax.broadcasted_iota(jnp.int32, logits.shape, 1)
    big = jnp.int32(1 << 20)
    is_g = lane < MOE_GROUPS
    gl = jnp.where(is_g, logits, NEG)
    gmax = jnp.max(gl, axis=-1, keepdims=True)
    p_sel = 1.0 / jnp.sum(jnp.where(is_g, jnp.exp(gl - gmax), 0.0), axis=-1, keepdims=True)
    g_sel = jnp.min(jnp.where(is_g & (gl == gmax), lane, big), axis=-1, keepdims=True)
    e_idx = lane - MOE_GROUPS
    in_grp = (e_idx >= g_sel * EXPERTS_PER_GROUP) & (e_idx < (g_sel + 1) * EXPERTS_PER_GROUP)
    el = jnp.where(in_grp, logits, NEG)
    m1 = jnp.max(el, axis=-1, keepdims=True)
    i1 = jnp.min(jnp.where(in_grp & (el == m1), lane, big), axis=-1, keepdims=True)
    el2 = jnp.where(lane == i1, NEG, el)
    m2 = jnp.max(el2, axis=-1, keepdims=True)
    i2 = jnp.min(jnp.where(in_grp & (lane != i1) & (el2 == m2), lane, big), axis=-1, keepdims=True)
    e2 = jnp.exp(m2 - m1)
    w1 = p_sel / (1.0 + e2)
    w2 = p_sel * e2 / (1.0 + e2)
    return jnp.where(lane == i1, w1, jnp.where(lane == i2, w2, 0.0))


def _ffn_kernel(x_ref, shift_ref, scale_ref, gate_ref, gn_ref, wr_ref, br_ref, wg_ref, wu_ref, wd_ref,
                o_ref, h_scr, comb_scr, acc_scr):
    e = pl.program_id(1)

    @pl.when(e == 0)
    def _():
        x = x_ref[...]
        ms = jnp.mean(x * x, axis=-1, keepdims=True)
        h = x * lax.rsqrt(ms + EPS) * gn_ref[...]
        h = h * (1.0 + scale_ref[0]) + shift_ref[0]
        logits = jnp.dot(h, wr_ref[...], preferred_element_type=F32,
                         precision=lax.Precision.HIGHEST) + br_ref[...]
        comb_scr[...] = _route(logits)
        h_scr[...] = h.astype(BF16)
        acc_scr[...] = jnp.zeros_like(acc_scr)

    hb = h_scr[...]
    a = jnp.dot(hb, wg_ref[0], preferred_element_type=F32)
    b = jnp.dot(hb, wu_ref[0], preferred_element_type=F32)
    comb = comb_scr[...]
    lane = lax.broadcasted_iota(jnp.int32, comb.shape, 1)
    tm = comb.shape[0]
    c = jnp.concatenate(
        [jnp.broadcast_to(jnp.sum(jnp.where(lane == MOE_GROUPS + e * EXPERTS_PER_GROUP + j, comb, 0.0),
                                  axis=-1, keepdims=True), (tm, EXPERT_FF))
         for j in range(EXPERTS_PER_GROUP)], axis=1)
    hid = (a * jax.nn.sigmoid(a)) * b * c
    acc_scr[...] += jnp.dot(hid.astype(BF16), wd_ref[0], preferred_element_type=F32)

    @pl.when(e == MOE_GROUPS - 1)
    def _():
        o_ref[...] = x_ref[...] + gate_ref[0] * acc_scr[...]


def _ffn(x, shift, scale, gate, gn, wr, br, wg, wu, wd, tm, mod_map):
    n = x.shape[0]
    mrows = shift.shape[1]
    mod = pl.BlockSpec((1, mrows, D_MODEL), lambda i, e: (mod_map(i), 0, 0))
    gff = EXPERTS_PER_GROUP * EXPERT_FF
    return pl.pallas_call(
        _ffn_kernel,
        grid=(n // tm, MOE_GROUPS),
        in_specs=[
            pl.BlockSpec((tm, D_MODEL), lambda i, e: (i, 0)),
            mod, mod, mod,
            pl.BlockSpec((1, D_MODEL), lambda i, e: (0, 0)),
            pl.BlockSpec((D_MODEL, LANES), lambda i, e: (0, 0)),
            pl.BlockSpec((1, LANES), lambda i, e: (0, 0)),
            pl.BlockSpec((1, D_MODEL, gff), lambda i, e: (e, 0, 0)),
            pl.BlockSpec((1, D_MODEL, gff), lambda i, e: (e, 0, 0)),
            pl.BlockSpec((1, gff, D_MODEL), lambda i, e: (e, 0, 0)),
        ],
        out_specs=pl.BlockSpec((tm, D_MODEL), lambda i, e: (i, 0)),
        out_shape=jax.ShapeDtypeStruct((n, D_MODEL), F32),
        scratch_shapes=[pltpu.VMEM((tm, D_MODEL), BF16), pltpu.VMEM((tm, LANES), F32),
                        pltpu.VMEM((tm, D_MODEL), F32)],
        compiler_params=_cparams(("parallel", "arbitrary")),
        name="ffn_moe",
    )(x, shift, scale, gate, gn, wr, br, wg, wu, wd)


def _final_norm_kernel(x_ref, g_ref, o_ref):
    x = x_ref[...]
    o_ref[...] = x * lax.rsqrt(jnp.mean(x * x, axis=-1, keepdims=True) + EPS) * g_ref[...]


def _final_norm(x, g, tm):
    n = x.shape[0]
    return pl.pallas_call(
        _final_norm_kernel,
        grid=(n // tm,),
        in_specs=[pl.BlockSpec((tm, D_MODEL), lambda i: (i, 0)), pl.BlockSpec((1, D_MODEL), lambda i: (0, 0))],
        out_specs=pl.BlockSpec((tm, D_MODEL), lambda i: (i, 0)),
        out_shape=jax.ShapeDtypeStruct((n, D_MODEL), F32),
        compiler_params=_cparams(("parallel",)),
        name="final_norm",
    )(x, g)


S5_CHUNK = 64
S5_LW = S5_CHUNK * SSM_GROUP


def _s5_mats(a_re, a_im, log_dt, b_re, b_im, c_re, c_im):
    L = S5_CHUNK
    hp = lax.Precision.HIGHEST
    dt = jnp.exp(log_dt)[:, None]
    k = jnp.arange(L + 1, dtype=F32)[:, None, None]
    mag = jnp.exp(a_re * dt * k)
    ang = a_im * dt * k
    lk_re, lk_im = mag * jnp.cos(ang), mag * jnp.sin(ang)
    lb_re, lb_im = lk_re[1], lk_im[1]
    den = a_re * a_re + a_im * a_im
    co_re = ((lb_re - 1.0) * a_re + lb_im * a_im) / den
    co_im = (lb_im * a_re - (lb_re - 1.0) * a_im) / den
    bb_re = co_re[..., None] * b_re - co_im[..., None] * b_im
    bb_im = co_re[..., None] * b_im + co_im[..., None] * b_re
    cl_re = c_re[None] * lk_re[:, :, None, :] - c_im[None] * lk_im[:, :, None, :]
    cl_im = c_re[None] * lk_im[:, :, None, :] + c_im[None] * lk_re[:, :, None, :]
    kk = (jnp.einsum('kgpn,gnq->kgpq', cl_re[:L], bb_re, precision=hp)
          - jnp.einsum('kgpn,gnq->kgpq', cl_im[:L], bb_im, precision=hp))
    kq = kk.transpose(1, 3, 0, 2).reshape(SSM_GROUPS, SSM_GROUP, S5_LW).astype(BF16)
    kp = jnp.concatenate([jnp.zeros_like(kq), kq], axis=-1)
    tmat = jnp.stack([kp[:, :, S5_LW - SSM_GROUP * i:2 * S5_LW - SSM_GROUP * i] for i in range(L)], axis=1)
    tmat = tmat.reshape(SSM_GROUPS, S5_LW, S5_LW)
    rev = L - 1 - jnp.arange(L)
    be_re = lk_re[rev][:, :, :, None] * bb_re[None] - lk_im[rev][:, :, :, None] * bb_im[None]
    be_im = lk_re[rev][:, :, :, None] * bb_im[None] + lk_im[rev][:, :, :, None] * bb_re[None]
    bmat = jnp.concatenate([be_re.transpose(1, 0, 3, 2).reshape(SSM_GROUPS, S5_LW, SSM_STATE),
                            be_im.transpose(1, 0, 3, 2).reshape(SSM_GROUPS, S5_LW, SSM_STATE)], axis=-1)
    cm_re = cl_re[1:].transpose(1, 3, 0, 2).reshape(SSM_GROUPS, SSM_STATE, S5_LW)
    cm_im = cl_im[1:].transpose(1, 3, 0, 2).reshape(SSM_GROUPS, SSM_STATE, S5_LW)
    cmat = jnp.concatenate([cm_re, -cm_im], axis=1)
    lbl = jnp.stack([lk_re[L], lk_im[L]], axis=1)
    return tmat.astype(BF16), bmat.astype(BF16), cmat.astype(BF16), lbl


def _s5_kernel(x_ref, t_ref, b_ref, c_ref, lbl_ref, d_ref, y_ref, sf_ref, er_scr, ei_scr, pr_scr, pi_scr):
    nb, _, nch, _ = x_ref.shape
    a_re = lbl_ref[0, 0:1, :]
    a_im = lbl_ref[0, 1:2, :]
    for b in range(nb):
        x = x_ref[b, 0]
        xb = x.astype(BF16)
        e = jnp.dot(xb, b_ref[0], preferred_element_type=F32)
        er_scr[...] = e[:, :SSM_STATE]
        ei_scr[...] = e[:, SSM_STATE:]

        def step(c, carry):
            s_re, s_im = carry
            pr_scr[pl.ds(c, 1), :] = s_re
            pi_scr[pl.ds(c, 1), :] = s_im
            n_re = a_re * s_re - a_im * s_im + er_scr[pl.ds(c, 1), :]
            n_im = a_re * s_im + a_im * s_re + ei_scr[pl.ds(c, 1), :]
            return n_re, n_im

        zero = jnp.zeros((1, SSM_STATE), F32)
        s_re, s_im = lax.fori_loop(0, nch, step, (zero, zero))
        sprev = jnp.concatenate([pr_scr[...], pi_scr[...]], axis=1).astype(BF16)
        y = (jnp.dot(xb, t_ref[0], preferred_element_type=F32)
             + jnp.dot(sprev, c_ref[0], preferred_element_type=F32) + d_ref[0] * x)
        y_ref[b, 0] = y
        sf_ref[b, 0] = jnp.concatenate([s_re, s_im], axis=1)


def _s5_prompt(u, mats, d):
    tmat, bmat, cmat, lbl = mats
    bsz, t = u.shape[:2]
    nch = t // S5_CHUNK
    x = u.reshape(bsz, nch, S5_CHUNK, SSM_GROUPS, SSM_GROUP).transpose(0, 3, 1, 2, 4).reshape(
        bsz, SSM_GROUPS, nch, S5_LW)
    dt = jnp.tile(d, (1, S5_CHUNK))[:, None, :]
    y, sf = pl.pallas_call(
        _s5_kernel,
        grid=(SSM_GROUPS,),
        in_specs=[
            pl.BlockSpec((bsz, 1, nch, S5_LW), lambda g: (0, g, 0, 0)),
            pl.BlockSpec((1, S5_LW, S5_LW), lambda g: (g, 0, 0)),
            pl.BlockSpec((1, S5_LW, 2 * SSM_STATE), lambda g: (g, 0, 0)),
            pl.BlockSpec((1, 2 * SSM_STATE, S5_LW), lambda g: (g, 0, 0)),
            pl.BlockSpec((1, 2, SSM_STATE), lambda g: (g, 0, 0)),
            pl.BlockSpec((1, 1, S5_LW), lambda g: (g, 0, 0)),
        ],
        out_specs=[pl.BlockSpec((bsz, 1, nch, S5_LW), lambda g: (0, g, 0, 0)),
                   pl.BlockSpec((bsz, 1, 1, 2 * SSM_STATE), lambda g: (0, g, 0, 0))],
        out_shape=[jax.ShapeDtypeStruct((bsz, SSM_GROUPS, nch, S5_LW), F32),
                   jax.ShapeDtypeStruct((bsz, SSM_GROUPS, 1, 2 * SSM_STATE), F32)],
        scratch_shapes=[pltpu.VMEM((nch, SSM_STATE), F32)] * 4,
        compiler_params=_cparams(("arbitrary",)),
        name="s5_prompt",
    )(x, tmat, bmat, cmat, lbl, dt)
    y = y.reshape(bsz, SSM_GROUPS, nch, S5_CHUNK, SSM_GROUP).transpose(0, 2, 3, 1, 4).reshape(bsz, t, SSM_WIDTH)
    sf = sf.reshape(bsz, SSM_GROUPS, 2, SSM_STATE).transpose(0, 2, 1, 3)
    return y, sf


SB_TK = 128
SB_TQ = 256
SB_PER = SB_TQ // SB_TK


def _sb_tri():
    j = np.arange(SB_TK)[:, None]
    s = np.arange(SB_TK)[None, :]
    return jnp.asarray(-(s > j).astype(np.float32), dtype=BF16)


def _sb_kernel(qt_ref, k_ref, vt_ref, tri_ref, o_ref, acc_scr, c_scr):
    qi = pl.program_id(1)
    tri = tri_ref[...]
    kidx = lax.broadcasted_iota(jnp.int32, (SB_TK, SB_TQ), 0)
    qidx = lax.broadcasted_iota(jnp.int32, (SB_TK, SB_TQ), 1)

    def softplus_tile(h, j, mask):
        kt = k_ref[0, h, pl.ds(pl.multiple_of(j * SB_TK, SB_TK), SB_TK), :]
        z = jnp.dot(kt, qt_ref[0, h], preferred_element_type=F32)
        sp = jnp.maximum(z, 0.0) + jnp.log(1.0 + jnp.exp(-jnp.abs(z)))
        if mask is not None:
            sp = jnp.where(mask, sp, 0.0)
        return z, sp

    def weights(z, sp, c, mask):
        r = jnp.dot(tri, sp.astype(BF16), preferred_element_type=F32)
        w = jnp.exp(z - sp + r - c)
        if mask is not None:
            w = jnp.where(mask, w, 0.0)
        return w.astype(BF16)

    def two_tiles(j_hi, masks):
        heads = range(SB_HEADS)
        zs1 = [softplus_tile(h, j_hi, masks[0]) for h in heads]
        zs0 = [softplus_tile(h, j_hi - 1, masks[1]) for h in heads]
        cs = [c_scr[h] for h in heads]
        t1 = [jnp.sum(zs1[h][1], axis=0, keepdims=True) for h in heads]
        w1 = [weights(zs1[h][0], zs1[h][1], cs[h], masks[0]) for h in heads]
        w0 = [weights(zs0[h][0], zs0[h][1], cs[h] + t1[h], masks[1]) for h in heads]
        for h in heads:
            acc_scr[h] += (jnp.dot(vt_ref[0, h, j_hi], w1[h], preferred_element_type=F32)
                           + jnp.dot(vt_ref[0, h, j_hi - 1], w0[h], preferred_element_type=F32))
            c_scr[h] = cs[h] + t1[h] + jnp.sum(zs0[h][1], axis=0, keepdims=True)

    acc_scr[...] = jnp.zeros_like(acc_scr)
    c_scr[...] = jnp.zeros_like(c_scr)
    two_tiles(SB_PER * qi + 1, (kidx + SB_TK < qidx, kidx < qidx))

    def body(p, carry):
        two_tiles(SB_PER * (qi - 1 - p) + 1, (None, None))
        return carry

    lax.fori_loop(0, qi, body, 0)
    o_ref[0] = acc_scr[...]


def _sb_prompt(sq, skv, bsz, t):
    assert SB_PER == 2
    nt = t // SB_TK
    nq = t // SB_TQ
    qt = (sq * (1.0 / math.sqrt(HEAD_DIM))).reshape(bsz, t, SB_HEADS, HEAD_DIM).transpose(0, 2, 3, 1).astype(BF16)
    k = skv[:, :SB_WIDTH].reshape(bsz, t, SB_HEADS, HEAD_DIM).transpose(0, 2, 1, 3).astype(BF16)
    vt = skv[:, SB_WIDTH:].reshape(bsz, nt, SB_TK, SB_HEADS, HEAD_DIM).transpose(0, 3, 1, 4, 2).astype(BF16)
    yt = pl.pallas_call(
        _sb_kernel,
        grid=(bsz, nq),
        in_specs=[
            pl.BlockSpec((1, SB_HEADS, HEAD_DIM, SB_TQ), lambda b, i: (b, 0, 0, i)),
            pl.BlockSpec((1, SB_HEADS, t, HEAD_DIM), lambda b, i: (b, 0, 0, 0)),
            pl.BlockSpec((1, SB_HEADS, nt, HEAD_DIM, SB_TK), lambda b, i: (b, 0, 0, 0, 0)),
            pl.BlockSpec((SB_TK, SB_TK), lambda b, i: (0, 0)),
        ],
        out_specs=pl.BlockSpec((1, SB_HEADS, HEAD_DIM, SB_TQ), lambda b, i: (b, 0, 0, i)),
        out_shape=jax.ShapeDtypeStruct((bsz, SB_HEADS, HEAD_DIM, t), F32),
        scratch_shapes=[pltpu.VMEM((SB_HEADS, HEAD_DIM, SB_TQ), F32), pltpu.VMEM((SB_HEADS, 1, SB_TQ), F32)],
        compiler_params=_cparams(("parallel", "arbitrary")),
        name="sb_prompt",
    )(qt, k, vt, _sb_tri())
    return yt.transpose(0, 3, 1, 2).reshape(bsz * t, SB_WIDTH)


NSA_TQ = 128
NSA_TQC = 512
NSA_KC = 1024
MASK_BIG = 2.0 ** 100
CMP_CHUNK_W = CMP_STRIDE * 2 * NSA_KV_WIDTH


def _cmp_weights(w_cmp, pe_cmp):
    r = CMP_LEN // CMP_STRIDE
    wj = w_cmp.reshape(2, r, CMP_STRIDE, HEAD_DIM, HEAD_DIM)
    eye = jnp.eye(2 * NSA_KV_HEADS, dtype=F32).reshape(2, NSA_KV_HEADS, 2, NSA_KV_HEADS)
    wb = jnp.einsum('kjsde,kgmh->jskgdmhe', wj, eye)
    wbig = wb.reshape(r, CMP_CHUNK_W, 2 * NSA_KV_WIDTH).transpose(1, 0, 2).reshape(CMP_CHUNK_W, r * 2 * NSA_KV_WIDTH)
    pj = pe_cmp.reshape(2, r, CMP_STRIDE, HEAD_DIM)
    pe = jnp.broadcast_to(pj.transpose(1, 2, 0, 3)[:, :, :, None, :],
                          (r, CMP_STRIDE, 2, NSA_KV_HEADS, HEAD_DIM)).reshape(r, CMP_CHUNK_W)
    pe8 = jnp.concatenate([pe, jnp.zeros((8 - r, CMP_CHUNK_W), F32)], axis=0)
    return wbig.astype(BF16), pe8.astype(BF16)


def _compress_kernel(x_ref, w_ref, pe_ref, o_ref):
    w = w_ref[...]
    width = 2 * NSA_KV_WIDTH
    a = jnp.dot(x_ref[0].astype(BF16), w, preferred_element_type=F32)
    pb = jnp.dot(pe_ref[...], w, preferred_element_type=F32)
    bias = pb[0:1, :width] + pb[1:2, width:]
    nch = a.shape[0]
    a1 = pltpu.roll(a[:, width:], nch - 1, 0)
    o_ref[0] = a[:, :width] + a1 + bias


def _nsa_compress(xc, wbig, pe8):
    bsz, nch, _ = xc.shape
    return pl.pallas_call(
        _compress_kernel,
        grid=(bsz,),
        in_specs=[pl.BlockSpec((1, nch, CMP_CHUNK_W), lambda b: (b, 0, 0)),
                  pl.BlockSpec(wbig.shape, lambda b: (0, 0)),
                  pl.BlockSpec(pe8.shape, lambda b: (0, 0))],
        out_specs=pl.BlockSpec((1, nch, 2 * NSA_KV_WIDTH), lambda b: (b, 0, 0)),
        out_shape=jax.ShapeDtypeStruct((bsz, nch, 2 * NSA_KV_WIDTH), F32),
        compiler_params=_cparams(("parallel",)),
        name="nsa_compress",
    )(xc, wbig, pe8)


def _split_hi_lo(x):
    hi = x.astype(BF16)
    lo = (x - hi.astype(F32)).astype(BF16)
    return jnp.concatenate([hi, lo], axis=1)


def _topk_unselected(imp, cur, axis=1):
    blk = lax.broadcasted_iota(jnp.int32, imp.shape, axis)
    imp = jnp.where(blk == cur, FORCE, jnp.where(blk < cur, imp, -1.0))
    unsel = jnp.ones(imp.shape, F32)
    for _ in range(SEL_TOPK):
        m = jnp.max(imp, axis=axis, keepdims=True)
        idx = jnp.min(jnp.where(imp == m, blk, LANES), axis=axis, keepdims=True)
        hit = blk == idx
        unsel = jnp.where(hit, 0.0, unsel)
        imp = jnp.where(hit, -3.0e38, imp)
    return jnp.where(blk <= cur, unsel, 1.0)


def _nsa_cmp_kernel(qt_ref, kc_ref, vct_ref, cmap_ref, ocmp_ref, unsel_ref):
    tq = NSA_TQC
    ncp = kc_ref.shape[2]
    q0 = pl.program_id(2) * tq
    qpos = q0 + lax.broadcasted_iota(jnp.int32, (1, tq), 1)
    cidx = lax.broadcasted_iota(jnp.int32, (ncp, tq), 0)
    mc = (cidx * CMP_STRIDE + (CMP_LEN - 1) <= qpos) & (cidx < ncp - 1)
    kc = kc_ref[0, 0]
    vct = vct_ref[0, 0]
    psum = jnp.zeros((ncp, tq), F32)
    outs = []
    for r in range(NSA_REP):
        s = jnp.where(mc, jnp.dot(kc, qt_ref[0, 0, 0, :, r * tq:(r + 1) * tq], preferred_element_type=F32), NEG)
        e = jnp.exp(s - jnp.max(s, axis=0, keepdims=True))
        p = jnp.where(mc, e / jnp.sum(e, axis=0, keepdims=True), 0.0)
        outs.append(jnp.dot(vct, p.astype(BF16), preferred_element_type=F32))
        psum = psum + p
    for i in range(tq // NSA_TQ):
        ocmp_ref[0, 0, i] = jnp.concatenate([o[:, i * NSA_TQ:(i + 1) * NSA_TQ] for o in outs], axis=1)
    ph = psum.astype(BF16)
    pl_ = (psum - ph.astype(F32)).astype(BF16)
    imp = jnp.dot(cmap_ref[...], jnp.concatenate([ph, pl_], axis=0), preferred_element_type=F32)
    unsel = _topk_unselected(imp, qpos // SEL_LEN, axis=0)
    for i in range(tq // NSA_TQ):
        unsel_ref[0, i] = unsel[:, i * NSA_TQ:(i + 1) * NSA_TQ]


def _nsa_cmp(qnt, kc, vct, cmap2t, bsz, t):
    nqt = t // NSA_TQC
    per = NSA_TQC // NSA_TQ
    ncp = kc.shape[2]
    return pl.pallas_call(
        _nsa_cmp_kernel,
        grid=(bsz, NSA_KV_HEADS, nqt),
        in_specs=[
            pl.BlockSpec((1, 1, 1, HEAD_DIM, NSA_REP * NSA_TQC), lambda b, g, i: (b, g, i, 0, 0)),
            pl.BlockSpec((1, 1, ncp, HEAD_DIM), lambda b, g, i: (b, g, 0, 0)),
            pl.BlockSpec((1, 1, HEAD_DIM, ncp), lambda b, g, i: (b, g, 0, 0)),
            pl.BlockSpec(cmap2t.shape, lambda b, g, i: (0, 0)),
        ],
        out_specs=[pl.BlockSpec((1, 1, per, HEAD_DIM, NSA_COLS), lambda b, g, i: (b, g, i, 0, 0)),
                   pl.BlockSpec((1, per, LANES, NSA_TQ), lambda b, g, i: (g, b * nqt + i, 0, 0))],
        out_shape=[jax.ShapeDtypeStruct((bsz, NSA_KV_HEADS, t // NSA_TQ, HEAD_DIM, NSA_COLS), F32),
                   jax.ShapeDtypeStruct((NSA_KV_HEADS, bsz * t // NSA_TQ, LANES, NSA_TQ), F32)],
        compiler_params=_cparams(("parallel", "parallel", "arbitrary")),
        name="nsa_cmp_select",
    )(qnt, kc, vct, cmap2t)


NSA_COLS = NSA_REP * NSA_TQ
NSA_KAUG = HEAD_DIM + LANES


def _nsa_selt_kernel(qt_ref, unsel_ref, ocmp_ref, gate_ref, ksa_ref, vst_ref, vsd_ref, kw_ref, vwt_ref, gexp_ref,
                     y_ref, acc_scr):
    tq = NSA_TQ
    qi = pl.program_id(2)
    qt = qt_ref[0, 0, 0]
    unsel = unsel_ref[0, 0]
    krow = lax.broadcasted_iota(jnp.int32, (tq, tq), 0)
    qcol = lax.broadcasted_iota(jnp.int32, (tq, tq), 1)

    def cols4(x):
        return jnp.concatenate([x] * NSA_REP, axis=1)

    u0 = unsel_ref[0, 0, pl.ds(2 * qi, 1), :]
    u1 = unsel_ref[0, 0, pl.ds(2 * qi + 1, 1), :]
    ud = jnp.where(krow < SEL_LEN, u0, u1)
    bias_d = jnp.where((ud < 0.5) & (krow <= qcol), 0.0, -MASK_BIG)
    kd = ksa_ref[0, 0, pl.ds(pl.multiple_of(qi * tq, tq), tq), 0:HEAD_DIM]
    s = jnp.dot(kd, qt, preferred_element_type=F32) + cols4(bias_d)
    m0 = jnp.max(s, axis=0, keepdims=True)
    p = jnp.exp(s - m0)
    l0 = jnp.sum(p, axis=0, keepdims=True)
    acc_scr[...] = jnp.dot(vsd_ref[0, 0, qi], p.astype(BF16), preferred_element_type=F32)

    unsel_past = jnp.where(krow >= 2 * qi, 1.0, unsel).astype(BF16)
    rhs = jnp.concatenate([qt, cols4(unsel_past)], axis=0)

    def chunk(c, carry):
        m_old, l_old = carry
        ka = ksa_ref[0, 0, pl.ds(pl.multiple_of(c * NSA_KC, NSA_KC), NSA_KC), :]
        s = jnp.dot(ka, rhs, preferred_element_type=F32)
        m_new = jnp.maximum(m_old, jnp.max(s, axis=0, keepdims=True))
        alpha = jnp.exp(m_old - m_new)
        p = jnp.exp(s - m_new)
        acc_scr[...] = acc_scr[...] * alpha + jnp.dot(vst_ref[0, 0, c], p.astype(BF16),
                                                      preferred_element_type=F32)
        return m_new, l_old * alpha + jnp.sum(p, axis=0, keepdims=True)

    _, l_sel = lax.fori_loop(0, (qi * tq + NSA_KC - 1) // NSA_KC, chunk, (m0, l0))
    o_sel_t = acc_scr[...] / l_sel

    ntw = (WINDOW + tq) // tq
    t0 = jnp.maximum(qi - WINDOW // tq, 0)
    kw = kw_ref[0, 0, pl.ds(pl.multiple_of(t0 * tq, tq), ntw * tq), :]
    kpos = t0 * tq + lax.broadcasted_iota(jnp.int32, (ntw * tq, tq), 0)
    qpw = qi * tq + lax.broadcasted_iota(jnp.int32, (ntw * tq, tq), 1)
    bias_w = jnp.where((kpos <= qpw) & (kpos > qpw - WINDOW), 0.0, NEG)
    sw = jnp.dot(kw, qt, preferred_element_type=F32) + cols4(bias_w)
    pw = jnp.exp(sw - jnp.max(sw, axis=0, keepdims=True))
    lw = jnp.sum(pw, axis=0, keepdims=True)
    pwb = pw.astype(BF16)
    o_win_t = jnp.zeros((HEAD_DIM, NSA_COLS), F32)
    for i in range(ntw):
        o_win_t = o_win_t + jnp.dot(vwt_ref[0, 0, t0 + i], pwb[i * tq:(i + 1) * tq], preferred_element_type=F32)
    o_win_t = o_win_t / lw

    o_cmp_t = ocmp_ref[0, 0, 0]

    def cols(x, r):
        return x[:, r * tq:(r + 1) * tq]

    both = [jnp.concatenate([cols(o_sel_t, r), cols(o_win_t, r)], axis=0).T for r in range(NSA_REP)]
    cmp2 = [jnp.concatenate([cols(o_cmp_t, r), cols(o_cmp_t, r + 1)], axis=0).T for r in range(0, NSA_REP, 2)]
    o_sel = jnp.concatenate([x[:, :HEAD_DIM] for x in both], axis=1)
    o_win = jnp.concatenate([x[:, HEAD_DIM:] for x in both], axis=1)
    o_cmp = jnp.concatenate(cmp2, axis=1)
    ge = jnp.dot(_split_hi_lo(gate_ref[...]), gexp_ref[0], preferred_element_type=F32)
    w = NSA_REP * HEAD_DIM
    y_ref[...] = ge[:, 0:w] * o_cmp + ge[:, w:2 * w] * o_sel + ge[:, 2 * w:3 * w] * o_win


def _nsa_selt(qt, unsel_t, ocmp, gates, ksa, vst, vsd, kw, vwt, gexp, bsz, t):
    nq = t // NSA_TQ
    w = NSA_REP * HEAD_DIM
    full = lambda a: pl.BlockSpec((1, 1) + a.shape[2:], lambda b, g, i: (b, g) + (0,) * (a.ndim - 2))
    return pl.pallas_call(
        _nsa_selt_kernel,
        grid=(bsz, NSA_KV_HEADS, nq),
        in_specs=[
            pl.BlockSpec((1, 1, 1, HEAD_DIM, NSA_COLS), lambda b, g, i: (b, g, i, 0, 0)),
            pl.BlockSpec((1, 1, LANES, NSA_TQ), lambda b, g, i: (g, b * nq + i, 0, 0)),
            pl.BlockSpec((1, 1, 1, HEAD_DIM, NSA_COLS), lambda b, g, i: (b, g, i, 0, 0)),
            pl.BlockSpec((NSA_TQ, LANES), lambda b, g, i: (b * nq + i, 0)),
            full(ksa), full(vst), full(vsd), full(kw), full(vwt),
            pl.BlockSpec((1,) + gexp.shape[1:], lambda b, g, i: (g, 0, 0)),
        ],
        out_specs=pl.BlockSpec((NSA_TQ, w), lambda b, g, i: (b * nq + i, g)),
        out_shape=jax.ShapeDtypeStruct((bsz * t, NSA_WIDTH), F32),
        scratch_shapes=[pltpu.VMEM((HEAD_DIM, NSA_COLS), F32)],
        compiler_params=_cparams(("parallel", "parallel", "arbitrary")),
        name="nsa_sel_win",
    )(qt, unsel_t, ocmp, gates, ksa, vst, vsd, kw, vwt, gexp)


def _nsa_consts(t):
    ncp = t // CMP_STRIDE
    ns = t // SEL_LEN
    cs = np.arange(ncp) * CMP_STRIDE
    ss = np.arange(LANES) * SEL_LEN
    ov = np.clip(np.minimum(cs[:, None] + CMP_LEN, ss[None, :] + SEL_LEN) - np.maximum(cs[:, None], ss[None, :]), 0, None)
    cmap = (ov / CMP_LEN).astype(np.float32)
    cmap[ncp - 1:, :] = 0.0
    cmap[:, ns:] = 0.0
    cmap2 = jnp.asarray(np.concatenate([cmap, cmap], axis=0), dtype=BF16)
    key_blk = (np.arange(t) // SEL_LEN)[:, None]
    bige = jnp.asarray(np.where(np.arange(LANES)[None, :] == key_blk, -MASK_BIG, 0.0), dtype=BF16)
    ge = np.zeros((NSA_KV_HEADS, LANES, 3 * NSA_REP * HEAD_DIM), np.float32)
    for g in range(NSA_KV_HEADS):
        for r in range(NSA_REP):
            for j in range(3):
                c0 = j * NSA_REP * HEAD_DIM + r * HEAD_DIM
                ge[g, (g * NSA_REP + r) * 3 + j, c0:c0 + HEAD_DIM] = 1.0
    gexp = jnp.asarray(np.concatenate([ge, ge], axis=1), dtype=BF16)
    return cmap2, bige, gexp, cmap2.T


def _nsa_prompt(qn, qr, nsa, win, gates, wbig, pe8, consts, bsz, t):
    _, bige, gexp, cmap2t = consts
    ncp = t // CMP_STRIDE
    xc = nsa[:, :2 * NSA_KV_WIDTH].reshape(bsz, ncp, CMP_CHUNK_W)
    kcv = _nsa_compress(xc, wbig, pe8).reshape(bsz, ncp, 2, NSA_KV_HEADS, HEAD_DIM)
    kc = kcv[:, :, 0].transpose(0, 2, 1, 3).astype(BF16)
    vct = kcv[:, :, 1].transpose(0, 2, 3, 1).astype(BF16)
    nqt = t // NSA_TQC
    qnt = qn.reshape(bsz, nqt, NSA_TQC, NSA_KV_HEADS, NSA_REP, HEAD_DIM).transpose(0, 3, 1, 5, 4, 2)
    qnt = qnt.reshape(bsz, NSA_KV_HEADS, nqt, HEAD_DIM, NSA_REP * NSA_TQC).astype(BF16)
    ocmp, unsel_t = _nsa_cmp(qnt, kc, vct, cmap2t, bsz, t)
    nq = t // NSA_TQ
    qt = qr.reshape(bsz, nq, NSA_TQ, NSA_KV_HEADS, NSA_REP, HEAD_DIM).transpose(0, 3, 1, 5, 4, 2)
    qt = qt.reshape(bsz, NSA_KV_HEADS, nq, HEAD_DIM, NSA_COLS).astype(BF16)

    def rows_major(x):
        return x.reshape(bsz, t, NSA_KV_HEADS, HEAD_DIM).transpose(0, 2, 1, 3).astype(BF16)

    def tiles_t(x, tk):
        return x.reshape(bsz, t // tk, tk, NSA_KV_HEADS, HEAD_DIM).transpose(0, 3, 1, 4, 2).astype(BF16)

    ksa = jnp.concatenate([rows_major(nsa[:, 256:384]),
                           jnp.broadcast_to(bige, (bsz, NSA_KV_HEADS, t, LANES))], axis=-1)
    vst = tiles_t(nsa[:, 384:512], NSA_KC)
    vsd = tiles_t(nsa[:, 384:512], NSA_TQ)
    kw = rows_major(win[:, :128])
    vwt = tiles_t(win[:, 128:], NSA_TQ)
    return _nsa_selt(qt, unsel_t, ocmp, gates, ksa, vst, vsd, kw, vwt, gexp, bsz, t)


PAGE = 128
N_PAGES = PAST_LEN // PAGE
HROWS = 8
NT_DIMS = (((1,), (1,)), ((), ()))


def _sb_tri_rows():
    s = np.arange(PAGE)[:, None]
    j = np.arange(2 * PAGE)[None, :]
    u = -((s > j) | (j >= PAGE)).astype(np.float32)
    return jnp.asarray(np.concatenate([u, u], axis=0), dtype=BF16)


def _pages_t(cache):
    dp, npool = cache.shape[:2]
    return cache.transpose(0, 1, 3, 4, 5, 2).reshape(dp * npool, -1, PAGE)


def _page_specs(feat, blk):
    return [pl.BlockSpec((1, feat, PAGE), lambda b, pt, p=p: (pt[b, p], blk, 0)) for p in range(N_PAGES)]


def _dec_sb_kernel(pt_ref, q_ref, *rest):
    pages = rest[:N_PAGES]
    tri_ref, dmask_ref, y_ref = rest[N_PAGES:]
    q8 = q_ref[0]
    z = jnp.concatenate(
        [jnp.dot(q8, pg[0, 0:SB_WIDTH, :].astype(BF16), preferred_element_type=F32) for pg in pages],
        axis=0)
    sp = jnp.maximum(z, 0.0) + jnp.log(1.0 + jnp.exp(-jnp.abs(z)))
    rt = jnp.dot(_split_hi_lo(sp), tri_ref[...], preferred_element_type=F32)
    cs = [None] * N_PAGES
    c = jnp.zeros((HROWS, PAGE), F32)
    for p in range(N_PAGES - 1, -1, -1):
        cs[p] = c
        c = c + rt[p * HROWS:(p + 1) * HROWS, PAGE:]
    w = jnp.exp(z - sp + rt[:, :PAGE] + jnp.concatenate(cs, axis=0)).astype(BF16)
    y8 = jnp.zeros((HROWS, SB_WIDTH), F32)
    for p, pg in enumerate(pages):
        y8 = y8 + lax.dot_general(w[p * HROWS:(p + 1) * HROWS], pg[0, SB_WIDTH:, :].astype(BF16), NT_DIMS,
                                  preferred_element_type=F32)
    y_ref[0] = jnp.sum(y8 * dmask_ref[...], axis=0, keepdims=True)


def _dec_sb(rows, sq, cache2d):
    bd = sq.shape[0]
    hm = np.zeros((HROWS, SB_WIDTH), np.float32)
    for h in range(SB_HEADS):
        hm[h, h * HEAD_DIM:(h + 1) * HEAD_DIM] = 1.0
    q8 = ((sq * (1.0 / math.sqrt(HEAD_DIM)))[:, None, :] * hm[None]).astype(BF16)
    grid_spec = pltpu.PrefetchScalarGridSpec(
        num_scalar_prefetch=1, grid=(bd,),
        in_specs=[pl.BlockSpec((1, HROWS, SB_WIDTH), lambda b, pt: (b, 0, 0))] + _page_specs(2 * SB_WIDTH, 0)
        + [pl.BlockSpec((2 * PAGE, 2 * PAGE), lambda b, pt: (0, 0)),
           pl.BlockSpec((HROWS, SB_WIDTH), lambda b, pt: (0, 0))],
        out_specs=pl.BlockSpec((1, 1, SB_WIDTH), lambda b, pt: (b, 0, 0)))
    y = pl.pallas_call(
        _dec_sb_kernel, grid_spec=grid_spec,
        out_shape=jax.ShapeDtypeStruct((bd, 1, SB_WIDTH), F32),
        compiler_params=_cparams(("arbitrary",)),
        name="dec_sb",
    )(rows, q8, *([cache2d] * N_PAGES), _sb_tri_rows(), jnp.asarray(hm))
    return y.reshape(bd, SB_WIDTH)


def _dec_cmp_kernel(pt_ref, q_ref, *rest):
    pages = rest[:N_PAGES]
    w_ref, pe_ref, cmap_ref, perm_ref, o_ref, imp_ref = rest[N_PAGES:]
    width = 2 * NSA_KV_WIDTH
    cpp = PAGE // CMP_STRIDE
    ncp = N_PAGES * cpp
    perm = perm_ref[...]
    pp = [lax.dot_general(perm, pg[0].astype(BF16), NT_DIMS, preferred_element_type=F32) for pg in pages]
    a = jnp.zeros((ncp, 2 * width), F32)
    for s in range(CMP_STRIDE):
        xs = jnp.concatenate([x[s * cpp:(s + 1) * cpp] for x in pp], axis=0)
        a = a + jnp.dot(xs.astype(BF16), w_ref[s * width:(s + 1) * width, :], preferred_element_type=F32)
    pb = jnp.dot(pe_ref[...], w_ref[...], preferred_element_type=F32)
    bias = pb[0:1, :width] + pb[1:2, width:]
    kcv = a[:, :width] + pltpu.roll(a[:, width:], ncp - 1, 0) + bias
    q8 = q_ref[0]
    s8 = lax.dot_general(q8, kcv[:, :NSA_KV_WIDTH].astype(BF16), NT_DIMS, preferred_element_type=F32)
    valid = lax.broadcasted_iota(jnp.int32, s8.shape, 1) < ncp - 1
    s8 = jnp.where(valid, s8, NEG)
    e = jnp.exp(s8 - jnp.max(s8, axis=-1, keepdims=True))
    p = jnp.where(valid, e / jnp.sum(e, axis=-1, keepdims=True), 0.0)
    o_ref[0] = jnp.dot(p.astype(BF16), kcv[:, NSA_KV_WIDTH:].astype(BF16), preferred_element_type=F32)
    row = lax.broadcasted_iota(jnp.int32, p.shape, 0)
    psum = jnp.where(row < NSA_REP, jnp.sum(p[:NSA_REP], axis=0, keepdims=True),
                     jnp.sum(p[NSA_REP:], axis=0, keepdims=True))
    imp_ref[0] = jnp.dot(_split_hi_lo(psum), cmap_ref[...], preferred_element_type=F32)


def _head_rows(x, bd):
    x = x.reshape(bd, NSA_KV_HEADS, NSA_REP, 1, HEAD_DIM)
    eye = jnp.eye(NSA_KV_HEADS, dtype=x.dtype).reshape(1, NSA_KV_HEADS, 1, NSA_KV_HEADS, 1)
    return (x * eye).reshape(bd, NSA_HEADS, NSA_KV_WIDTH)


def _dec_cmp(rows, qn, cache2d, wbig, pe8, cmap2):
    bd = qn.shape[0]
    q8 = _head_rows(qn, bd).astype(BF16)
    cpp = PAGE // CMP_STRIDE
    pm = np.zeros((PAGE, PAGE), np.float32)
    for s in range(CMP_STRIDE):
        for n in range(cpp):
            pm[s * cpp + n, CMP_STRIDE * n + s] = 1.0
    perm = jnp.asarray(pm, dtype=BF16)
    grid_spec = pltpu.PrefetchScalarGridSpec(
        num_scalar_prefetch=1, grid=(bd,),
        in_specs=[pl.BlockSpec((1, NSA_HEADS, NSA_KV_WIDTH), lambda b, pt: (b, 0, 0))]
        + _page_specs(2 * NSA_KV_WIDTH, 0)
        + [pl.BlockSpec(wbig.shape, lambda b, pt: (0, 0)), pl.BlockSpec(pe8.shape, lambda b, pt: (0, 0)),
           pl.BlockSpec(cmap2.shape, lambda b, pt: (0, 0)), pl.BlockSpec((PAGE, PAGE), lambda b, pt: (0, 0))],
        out_specs=[pl.BlockSpec((1, NSA_HEADS, NSA_KV_WIDTH), lambda b, pt: (b, 0, 0)),
                   pl.BlockSpec((1, NSA_HEADS, LANES), lambda b, pt: (b, 0, 0))])
    return pl.pallas_call(
        _dec_cmp_kernel, grid_spec=grid_spec,
        out_shape=[jax.ShapeDtypeStruct((bd, NSA_HEADS, NSA_KV_WIDTH), F32),
                   jax.ShapeDtypeStruct((bd, NSA_HEADS, LANES), F32)],
        compiler_params=_cparams(("arbitrary",)),
        name="dec_cmp",
    )(rows, q8, *([cache2d] * N_PAGES), wbig, pe8, cmap2, perm)


def _dec_topk_kernel(imp_ref, unsel_ref):
    unsel_ref[...] = _topk_unselected(imp_ref[...], PAST_LEN // SEL_LEN)


def _dec_topk(imp):
    return pl.pallas_call(
        _dec_topk_kernel,
        out_shape=jax.ShapeDtypeStruct(imp.shape, F32),
        name="dec_topk",
    )(imp)


def _dec_sel_kernel(pt_ref, q_ref, unsel_ref, ocmp_ref, gate_ref, new_ref, win_ref, nwin_ref, *rest):
    pages = rest[:N_PAGES]
    bige_ref, y_ref = rest[N_PAGES:]
    q8 = q_ref[0]
    q8f = q8.astype(F32)

    def new_key_score(krow):
        return jnp.sum(q8f * krow.astype(BF16).astype(F32), axis=-1, keepdims=True)

    def attend(s3, s_new, vals_t, v_new):
        m = jnp.maximum(jnp.max(jnp.max(s3, axis=0), axis=-1, keepdims=True), s_new)
        p3 = jnp.exp(s3 - m[None])
        p_new = jnp.exp(s_new - m)
        den = jnp.sum(jnp.sum(p3, axis=0), axis=-1, keepdims=True) + p_new
        o = p_new * v_new.astype(BF16).astype(F32)
        for i, vt in enumerate(vals_t):
            o = o + lax.dot_general(p3[i].astype(BF16), vt, NT_DIMS, preferred_element_type=F32)
        return o / den

    bias = jnp.dot(unsel_ref[0].astype(BF16), bige_ref[...], preferred_element_type=F32)
    s_sel = jnp.stack(
        [jnp.dot(q8, pg[0, 0:NSA_KV_WIDTH, :].astype(BF16), preferred_element_type=F32)
         + bias[:, p * PAGE:(p + 1) * PAGE] for p, pg in enumerate(pages)], axis=0)
    new = new_ref[0]
    o_sel = attend(s_sel, new_key_score(new[:, 2 * NSA_KV_WIDTH:3 * NSA_KV_WIDTH]),
                   [pg[0, NSA_KV_WIDTH:, :].astype(BF16) for pg in pages], new[:, 3 * NSA_KV_WIDTH:])

    nt = WINDOW // PAGE
    kpos0 = lax.broadcasted_iota(jnp.int32, (HROWS, PAGE), 1) == 0
    s_win = []
    for i in range(nt):
        s = jnp.dot(q8, win_ref[0, 0:NSA_KV_WIDTH, i * PAGE:(i + 1) * PAGE].astype(BF16),
                    preferred_element_type=F32)
        s_win.append(jnp.where(kpos0, NEG, s) if i == 0 else s)
    nwin = nwin_ref[0]
    o_win = attend(jnp.stack(s_win, axis=0), new_key_score(nwin[:, :NSA_KV_WIDTH]),
                   [win_ref[0, NSA_KV_WIDTH:, i * PAGE:(i + 1) * PAGE].astype(BF16) for i in range(nt)],
                   nwin[:, NSA_KV_WIDTH:])

    g = gate_ref[0]
    y8 = g[:, 0:LANES] * ocmp_ref[0] + g[:, LANES:2 * LANES] * o_sel + g[:, 2 * LANES:] * o_win
    for h in range(NSA_HEADS):
        c0 = (h // NSA_REP) * HEAD_DIM
        y_ref[0, :, h * HEAD_DIM:(h + 1) * HEAD_DIM] = y8[h:h + 1, c0:c0 + HEAD_DIM]


def _dec_sel(rows, qr, unsel, ocmp, gates, new_nsa, win_state, new_win, cache2d, bige):
    bd = qr.shape[0]
    q8 = _head_rows(qr, bd).astype(BF16)
    g3 = jnp.broadcast_to(gates[:, :3 * NSA_HEADS].reshape(bd, NSA_HEADS, 3, 1),
                          (bd, NSA_HEADS, 3, LANES)).reshape(bd, NSA_HEADS, 3 * LANES)
    per_b = lambda *shape: pl.BlockSpec((1,) + shape, lambda b, pt: (b,) + (0,) * len(shape))
    grid_spec = pltpu.PrefetchScalarGridSpec(
        num_scalar_prefetch=1, grid=(bd,),
        in_specs=[per_b(NSA_HEADS, NSA_KV_WIDTH), per_b(NSA_HEADS, LANES), per_b(NSA_HEADS, NSA_KV_WIDTH),
                  per_b(NSA_HEADS, 3 * LANES), per_b(1, 4 * NSA_KV_WIDTH), per_b(2 * NSA_KV_WIDTH, WINDOW),
                  per_b(1, 2 * NSA_KV_WIDTH)]
        + _page_specs(2 * NSA_KV_WIDTH, 1)
        + [pl.BlockSpec(bige.shape, lambda b, pt: (0, 0))],
        out_specs=per_b(1, NSA_WIDTH))
    y = pl.pallas_call(
        _dec_sel_kernel, grid_spec=grid_spec,
        out_shape=jax.ShapeDtypeStruct((bd, 1, NSA_WIDTH), F32),
        compiler_params=_cparams(("arbitrary",)),
        name="dec_sel_win",
    )(rows, q8, unsel, ocmp, g3, new_nsa.reshape(bd, 1, -1), win_state, new_win.reshape(bd, 1, -1),
      *([cache2d] * N_PAGES), bige)
    return y.reshape(bd, NSA_WIDTH)


def _dec_consts():
    cmap2 = _nsa_consts(PAST_LEN)[0]
    blk = np.arange(PAST_LEN) // SEL_LEN
    bige = jnp.asarray(np.where(np.arange(LANES)[:, None] == blk[None, :], -MASK_BIG, 0.0), dtype=BF16)
    return cmap2, bige


def _cplx_affine(e1, e2):
    a1r, a1i, b1r, b1i = e1
    a2r, a2i, b2r, b2i = e2
    return (a2r * a1r - a2i * a1i, a2r * a1i + a2i * a1r,
            a2r * b1r - a2i * b1i + b2r, a2r * b1i + a2i * b1r + b2i)


def _s5_branch(u, s0_re, s0_im, a_re, a_im, log_dt, b_re, b_im, c_re, c_im, d):
    bsz, t = u.shape[:2]
    dt = jnp.exp(log_dt)[:, None]
    mag = jnp.exp(a_re * dt)
    lb_re, lb_im = mag * jnp.cos(a_im * dt), mag * jnp.sin(a_im * dt)
    den = a_re * a_re + a_im * a_im
    co_re = ((lb_re - 1.0) * a_re + lb_im * a_im) / den
    co_im = (lb_im * a_re - (lb_re - 1.0) * a_im) / den
    bb_re = co_re[..., None] * b_re - co_im[..., None] * b_im
    bb_im = co_re[..., None] * b_im + co_im[..., None] * b_re
    ug = u.reshape(bsz, t, SSM_GROUPS, SSM_GROUP)
    bu_re = jnp.einsum('btgp,gnp->btgn', ug, bb_re)
    bu_im = jnp.einsum('btgp,gnp->btgn', ug, bb_im)
    bu_re = bu_re.at[:, 0].add(lb_re * s0_re - lb_im * s0_im)
    bu_im = bu_im.at[:, 0].add(lb_re * s0_im + lb_im * s0_re)
    la_re = jnp.broadcast_to(lb_re, bu_re.shape)
    la_im = jnp.broadcast_to(lb_im, bu_im.shape)
    _, _, s_re, s_im = lax.associative_scan(_cplx_affine, (la_re, la_im, bu_re, bu_im), axis=1)
    y = (jnp.einsum('btgn,gpn->btgp', s_re, c_re) - jnp.einsum('btgn,gpn->btgp', s_im, c_im) + d * ug)
    return y.reshape(bsz, t, SSM_WIDTH), s_re[:, -1], s_im[:, -1]


def _sb_core(q, k, v, qpos, kpos):
    z = jnp.einsum('bqhd,bshd->bhqs', q, k) / math.sqrt(HEAD_DIM)
    mask = kpos[None, :] < qpos[:, None]
    log_1m = jnp.where(mask, jax.nn.log_sigmoid(-z), 0.0)
    between = lax.cumsum(log_1m, axis=3, reverse=True) - log_1m
    w = jnp.where(mask, jnp.exp(jax.nn.log_sigmoid(z) + between), 0.0)
    return jnp.einsum('bhqs,bshd->bqhd', w, v)


def _compress(rows, pe, w):
    bsz, length = rows.shape[:2]
    r = CMP_LEN // CMP_STRIDE
    n_chunks = length // CMP_STRIDE
    nc = n_chunks - r + 1
    chunks = rows[:, :n_chunks * CMP_STRIDE].reshape(bsz, n_chunks, CMP_STRIDE, NSA_KV_HEADS, HEAD_DIM)
    wj = w.reshape(r, CMP_STRIDE, HEAD_DIM, HEAD_DIM)
    pj = pe.reshape(r, CMP_STRIDE, HEAD_DIM)
    out = 0.0
    for j in range(r):
        out = out + jnp.einsum('bnsgd,sde->bnge', chunks[:, j:j + nc], wj[j]) + jnp.einsum('sd,sde->e', pj[j], wj[j])
    cend = jnp.arange(nc) * CMP_STRIDE + CMP_LEN - 1
    return out, cend


def _cmp_to_sel(nc, ns):
    cs = np.arange(nc) * CMP_STRIDE
    ss = np.arange(ns) * SEL_LEN
    ov = np.clip(np.minimum(cs[:, None] + CMP_LEN, ss[None, :] + SEL_LEN) - np.maximum(cs[:, None], ss[None, :]), 0, None)
    return jnp.asarray(ov / CMP_LEN, dtype=F32)


def _pad_blocks(a):
    pad = (-a.shape[1]) % SEL_LEN
    return jnp.pad(a, ((0, 0), (0, pad), (0, 0), (0, 0)))


def _nsa_prepare(k_cmp, v_cmp, k_sel, v_sel, w_cmp, pe_cmp):
    kc, cend = _compress(k_cmp, pe_cmp[0], w_cmp[0])
    vc, _ = _compress(v_cmp, pe_cmp[1], w_cmp[1])
    ks, vs = _pad_blocks(k_sel), _pad_blocks(v_sel)
    cmap = _cmp_to_sel(kc.shape[1], ks.shape[1] // SEL_LEN)
    return kc, vc, cend, cmap, ks, vs


def _nsa_core(qr, qn, g, qpos, kc, vc, cend, cmap, ks, vs, kw, vw, kwpos):
    bsz, nq = qr.shape[:2]
    qr = qr.reshape(bsz, nq, NSA_KV_HEADS, NSA_REP, HEAD_DIM)
    qn = qn.reshape(bsz, nq, NSA_KV_HEADS, NSA_REP, HEAD_DIM)
    mc = cend[None, :] <= qpos[:, None]
    sc = jnp.einsum('bqgrd,bcgd->bgrqc', qn, kc)
    pc = jax.nn.softmax(jnp.where(mc, sc, NEG), axis=-1) * mc
    o_cmp = jnp.einsum('bgrqc,bcgd->bqgrd', pc, vc)
    ns = cmap.shape[1]
    topk = min(SEL_TOPK, ns)
    imp = jnp.einsum('bgrqc,cn->bgqn', pc, cmap)
    blk = jnp.arange(ns)[None, :]
    cur = (qpos // SEL_LEN)[:, None]
    imp = jnp.where(blk == cur, FORCE, jnp.where(blk < cur, imp, -1.0))
    _, idx = lax.top_k(imp, topk)
    ksb = ks.reshape(bsz, ns, SEL_LEN, NSA_KV_HEADS, HEAD_DIM).transpose(0, 3, 1, 2, 4)
    vsb = vs.reshape(bsz, ns, SEL_LEN, NSA_KV_HEADS, HEAD_DIM).transpose(0, 3, 1, 2, 4)
    bi = jnp.arange(bsz)[:, None, None, None]
    gi = jnp.arange(NSA_KV_HEADS)[None, :, None, None]
    kg, vg = ksb[bi, gi, idx], vsb[bi, gi, idx]
    spos = idx[..., None] * SEL_LEN + jnp.arange(SEL_LEN)
    ms = spos <= qpos[None, None, :, None, None]
    ss = jnp.einsum('bqgrd,bgqkld->bgrqkl', qr, kg)
    ss = jnp.where(ms[:, :, None], ss, NEG).reshape(bsz, NSA_KV_HEADS, NSA_REP, nq, topk * SEL_LEN)
    ps = jax.nn.softmax(ss, axis=-1).reshape(bsz, NSA_KV_HEADS, NSA_REP, nq, topk, SEL_LEN)
    o_sel = jnp.einsum('bgrqkl,bgqkld->bqgrd', ps, vg)
    mw = (kwpos[None, :] <= qpos[:, None]) & (kwpos[None, :] > qpos[:, None] - WINDOW) & (kwpos[None, :] >= 0)
    sw = jnp.einsum('bqgrd,bwgd->bgrqw', qr, kw)
    pw = jax.nn.softmax(jnp.where(mw, sw, NEG), axis=-1)
    o_win = jnp.einsum('bgrqw,bwgd->bqgrd', pw, vw)
    g = g.reshape(bsz, nq, NSA_KV_HEADS, NSA_REP, 3)
    o = g[..., 0:1] * o_cmp + g[..., 1:2] * o_sel + g[..., 2:3] * o_win
    return o.reshape(bsz, nq, NSA_WIDTH)


def _rope_tables(pos):
    half = HEAD_DIM // 2
    freqs = ROPE_THETA ** (-jnp.arange(half, dtype=F32) / half)
    ang = pos.astype(F32)[:, None] * freqs
    cos, sin = jnp.cos(ang), jnp.sin(ang)
    cos2 = jnp.concatenate([cos, cos], axis=-1)
    sin2 = jnp.concatenate([-sin, sin], axis=-1)
    return jnp.tile(cos2, (1, LANES // HEAD_DIM)), jnp.tile(sin2, (1, LANES // HEAD_DIM))


def _pack_w_in(w):
    pad = jnp.zeros((D_MODEL, LANES - 3 * NSA_HEADS), w.dtype)
    return jnp.concatenate([w[:, :2304], w[:, 2304:2328], pad, w[:, 2328:]], axis=1).astype(BF16)


def kernel(x_prompt, x_sample, cache_sb, cache_nsa, state_win, state_ssm, page_table, c_prompt, c_sample,
           w_ada, b_ada, norm_mix, norm_ffn, w_in, ssm_a_re, ssm_a_im, ssm_log_dt, ssm_b_re, ssm_b_im,
           ssm_c_re, ssm_c_im, ssm_d, w_glu, b_glu, nsa_w_cmp, nsa_pe_cmp, w_branch, w_out,
           w_grp, b_grp, w_rt, b_rt, w_e_gate, w_e_up, w_e_down, final_norm):
    bp, t = x_prompt.shape[:2]
    bd, s = x_sample.shape[:2]
    np_tok = bp * t
    win_len = state_win.shape[2]
    tm_p = 256
    tiles_pb = t // tm_p
    tm_f = 1024
    tiles_fb = t // tm_f

    xp = x_prompt.reshape(np_tok, D_MODEL)
    xs = x_sample.reshape(bd * s, D_MODEL)
    cos_p, sin_p = _rope_tables(jnp.arange(t))
    cos_s, sin_s = _rope_tables(jnp.full((bd,), PAST_LEN))
    pos_s = PAST_LEN + jnp.arange(s)
    nsa_consts = _nsa_consts(t)
    dec_cmap2, dec_bige = _dec_consts()
    n_pool = cache_sb.shape[1]
    cache_sb2d = _pages_t(cache_sb)
    cache_nsa2d = _pages_t(cache_nsa)
    win_t = state_win.transpose(0, 1, 3, 4, 5, 2).reshape(DEPTH, bd, 2 * NSA_KV_WIDTH, win_len)

    sb_p, sb_s, nsa_p, nsa_s, win_p, win_s, ssm_p, ssm_s = [], [], [], [], [], [], [], []
    for l in range(DEPTH):
        w_in_l = _pack_w_in(w_in[l])
        wb_l = w_branch[l].astype(BF16)
        wo_l = w_out[l].astype(BF16)
        wr_l = jnp.concatenate([w_grp[l], w_rt[l], jnp.zeros((D_MODEL, LANES - 20), F32)], axis=1)
        br_l = jnp.concatenate([b_grp[l], b_rt[l], jnp.zeros((LANES - 20,), F32)])[None, :]
        def by_group(w):
            w = w.astype(BF16).reshape(MOE_GROUPS, EXPERTS_PER_GROUP, D_MODEL, EXPERT_FF)
            return w.transpose(0, 2, 1, 3).reshape(MOE_GROUPS, D_MODEL, EXPERTS_PER_GROUP * EXPERT_FF)

        wg_l, wu_l = by_group(w_e_gate[l]), by_group(w_e_up[l])
        wd_l = w_e_down[l].astype(BF16).reshape(MOE_GROUPS, EXPERTS_PER_GROUP * EXPERT_FF, D_MODEL)
        gn_mix = norm_mix[l][None, :]
        gn_ffn = norm_ffn[l][None, :]
        ssm_l = (ssm_a_re[l], ssm_a_im[l], ssm_log_dt[l], ssm_b_re[l], ssm_b_im[l], ssm_c_re[l], ssm_c_im[l],
                 ssm_d[l])
        wglu_l = w_glu[l].astype(BF16)
        bglu_l = b_glu[l][None, :]

        mp = jnp.split(jax.nn.silu(c_prompt) @ w_ada[l] + b_ada[l], 6, axis=-1)
        mp = [m[:, None, :] for m in mp]
        u, sq, skv, qn, qr, nsa, win, ng, g = _inproj(
            xp, mp[0], mp[1], gn_mix, cos_p, sin_p, w_in_l, tm_p,
            lambda i: i // tiles_pb, lambda i: i % tiles_pb)
        y_a, sf = _s5_prompt(u.reshape(bp, t, SSM_WIDTH), _s5_mats(*ssm_l[:7]), ssm_d[l])
        y_b = _sb_prompt(sq, skv, bp, t)
        wbig, pe8 = _cmp_weights(nsa_w_cmp[l], nsa_pe_cmp[l])
        y_c = _nsa_prompt(qn, qr, nsa, win, ng, wbig, pe8, nsa_consts, bp, t)
        xp = _merge(xp, mp[2], y_a.reshape(np_tok, SSM_WIDTH), y_b, y_c, g, wglu_l, bglu_l, wb_l, wo_l,
                    tm_p, lambda i: i // tiles_pb)
        xp = _ffn(xp, mp[3], mp[4], mp[5], gn_ffn, wr_l, br_l, wg_l, wu_l, wd_l, tm_f, lambda i: i // tiles_fb)
        sb_p.append(skv.reshape(bp, t, 2, SB_HEADS, HEAD_DIM))
        nsa_p.append(nsa.reshape(bp, t, 4, NSA_KV_HEADS, HEAD_DIM))
        win_p.append(win.reshape(bp, t, 2, NSA_KV_HEADS, HEAD_DIM)[:, t - min(WINDOW, t):])
        ssm_p.append(sf)

        ms = jnp.split(jax.nn.silu(c_sample) @ w_ada[l] + b_ada[l], 6, axis=-1)
        ms = [m[None, :, :] for m in ms]
        u, sq, skv, qn, qr, nsa, win, ng, g = _inproj(
            xs, ms[0], ms[1], gn_mix, cos_s, sin_s, w_in_l, bd, lambda i: 0, lambda i: 0)
        y_a, sr, si = _s5_branch(u.reshape(bd, s, SSM_WIDTH), state_ssm[l][:, 0], state_ssm[l][:, 1], *ssm_l)
        new_sb = skv.reshape(bd, s, 2, SB_HEADS, HEAD_DIM)
        rows_l = page_table + l * n_pool
        y_b = _dec_sb(rows_l, sq, cache_sb2d)
        new_nsa = nsa.reshape(bd, s, 4, NSA_KV_HEADS, HEAD_DIM)
        ocmp, imp = _dec_cmp(rows_l, qn, cache_nsa2d, wbig, pe8, dec_cmap2)
        unsel = _dec_topk(imp.reshape(bd * NSA_HEADS, LANES)).reshape(bd, NSA_HEADS, LANES)
        new_win = win.reshape(bd, s, 2, NSA_KV_HEADS, HEAD_DIM)
        winc = jnp.concatenate([state_win[l], new_win], axis=1)
        y_c = _dec_sel(rows_l, qr, unsel, ocmp, ng, nsa, win_t[l], win, cache_nsa2d, dec_bige)
        xs = _merge(xs, ms[2], y_a.reshape(bd * s, SSM_WIDTH), y_b, y_c, g, wglu_l, bglu_l, wb_l, wo_l,
                    bd, lambda i: 0)
        xs = _ffn(xs, ms[3], ms[4], ms[5], gn_ffn, wr_l, br_l, wg_l, wu_l, wd_l, bd, lambda i: 0)
        sb_s.append(new_sb)
        nsa_s.append(new_nsa)
        win_s.append(winc[:, s:])
        ssm_s.append(jnp.stack([sr, si], axis=1))

    fn = final_norm[None, :]
    y_prompt = _final_norm(xp, fn, tm_f).reshape(bp, t, D_MODEL)
    y_sample = _final_norm(xs, fn, bd).reshape(bd, s, D_MODEL)
    return (y_prompt, y_sample,
            jnp.stack(sb_p), jnp.stack(sb_s),
            jnp.stack(nsa_p), jnp.stack(nsa_s),
            jnp.stack(win_p), jnp.stack(win_s),
            jnp.stack(ssm_p), jnp.stack(ssm_s))
```

```python
import functools
import math

import jax
import jax.numpy as jnp
import numpy as np
from jax import lax
from jax.experimental import pallas as pl
from jax.experimental.pallas import tpu as pltpu

F32 = jnp.float32
BF16 = jnp.bfloat16

D_MODEL = 1024
DEPTH = 4
PAST_LEN = 2048
HEAD_DIM = 64
SSM_WIDTH = 256
SSM_GROUP = 16
SSM_GROUPS = 16
SSM_STATE = 64
SB_HEADS = 4
SB_WIDTH = 256
NSA_HEADS = 8
NSA_KV_HEADS = 2
NSA_REP = 4
NSA_WIDTH = 512
NSA_KV_WIDTH = 128
CMP_LEN = 32
CMP_STRIDE = 16
SEL_LEN = 64
SEL_TOPK = 16
WINDOW = 512
N_BRANCH = 3
Q_BLOCK = 128
ROPE_THETA = 10000.0
MOE_GROUPS = 4
EXPERTS_PER_GROUP = 4
N_EXPERTS = 16
EXPERT_FF = 256
EPS = 1e-6
NEG = -1e30
FORCE = 1e9

LANES = 128
VMEM_LIMIT = 56 * 1024 * 1024

_C_U = 0
_C_SQ = 256
_C_SKV = 512
_C_NQ = 1024
_C_NKV = 1536
_C_NG = 2304
_C_MG = 2432
_C_END = 5504


def _cparams(sem):
    return pltpu.CompilerParams(dimension_semantics=sem, vmem_limit_bytes=VMEM_LIMIT)


def _rope_slab(v, cos, sin_signed):
    lane = lax.broadcasted_iota(jnp.int32, v.shape, 1)
    first = (lane % HEAD_DIM) < (HEAD_DIM // 2)
    swapped = jnp.where(first, pltpu.roll(v, LANES - HEAD_DIM // 2, 1), pltpu.roll(v, HEAD_DIM // 2, 1))
    return v * cos + swapped * sin_signed


def _inproj_kernel(x_ref, shift_ref, scale_ref, gn_ref, cos_ref, sin_ref, w_ref,
                   u_ref, sq_ref, skv_ref, qn_ref, qr_ref, nsa_ref, win_ref, ng_ref, g_ref):
    x = x_ref[...]
    ms = jnp.mean(x * x, axis=-1, keepdims=True)
    h = x * lax.rsqrt(ms + EPS) * gn_ref[...]
    h = h * (1.0 + scale_ref[0]) + shift_ref[0]
    hb = h.astype(BF16)

    def mm(lo, hi):
        return jnp.dot(hb, w_ref[:, lo:hi], preferred_element_type=F32)

    cos = cos_ref[...]
    sin = sin_ref[...]
    u_ref[...] = mm(_C_U, _C_SQ)
    sq_ref[...] = mm(_C_SQ, _C_SKV)
    skv_ref[...] = mm(_C_SKV, _C_NQ)
    q = mm(_C_NQ, _C_NKV) * (1.0 / math.sqrt(HEAD_DIM))
    qn_ref[...] = q
    for s in range(NSA_WIDTH // LANES):
        qr_ref[:, s * LANES:(s + 1) * LANES] = _rope_slab(q[:, s * LANES:(s + 1) * LANES], cos, sin)
    kv = mm(_C_NKV, _C_NG)
    nsa_ref[:, 0:256] = kv[:, 0:256]
    nsa_ref[:, 256:384] = _rope_slab(kv[:, 256:384], cos, sin)
    nsa_ref[:, 384:512] = kv[:, 384:512]
    win_ref[:, 0:128] = _rope_slab(kv[:, 512:640], cos, sin)
    win_ref[:, 128:256] = kv[:, 640:768]
    ng_ref[...] = jax.nn.sigmoid(mm(_C_NG, _C_MG))
    g_ref[...] = jax.nn.sigmoid(mm(_C_MG, _C_END)).astype(g_ref.dtype)


def _inproj(x, shift, scale, gn, cos, sin, w, tm, mod_map, pos_map):
    n = x.shape[0]
    mrows = shift.shape[1]
    row = lambda width: pl.BlockSpec((tm, width), lambda i: (i, 0))
    widths = (256, 256, 512, 512, 512, 512, 256, 128, 3072)
    return pl.pallas_call(
        _inproj_kernel,
        grid=(n // tm,),
        in_specs=[
            row(D_MODEL),
            pl.BlockSpec((1, mrows, D_MODEL), lambda i: (mod_map(i), 0, 0)),
            pl.BlockSpec((1, mrows, D_MODEL), lambda i: (mod_map(i), 0, 0)),
            pl.BlockSpec((1, D_MODEL), lambda i: (0, 0)),
            pl.BlockSpec((tm, LANES), lambda i: (pos_map(i), 0)),
            pl.BlockSpec((tm, LANES), lambda i: (pos_map(i), 0)),
            pl.BlockSpec((D_MODEL, _C_END), lambda i: (0, 0)),
        ],
        out_specs=[row(wd) for wd in widths],
        out_shape=[jax.ShapeDtypeStruct((n, wd), BF16 if k == len(widths) - 1 else F32)
                   for k, wd in enumerate(widths)],
        compiler_params=_cparams(("parallel",)),
        name="inproj",
    )(x, shift, scale, gn, cos, sin, w)


def _merge_kernel(x_ref, gate_ref, ya_ref, yb_ref, yc_ref, g_ref, wglu_ref, bglu_ref, wb_ref, wo_ref, o_ref):
    def mm(a, w):
        return jnp.dot(a.astype(BF16), w, preferred_element_type=F32)

    ya = jax.nn.gelu(ya_ref[...])
    ya = ya * jax.nn.sigmoid(mm(ya, wglu_ref[...]) + bglu_ref[...])
    br_a = mm(ya, wb_ref[0:SSM_WIDTH, :])
    br_b = mm(yb_ref[...], wb_ref[SSM_WIDTH:SSM_WIDTH + SB_WIDTH, :])
    br_c = mm(yc_ref[...], wb_ref[SSM_WIDTH + SB_WIDTH:, :])
    merged = (g_ref[:, 0:D_MODEL] * br_a + g_ref[:, D_MODEL:2 * D_MODEL] * br_b
              + g_ref[:, 2 * D_MODEL:] * br_c)
    out = mm(merged, wo_ref[...])
    o_ref[...] = x_ref[...] + gate_ref[0] * out


def _merge(x, gate, ya, yb, yc, g, wglu, bglu, wb, wo, tm, mod_map):
    n = x.shape[0]
    mrows = gate.shape[1]
    row = lambda width: pl.BlockSpec((tm, width), lambda i: (i, 0))
    return pl.pallas_call(
        _merge_kernel,
        grid=(n // tm,),
        in_specs=[
            row(D_MODEL),
            pl.BlockSpec((1, mrows, D_MODEL), lambda i: (mod_map(i), 0, 0)),
            row(SSM_WIDTH), row(SB_WIDTH), row(NSA_WIDTH), row(N_BRANCH * D_MODEL),
            pl.BlockSpec((SSM_WIDTH, SSM_WIDTH), lambda i: (0, 0)),
            pl.BlockSpec((1, SSM_WIDTH), lambda i: (0, 0)),
            pl.BlockSpec((D_MODEL, D_MODEL), lambda i: (0, 0)),
            pl.BlockSpec((D_MODEL, D_MODEL), lambda i: (0, 0)),
        ],
        out_specs=row(D_MODEL),
        out_shape=jax.ShapeDtypeStruct((n, D_MODEL), F32),
        compiler_params=_cparams(("parallel",)),
        name="merge",
    )(x, gate, ya, yb, yc, g, wglu, bglu, wb, wo)


def _route(logits):
    lane = lax.broadcasted_iota(jnp.int32, logits.shape, 1)
    big = jnp.int32(1 << 20)
    is_g = lane < MOE_GROUPS
    gl = jnp.where(is_g, logits, NEG)
    gmax = jnp.max(gl, axis=-1, keepdims=True)
    p_sel = 1.0 / jnp.sum(jnp.where(is_g, jnp.exp(gl - gmax), 0.0), axis=-1, keepdims=True)
    g_sel = jnp.min(jnp.where(is_g & (gl == gmax), lane, big), axis=-1, keepdims=True)
    e_idx = lane - MOE_GROUPS
    in_grp = (e_idx >= g_sel * EXPERTS_PER_GROUP) & (e_idx < (g_sel + 1) * EXPERTS_PER_GROUP)
    el = jnp.where(in_grp, logits, NEG)
    m1 = jnp.max(el, axis=-1, keepdims=True)
    i1 = jnp.min(jnp.where(in_grp & (el == m1), lane, big), axis=-1, keepdims=True)
    el2 = jnp.where(lane == i1, NEG, el)
    m2 = jnp.max(el2, axis=-1, keepdims=True)
    i2 = jnp.min(jnp.where(in_grp & (lane != i1) & (el2 == m2), lane, big), axis=-1, keepdims=True)
    e2 = jnp.exp(m2 - m1)
    w1 = p_sel / (1.0 + e2)
    w2 = p_sel * e2 / (1.0 + e2)
    return jnp.where(lane == i1, w1, jnp.where(lane == i2, w2, 0.0))


def _ffn_kernel(x_ref, shift_ref, scale_ref, gate_ref, gn_ref, wr_ref, br_ref, wg_ref, wu_ref, wd_ref,
                o_ref, h_scr, comb_scr, acc_scr):
    e = pl.program_id(1)

    @pl.when(e == 0)
    def _():
        x = x_ref[...]
        ms = jnp.mean(x * x, axis=-1, keepdims=True)
        h = x * lax.rsqrt(ms + EPS) * gn_ref[...]
        h = h * (1.0 + scale_ref[0]) + shift_ref[0]
        logits = jnp.dot(h, wr_ref[...], preferred_element_type=F32,
                         precision=lax.Precision.HIGHEST) + br_ref[...]
        comb_scr[...] = _route(logits)
        h_scr[...] = h.astype(BF16)
        acc_scr[...] = jnp.zeros_like(acc_scr)

    hb = h_scr[...]
    a = jnp.concatenate([jnp.dot(hb, wg_ref[0, j], preferred_element_type=F32)
                         for j in range(EXPERTS_PER_GROUP)], axis=1)
    b = jnp.concatenate([jnp.dot(hb, wu_ref[0, j], preferred_element_type=F32)
                         for j in range(EXPERTS_PER_GROUP)], axis=1)
    comb = comb_scr[...]
    lane = lax.broadcasted_iota(jnp.int32, comb.shape, 1)
    tm = comb.shape[0]
    c = jnp.concatenate(
        [jnp.broadcast_to(jnp.sum(jnp.where(lane == MOE_GROUPS + e * EXPERTS_PER_GROUP + j, comb, 0.0),
                                  axis=-1, keepdims=True), (tm, EXPERT_FF))
         for j in range(EXPERTS_PER_GROUP)], axis=1)
    hid = (a * jax.nn.sigmoid(a)) * b * c
    acc_scr[...] += jnp.dot(hid.astype(BF16), wd_ref[0], preferred_element_type=F32)

    @pl.when(e == MOE_GROUPS - 1)
    def _():
        o_ref[...] = x_ref[...] + gate_ref[0] * acc_scr[...]


def _ffn(x, shift, scale, gate, gn, wr, br, wg, wu, wd, tm, mod_map):
    n = x.shape[0]
    mrows = shift.shape[1]
    mod = pl.BlockSpec((1, mrows, D_MODEL), lambda i, e: (mod_map(i), 0, 0))
    gff = EXPERTS_PER_GROUP * EXPERT_FF
    wspec = pl.BlockSpec((1, EXPERTS_PER_GROUP, D_MODEL, EXPERT_FF), lambda i, e: (e, 0, 0, 0))
    return pl.pallas_call(
        _ffn_kernel,
        grid=(n // tm, MOE_GROUPS),
        in_specs=[
            pl.BlockSpec((tm, D_MODEL), lambda i, e: (i, 0)),
            mod, mod, mod,
            pl.BlockSpec((1, D_MODEL), lambda i, e: (0, 0)),
            pl.BlockSpec((D_MODEL, LANES), lambda i, e: (0, 0)),
            pl.BlockSpec((1, LANES), lambda i, e: (0, 0)),
            wspec, wspec,
            pl.BlockSpec((1, gff, D_MODEL), lambda i, e: (e, 0, 0)),
        ],
        out_specs=pl.BlockSpec((tm, D_MODEL), lambda i, e: (i, 0)),
        out_shape=jax.ShapeDtypeStruct((n, D_MODEL), F32),
        scratch_shapes=[pltpu.VMEM((tm, D_MODEL), BF16), pltpu.VMEM((tm, LANES), F32),
                        pltpu.VMEM((tm, D_MODEL), F32)],
        compiler_params=_cparams(("parallel", "arbitrary")),
        name="ffn_moe",
    )(x, shift, scale, gate, gn, wr, br, wg, wu, wd)


def _final_norm_kernel(x_ref, g_ref, o_ref):
    x = x_ref[...]
    o_ref[...] = x * lax.rsqrt(jnp.mean(x * x, axis=-1, keepdims=True) + EPS) * g_ref[...]


def _final_norm(x, g, tm):
    n = x.shape[0]
    return pl.pallas_call(
        _final_norm_kernel,
        grid=(n // tm,),
        in_specs=[pl.BlockSpec((tm, D_MODEL), lambda i: (i, 0)), pl.BlockSpec((1, D_MODEL), lambda i: (0, 0))],
        out_specs=pl.BlockSpec((tm, D_MODEL), lambda i: (i, 0)),
        out_shape=jax.ShapeDtypeStruct((n, D_MODEL), F32),
        compiler_params=_cparams(("parallel",)),
        name="final_norm",
    )(x, g)


S5_CHUNK = 64
S5_LW = S5_CHUNK * SSM_GROUP


def _s5_mats(a_re, a_im, log_dt, b_re, b_im, c_re, c_im):
    L = S5_CHUNK
    hp = lax.Precision.HIGHEST
    dt = jnp.exp(log_dt)[:, None]
    k = jnp.arange(L + 1, dtype=F32)[:, None, None]
    mag = jnp.exp(a_re * dt * k)
    ang = a_im * dt * k
    lk_re, lk_im = mag * jnp.cos(ang), mag * jnp.sin(ang)
    lb_re, lb_im = lk_re[1], lk_im[1]
    den = a_re * a_re + a_im * a_im
    co_re = ((lb_re - 1.0) * a_re + lb_im * a_im) / den
    co_im = (lb_im * a_re - (lb_re - 1.0) * a_im) / den
    bb_re = co_re[..., None] * b_re - co_im[..., None] * b_im
    bb_im = co_re[..., None] * b_im + co_im[..., None] * b_re
    cl_re = c_re[None] * lk_re[:, :, None, :] - c_im[None] * lk_im[:, :, None, :]
    cl_im = c_re[None] * lk_im[:, :, None, :] + c_im[None] * lk_re[:, :, None, :]
    kk = (jnp.einsum('kgpn,gnq->kgpq', cl_re[:L], bb_re, precision=hp)
          - jnp.einsum('kgpn,gnq->kgpq', cl_im[:L], bb_im, precision=hp))
    kq = kk.transpose(1, 3, 0, 2).reshape(SSM_GROUPS, SSM_GROUP, S5_LW).astype(BF16)
    kp = jnp.concatenate([jnp.zeros_like(kq), kq], axis=-1)
    tmat = jnp.stack([kp[:, :, S5_LW - SSM_GROUP * i:2 * S5_LW - SSM_GROUP * i] for i in range(L)], axis=1)
    tmat = tmat.reshape(SSM_GROUPS, S5_LW, S5_LW)
    rev = L - 1 - jnp.arange(L)
    be_re = lk_re[rev][:, :, :, None] * bb_re[None] - lk_im[rev][:, :, :, None] * bb_im[None]
    be_im = lk_re[rev][:, :, :, None] * bb_im[None] + lk_im[rev][:, :, :, None] * bb_re[None]
    bmat = jnp.concatenate([be_re.transpose(1, 0, 3, 2).reshape(SSM_GROUPS, S5_LW, SSM_STATE),
                            be_im.transpose(1, 0, 3, 2).reshape(SSM_GROUPS, S5_LW, SSM_STATE)], axis=-1)
    cm_re = cl_re[1:].transpose(1, 3, 0, 2).reshape(SSM_GROUPS, SSM_STATE, S5_LW)
    cm_im = cl_im[1:].transpose(1, 3, 0, 2).reshape(SSM_GROUPS, SSM_STATE, S5_LW)
    cmat = jnp.concatenate([cm_re, -cm_im], axis=1)
    lbl = jnp.stack([lk_re[L], lk_im[L]], axis=1)
    return tmat.astype(BF16), bmat.astype(BF16), cmat.astype(BF16), lbl


def _s5_kernel(x_ref, t_ref, b_ref, c_ref, lbl_ref, d_ref, y_ref, sf_ref, er_scr, ei_scr, pr_scr, pi_scr):
    nb, _, nch, _ = x_ref.shape
    a_re = lbl_ref[0, 0:1, :]
    a_im = lbl_ref[0, 1:2, :]
    for b in range(nb):
        x = x_ref[b, 0]
        xb = x.astype(BF16)
        e = jnp.dot(xb, b_ref[0], preferred_element_type=F32)
        er_scr[...] = e[:, :SSM_STATE]
        ei_scr[...] = e[:, SSM_STATE:]

        def step(c, carry):
            s_re, s_im = carry
            pr_scr[pl.ds(c, 1), :] = s_re
            pi_scr[pl.ds(c, 1), :] = s_im
            n_re = a_re * s_re - a_im * s_im + er_scr[pl.ds(c, 1), :]
            n_im = a_re * s_im + a_im * s_re + ei_scr[pl.ds(c, 1), :]
            return n_re, n_im

        zero = jnp.zeros((1, SSM_STATE), F32)
        s_re, s_im = lax.fori_loop(0, nch, step, (zero, zero))
        sprev = jnp.concatenate([pr_scr[...], pi_scr[...]], axis=1).astype(BF16)
        y = (jnp.dot(xb, t_ref[0], preferred_element_type=F32)
             + jnp.dot(sprev, c_ref[0], preferred_element_type=F32) + d_ref[0] * x)
        y_ref[b, 0] = y
        sf_ref[b, 0] = jnp.concatenate([s_re, s_im], axis=1)


def _s5_prompt(u, mats, d):
    tmat, bmat, cmat, lbl = mats
    bsz, t = u.shape[:2]
    nch = t // S5_CHUNK
    x = u.reshape(bsz, nch, S5_CHUNK, SSM_GROUPS, SSM_GROUP).transpose(0, 3, 1, 2, 4).reshape(
        bsz, SSM_GROUPS, nch, S5_LW)
    dt = jnp.tile(d, (1, S5_CHUNK))[:, None, :]
    y, sf = pl.pallas_call(
        _s5_kernel,
        grid=(SSM_GROUPS,),
        in_specs=[
            pl.BlockSpec((bsz, 1, nch, S5_LW), lambda g: (0, g, 0, 0)),
            pl.BlockSpec((1, S5_LW, S5_LW), lambda g: (g, 0, 0)),
            pl.BlockSpec((1, S5_LW, 2 * SSM_STATE), lambda g: (g, 0, 0)),
            pl.BlockSpec((1, 2 * SSM_STATE, S5_LW), lambda g: (g, 0, 0)),
            pl.BlockSpec((1, 2, SSM_STATE), lambda g: (g, 0, 0)),
            pl.BlockSpec((1, 1, S5_LW), lambda g: (g, 0, 0)),
        ],
        out_specs=[pl.BlockSpec((bsz, 1, nch, S5_LW), lambda g: (0, g, 0, 0)),
                   pl.BlockSpec((bsz, 1, 1, 2 * SSM_STATE), lambda g: (0, g, 0, 0))],
        out_shape=[jax.ShapeDtypeStruct((bsz, SSM_GROUPS, nch, S5_LW), F32),
                   jax.ShapeDtypeStruct((bsz, SSM_GROUPS, 1, 2 * SSM_STATE), F32)],
        scratch_shapes=[pltpu.VMEM((nch, SSM_STATE), F32)] * 4,
        compiler_params=_cparams(("arbitrary",)),
        name="s5_prompt",
    )(x, tmat, bmat, cmat, lbl, dt)
    y = y.reshape(bsz, SSM_GROUPS, nch, S5_CHUNK, SSM_GROUP).transpose(0, 2, 3, 1, 4).reshape(bsz, t, SSM_WIDTH)
    sf = sf.reshape(bsz, SSM_GROUPS, 2, SSM_STATE).transpose(0, 2, 1, 3)
    return y, sf


SB_TK = 128
SB_TQ = 256
SB_PER = SB_TQ // SB_TK


def _sb_tri():
    j = np.arange(SB_TK)[:, None]
    s = np.arange(SB_TK)[None, :]
    return jnp.asarray(-(s > j).astype(np.float32), dtype=BF16)


def _sb_kernel(qt_ref, k_ref, vt_ref, tri_ref, o_ref, acc_scr, c_scr):
    qi = pl.program_id(1)
    tri = tri_ref[...]
    kidx = lax.broadcasted_iota(jnp.int32, (SB_TK, SB_TQ), 0)
    qidx = lax.broadcasted_iota(jnp.int32, (SB_TK, SB_TQ), 1)

    def softplus_tile(h, j, mask):
        kt = k_ref[0, h, pl.ds(pl.multiple_of(j * SB_TK, SB_TK), SB_TK), :]
        z = jnp.dot(kt, qt_ref[0, h], preferred_element_type=F32)
        sp = jnp.maximum(z, 0.0) + jnp.log(1.0 + jnp.exp(-jnp.abs(z)))
        if mask is not None:
            sp = jnp.where(mask, sp, 0.0)
        return z, sp

    def weights(z, sp, c, mask):
        r = jnp.dot(tri, sp.astype(BF16), preferred_element_type=F32)
        w = jnp.exp(z - sp + r - c)
        if mask is not None:
            w = jnp.where(mask, w, 0.0)
        return w.astype(BF16)

    def two_tiles(j_hi, masks):
        heads = range(SB_HEADS)
        zs1 = [softplus_tile(h, j_hi, masks[0]) for h in heads]
        zs0 = [softplus_tile(h, j_hi - 1, masks[1]) for h in heads]
        cs = [c_scr[h] for h in heads]
        t1 = [jnp.sum(zs1[h][1], axis=0, keepdims=True) for h in heads]
        w1 = [weights(zs1[h][0], zs1[h][1], cs[h], masks[0]) for h in heads]
        w0 = [weights(zs0[h][0], zs0[h][1], cs[h] + t1[h], masks[1]) for h in heads]
        for h in heads:
            acc_scr[h] += (jnp.dot(vt_ref[0, h, j_hi], w1[h], preferred_element_type=F32)
                           + jnp.dot(vt_ref[0, h, j_hi - 1], w0[h], preferred_element_type=F32))
            c_scr[h] = cs[h] + t1[h] + jnp.sum(zs0[h][1], axis=0, keepdims=True)

    acc_scr[...] = jnp.zeros_like(acc_scr)
    c_scr[...] = jnp.zeros_like(c_scr)
    two_tiles(SB_PER * qi + 1, (kidx + SB_TK < qidx, kidx < qidx))

    def body(p, carry):
        two_tiles(SB_PER * (qi - 1 - p) + 1, (None, None))
        return carry

    lax.fori_loop(0, qi, body, 0)
    o_ref[0] = acc_scr[...]


def _sb_prompt(sq, skv, bsz, t):
    assert SB_PER == 2
    nt = t // SB_TK
    nq = t // SB_TQ
    qt = (sq * (1.0 / math.sqrt(HEAD_DIM))).reshape(bsz, t, SB_HEADS, HEAD_DIM).transpose(0, 2, 3, 1).astype(BF16)
    k = skv[:, :SB_WIDTH].reshape(bsz, t, SB_HEADS, HEAD_DIM).transpose(0, 2, 1, 3).astype(BF16)
    vt = skv[:, SB_WIDTH:].reshape(bsz, nt, SB_TK, SB_HEADS, HEAD_DIM).transpose(0, 3, 1, 4, 2).astype(BF16)
    yt = pl.pallas_call(
        _sb_kernel,
        grid=(bsz, nq),
        in_specs=[
            pl.BlockSpec((1, SB_HEADS, HEAD_DIM, SB_TQ), lambda b, i: (b, 0, 0, i)),
            pl.BlockSpec((1, SB_HEADS, t, HEAD_DIM), lambda b, i: (b, 0, 0, 0)),
            pl.BlockSpec((1, SB_HEADS, nt, HEAD_DIM, SB_TK), lambda b, i: (b, 0, 0, 0, 0)),
            pl.BlockSpec((SB_TK, SB_TK), lambda b, i: (0, 0)),
        ],
        out_specs=pl.BlockSpec((1, SB_HEADS, HEAD_DIM, SB_TQ), lambda b, i: (b, 0, 0, i)),
        out_shape=jax.ShapeDtypeStruct((bsz, SB_HEADS, HEAD_DIM, t), F32),
        scratch_shapes=[pltpu.VMEM((SB_HEADS, HEAD_DIM, SB_TQ), F32), pltpu.VMEM((SB_HEADS, 1, SB_TQ), F32)],
        compiler_params=_cparams(("parallel", "arbitrary")),
        name="sb_prompt",
    )(qt, k, vt, _sb_tri())
    return yt.transpose(0, 3, 1, 2).reshape(bsz * t, SB_WIDTH)


NSA_TQ = 128
NSA_TQC = 512
NSA_KC = 1024
MASK_BIG = 2.0 ** 100
CMP_CHUNK_W = CMP_STRIDE * 2 * NSA_KV_WIDTH


def _cmp_weights(w_cmp, pe_cmp):
    r = CMP_LEN // CMP_STRIDE
    wj = w_cmp.reshape(2, r, CMP_STRIDE, HEAD_DIM, HEAD_DIM)
    eye = jnp.eye(2 * NSA_KV_HEADS, dtype=F32).reshape(2, NSA_KV_HEADS, 2, NSA_KV_HEADS)
    wb = jnp.einsum('kjsde,kgmh->jskgdmhe', wj, eye)
    wbig = wb.reshape(r, CMP_CHUNK_W, 2 * NSA_KV_WIDTH).transpose(1, 0, 2).reshape(CMP_CHUNK_W, r * 2 * NSA_KV_WIDTH)
    pj = pe_cmp.reshape(2, r, CMP_STRIDE, HEAD_DIM)
    pe = jnp.broadcast_to(pj.transpose(1, 2, 0, 3)[:, :, :, None, :],
                          (r, CMP_STRIDE, 2, NSA_KV_HEADS, HEAD_DIM)).reshape(r, CMP_CHUNK_W)
    pe8 = jnp.concatenate([pe, jnp.zeros((16 - r, CMP_CHUNK_W), F32)], axis=0)
    return wbig.astype(BF16), pe8.astype(BF16)


def _compress_kernel(x_ref, w_ref, pe_ref, o_ref):
    w = w_ref[...]
    width = 2 * NSA_KV_WIDTH
    a = jnp.dot(x_ref[0].astype(BF16), w, preferred_element_type=F32)
    pb = jnp.dot(pe_ref[...], w, preferred_element_type=F32)
    bias = pb[0:1, :width] + pb[1:2, width:]
    nch = a.shape[0]
    a1 = pltpu.roll(a[:, width:], nch - 1, 0)
    o_ref[0] = a[:, :width] + a1 + bias


def _nsa_compress(xc, wbig, pe8):
    bsz, nch, _ = xc.shape
    return pl.pallas_call(
        _compress_kernel,
        grid=(bsz,),
        in_specs=[pl.BlockSpec((1, nch, CMP_CHUNK_W), lambda b: (b, 0, 0)),
                  pl.BlockSpec(wbig.shape, lambda b: (0, 0)),
                  pl.BlockSpec(pe8.shape, lambda b: (0, 0))],
        out_specs=pl.BlockSpec((1, nch, 2 * NSA_KV_WIDTH), lambda b: (b, 0, 0)),
        out_shape=jax.ShapeDtypeStruct((bsz, nch, 2 * NSA_KV_WIDTH), F32),
        compiler_params=_cparams(("parallel",)),
        name="nsa_compress",
    )(xc, wbig, pe8)


def _split_hi_lo(x):
    hi = x.astype(BF16)
    lo = (x - hi.astype(F32)).astype(BF16)
    return jnp.concatenate([hi, lo], axis=1)


def _topk_unselected(imp, cur, axis=1):
    blk = lax.broadcasted_iota(jnp.int32, imp.shape, axis)
    imp = jnp.where(blk == cur, FORCE, jnp.where(blk < cur, imp, -1.0))
    unsel = jnp.ones(imp.shape, F32)
    for _ in range(SEL_TOPK):
        m = jnp.max(imp, axis=axis, keepdims=True)
        idx = jnp.min(jnp.where(imp == m, blk, LANES), axis=axis, keepdims=True)
        hit = blk == idx
        unsel = jnp.where(hit, 0.0, unsel)
        imp = jnp.where(hit, -3.0e38, imp)
    return jnp.where(blk <= cur, unsel, 1.0)


def _nsa_cmp_kernel(qt_ref, kc_ref, vct_ref, cmap_ref, ocmp_ref, unsel_ref):
    tq = NSA_TQC
    ncp = kc_ref.shape[2]
    q0 = pl.program_id(2) * tq
    qpos = q0 + lax.broadcasted_iota(jnp.int32, (1, tq), 1)
    cidx = lax.broadcasted_iota(jnp.int32, (ncp, tq), 0)
    mc = (cidx * CMP_STRIDE + (CMP_LEN - 1) <= qpos) & (cidx < ncp - 1)
    kc = kc_ref[0, 0]
    vct = vct_ref[0, 0]
    psum = jnp.zeros((ncp, tq), F32)
    outs = []
    for r in range(NSA_REP):
        s = jnp.where(mc, jnp.dot(kc, qt_ref[0, 0, 0, :, r * tq:(r + 1) * tq], preferred_element_type=F32), NEG)
        e = jnp.exp(s - jnp.max(s, axis=0, keepdims=True))
        p = jnp.where(mc, e / jnp.sum(e, axis=0, keepdims=True), 0.0)
        outs.append(jnp.dot(vct, p.astype(BF16), preferred_element_type=F32))
        psum = psum + p
    for i in range(tq // NSA_TQ):
        ocmp_ref[0, 0, i] = jnp.concatenate([o[:, i * NSA_TQ:(i + 1) * NSA_TQ] for o in outs], axis=1)
    ph = psum.astype(BF16)
    pl_ = (psum - ph.astype(F32)).astype(BF16)
    imp = jnp.dot(cmap_ref[...], jnp.concatenate([ph, pl_], axis=0), preferred_element_type=F32)
    unsel = _topk_unselected(imp, qpos // SEL_LEN, axis=0)
    for i in range(tq // NSA_TQ):
        unsel_ref[0, i] = unsel[:, i * NSA_TQ:(i + 1) * NSA_TQ]


def _nsa_cmp(qnt, kc, vct, cmap2t, bsz, t):
    nqt = t // NSA_TQC
    per = NSA_TQC // NSA_TQ
    ncp = kc.shape[2]
    return pl.pallas_call(
        _nsa_cmp_kernel,
        grid=(bsz, NSA_KV_HEADS, nqt),
        in_specs=[
            pl.BlockSpec((1, 1, 1, HEAD_DIM, NSA_REP * NSA_TQC), lambda b, g, i: (b, g, i, 0, 0)),
            pl.BlockSpec((1, 1, ncp, HEAD_DIM), lambda b, g, i: (b, g, 0, 0)),
            pl.BlockSpec((1, 1, HEAD_DIM, ncp), lambda b, g, i: (b, g, 0, 0)),
            pl.BlockSpec(cmap2t.shape, lambda b, g, i: (0, 0)),
        ],
        out_specs=[pl.BlockSpec((1, 1, per, HEAD_DIM, NSA_COLS), lambda b, g, i: (b, g, i, 0, 0)),
                   pl.BlockSpec((1, per, LANES, NSA_TQ), lambda b, g, i: (g, b * nqt + i, 0, 0))],
        out_shape=[jax.ShapeDtypeStruct((bsz, NSA_KV_HEADS, t // NSA_TQ, HEAD_DIM, NSA_COLS), F32),
                   jax.ShapeDtypeStruct((NSA_KV_HEADS, bsz * t // NSA_TQ, LANES, NSA_TQ), F32)],
        compiler_params=_cparams(("parallel", "parallel", "arbitrary")),
        name="nsa_cmp_select",
    )(qnt, kc, vct, cmap2t)


NSA_COLS = NSA_REP * NSA_TQ
NSA_KAUG = HEAD_DIM + LANES


def _nsa_selt_kernel(qt_ref, unsel_ref, ocmp_ref, gate_ref, ksa_ref, vst_ref, vsd_ref, kw_ref, vwt_ref, gexp_ref,
                     y_ref, acc_scr):
    tq = NSA_TQ
    qi = pl.program_id(2)
    qt = qt_ref[0, 0, 0]
    unsel = unsel_ref[0, 0]
    krow = lax.broadcasted_iota(jnp.int32, (tq, tq), 0)
    qcol = lax.broadcasted_iota(jnp.int32, (tq, tq), 1)

    def cols4(x):
        return jnp.concatenate([x] * NSA_REP, axis=1)

    u0 = unsel_ref[0, 0, pl.ds(2 * qi, 1), :]
    u1 = unsel_ref[0, 0, pl.ds(2 * qi + 1, 1), :]
    ud = jnp.where(krow < SEL_LEN, u0, u1)
    bias_d = jnp.where((ud < 0.5) & (krow <= qcol), 0.0, -MASK_BIG)
    kd = ksa_ref[0, 0, pl.ds(pl.multiple_of(qi * tq, tq), tq), 0:HEAD_DIM]
    s = jnp.dot(kd, qt, preferred_element_type=F32) + cols4(bias_d)
    m0 = jnp.max(s, axis=0, keepdims=True)
    p = jnp.exp(s - m0)
    l0 = jnp.sum(p, axis=0, keepdims=True)
    acc_scr[...] = jnp.dot(vsd_ref[0, 0, qi], p.astype(BF16), preferred_element_type=F32)

    unsel_past = jnp.where(krow >= 2 * qi, 1.0, unsel).astype(BF16)
    rhs = jnp.concatenate([qt, cols4(unsel_past)], axis=0)

    def chunk(c, carry):
        m_old, l_old = carry
        ka = ksa_ref[0, 0, pl.ds(pl.multiple_of(c * NSA_KC, NSA_KC), NSA_KC), :]
        s = jnp.dot(ka, rhs, preferred_element_type=F32)
        m_new = jnp.maximum(m_old, jnp.max(s, axis=0, keepdims=True))
        alpha = jnp.exp(m_old - m_new)
        p = jnp.exp(s - m_new)
        acc_scr[...] = acc_scr[...] * alpha + jnp.dot(vst_ref[0, 0, c], p.astype(BF16),
                                                      preferred_element_type=F32)
        return m_new, l_old * alpha + jnp.sum(p, axis=0, keepdims=True)

    _, l_sel = lax.fori_loop(0, (qi * tq + NSA_KC - 1) // NSA_KC, chunk, (m0, l0))
    o_sel_t = acc_scr[...] / l_sel

    ntw = (WINDOW + tq) // tq
    t0 = jnp.maximum(qi - WINDOW // tq, 0)
    kw = kw_ref[0, 0, pl.ds(pl.multiple_of(t0 * tq, tq), ntw * tq), :]
    kpos = t0 * tq + lax.broadcasted_iota(jnp.int32, (ntw * tq, tq), 0)
    qpw = qi * tq + lax.broadcasted_iota(jnp.int32, (ntw * tq, tq), 1)
    bias_w = jnp.where((kpos <= qpw) & (kpos > qpw - WINDOW), 0.0, NEG)
    sw = jnp.dot(kw, qt, preferred_element_type=F32) + cols4(bias_w)
    pw = jnp.exp(sw - jnp.max(sw, axis=0, keepdims=True))
    lw = jnp.sum(pw, axis=0, keepdims=True)
    pwb = pw.astype(BF16)
    o_win_t = jnp.zeros((HEAD_DIM, NSA_COLS), F32)
    for i in range(ntw):
        o_win_t = o_win_t + jnp.dot(vwt_ref[0, 0, t0 + i], pwb[i * tq:(i + 1) * tq], preferred_element_type=F32)
    o_win_t = o_win_t / lw

    o_cmp_t = ocmp_ref[0, 0, 0]

    def cols(x, r):
        return x[:, r * tq:(r + 1) * tq]

    both = [jnp.concatenate([cols(o_sel_t, r), cols(o_win_t, r)], axis=0).T for r in range(NSA_REP)]
    cmp2 = [jnp.concatenate([cols(o_cmp_t, r), cols(o_cmp_t, r + 1)], axis=0).T for r in range(0, NSA_REP, 2)]
    o_sel = jnp.concatenate([x[:, :HEAD_DIM] for x in both], axis=1)
    o_win = jnp.concatenate([x[:, HEAD_DIM:] for x in both], axis=1)
    o_cmp = jnp.concatenate(cmp2, axis=1)
    ge = jnp.dot(_split_hi_lo(gate_ref[...]), gexp_ref[0], preferred_element_type=F32)
    w = NSA_REP * HEAD_DIM
    y_ref[...] = ge[:, 0:w] * o_cmp + ge[:, w:2 * w] * o_sel + ge[:, 2 * w:3 * w] * o_win


def _nsa_selt(qt, unsel_t, ocmp, gates, ksa, vst, vsd, kw, vwt, gexp, bsz, t):
    nq = t // NSA_TQ
    w = NSA_REP * HEAD_DIM
    full = lambda a: pl.BlockSpec((1, 1) + a.shape[2:], lambda b, g, i: (b, g) + (0,) * (a.ndim - 2))
    return pl.pallas_call(
        _nsa_selt_kernel,
        grid=(bsz, NSA_KV_HEADS, nq),
        in_specs=[
            pl.BlockSpec((1, 1, 1, HEAD_DIM, NSA_COLS), lambda b, g, i: (b, g, i, 0, 0)),
            pl.BlockSpec((1, 1, LANES, NSA_TQ), lambda b, g, i: (g, b * nq + i, 0, 0)),
            pl.BlockSpec((1, 1, 1, HEAD_DIM, NSA_COLS), lambda b, g, i: (b, g, i, 0, 0)),
            pl.BlockSpec((NSA_TQ, LANES), lambda b, g, i: (b * nq + i, 0)),
            full(ksa), full(vst), full(vsd), full(kw), full(vwt),
            pl.BlockSpec((1,) + gexp.shape[1:], lambda b, g, i: (g, 0, 0)),
        ],
        out_specs=pl.BlockSpec((NSA_TQ, w), lambda b, g, i: (b * nq + i, g)),
        out_shape=jax.ShapeDtypeStruct((bsz * t, NSA_WIDTH), F32),
        scratch_shapes=[pltpu.VMEM((HEAD_DIM, NSA_COLS), F32)],
        compiler_params=_cparams(("parallel", "parallel", "arbitrary")),
        name="nsa_sel_win",
    )(qt, unsel_t, ocmp, gates, ksa, vst, vsd, kw, vwt, gexp)


def _nsa_consts(t):
    ncp = t // CMP_STRIDE
    ns = t // SEL_LEN
    cs = np.arange(ncp) * CMP_STRIDE
    ss = np.arange(LANES) * SEL_LEN
    ov = np.clip(np.minimum(cs[:, None] + CMP_LEN, ss[None, :] + SEL_LEN) - np.maximum(cs[:, None], ss[None, :]), 0, None)
    cmap = (ov / CMP_LEN).astype(np.float32)
    cmap[ncp - 1:, :] = 0.0
    cmap[:, ns:] = 0.0
    cmap2 = jnp.asarray(np.concatenate([cmap, cmap], axis=0), dtype=BF16)
    key_blk = (np.arange(t) // SEL_LEN)[:, None]
    bige = jnp.asarray(np.where(np.arange(LANES)[None, :] == key_blk, -MASK_BIG, 0.0), dtype=BF16)
    ge = np.zeros((NSA_KV_HEADS, LANES, 3 * NSA_REP * HEAD_DIM), np.float32)
    for g in range(NSA_KV_HEADS):
        for r in range(NSA_REP):
            for j in range(3):
                c0 = j * NSA_REP * HEAD_DIM + r * HEAD_DIM
                ge[g, (g * NSA_REP + r) * 3 + j, c0:c0 + HEAD_DIM] = 1.0
    gexp = jnp.asarray(np.concatenate([ge, ge], axis=1), dtype=BF16)
    return cmap2, bige, gexp, cmap2.T


def _nsa_prompt(qn, qr, nsa, win, gates, wbig, pe8, consts, bsz, t):
    _, bige, gexp, cmap2t = consts
    ncp = t // CMP_STRIDE
    xc = nsa[:, :2 * NSA_KV_WIDTH].reshape(bsz, ncp, CMP_CHUNK_W)
    kcv = _nsa_compress(xc, wbig, pe8).reshape(bsz, ncp, 2, NSA_KV_HEADS, HEAD_DIM)
    kc = kcv[:, :, 0].transpose(0, 2, 1, 3).astype(BF16)
    vct = kcv[:, :, 1].transpose(0, 2, 3, 1).astype(BF16)
    nqt = t // NSA_TQC
    qnt = qn.reshape(bsz, nqt, NSA_TQC, NSA_KV_HEADS, NSA_REP, HEAD_DIM).transpose(0, 3, 1, 5, 4, 2)
    qnt = qnt.reshape(bsz, NSA_KV_HEADS, nqt, HEAD_DIM, NSA_REP * NSA_TQC).astype(BF16)
    ocmp, unsel_t = _nsa_cmp(qnt, kc, vct, cmap2t, bsz, t)
    nq = t // NSA_TQ
    qt = qr.reshape(bsz, nq, NSA_TQ, NSA_KV_HEADS, NSA_REP, HEAD_DIM).transpose(0, 3, 1, 5, 4, 2)
    qt = qt.reshape(bsz, NSA_KV_HEADS, nq, HEAD_DIM, NSA_COLS).astype(BF16)

    def rows_major(x):
        return x.reshape(bsz, t, NSA_KV_HEADS, HEAD_DIM).transpose(0, 2, 1, 3).astype(BF16)

    def tiles_t(x, tk):
        return x.reshape(bsz, t // tk, tk, NSA_KV_HEADS, HEAD_DIM).transpose(0, 3, 1, 4, 2).astype(BF16)

    ksa = jnp.concatenate([rows_major(nsa[:, 256:384]),
                           jnp.broadcast_to(bige, (bsz, NSA_KV_HEADS, t, LANES))], axis=-1)
    vst = tiles_t(nsa[:, 384:512], NSA_KC)
    vsd = tiles_t(nsa[:, 384:512], NSA_TQ)
    kw = rows_major(win[:, :128])
    vwt = tiles_t(win[:, 128:], NSA_TQ)
    return _nsa_selt(qt, unsel_t, ocmp, gates, ksa, vst, vsd, kw, vwt, gexp, bsz, t)


PAGE = 128
N_PAGES = PAST_LEN // PAGE
HROWS = 8
NT_DIMS = (((1,), (1,)), ((), ()))


def _sb_tri_rows():
    s = np.arange(PAGE)[:, None]
    j = np.arange(2 * PAGE)[None, :]
    u = -((s > j) | (j >= PAGE)).astype(np.float32)
    return jnp.asarray(np.concatenate([u, u], axis=0), dtype=BF16)


def _pages_t(cache):
    dp, npool = cache.shape[:2]
    return cache.transpose(0, 1, 3, 4, 5, 2).reshape(dp * npool, -1, PAGE)


def _page_specs(feat, blk):
    return [pl.BlockSpec((1, feat, PAGE), lambda b, pt, p=p: (pt[b, p], blk, 0)) for p in range(N_PAGES)]


def _dec_sb_kernel(pt_ref, q_ref, *rest):
    pages = rest[:N_PAGES]
    tri_ref, dmask_ref, y_ref = rest[N_PAGES:]
    q8 = q_ref[0]
    z = jnp.concatenate(
        [jnp.dot(q8, pg[0, 0:SB_WIDTH, :].astype(BF16), preferred_element_type=F32) for pg in pages],
        axis=0)
    sp = jnp.maximum(z, 0.0) + jnp.log(1.0 + jnp.exp(-jnp.abs(z)))
    rt = jnp.dot(_split_hi_lo(sp), tri_ref[...], preferred_element_type=F32)
    cs = [None] * N_PAGES
    c = jnp.zeros((HROWS, PAGE), F32)
    for p in range(N_PAGES - 1, -1, -1):
        cs[p] = c
        c = c + rt[p * HROWS:(p + 1) * HROWS, PAGE:]
    w = jnp.exp(z - sp + rt[:, :PAGE] + jnp.concatenate(cs, axis=0)).astype(BF16)
    y8 = jnp.zeros((HROWS, SB_WIDTH), F32)
    for p, pg in enumerate(pages):
        y8 = y8 + lax.dot_general(w[p * HROWS:(p + 1) * HROWS], pg[0, SB_WIDTH:, :].astype(BF16), NT_DIMS,
                                  preferred_element_type=F32)
    y_ref[0] = jnp.sum(y8 * dmask_ref[...], axis=0, keepdims=True)


def _dec_sb(rows, sq, cache2d):
    bd = sq.shape[0]
    hm = np.zeros((HROWS, SB_WIDTH), np.float32)
    for h in range(SB_HEADS):
        hm[h, h * HEAD_DIM:(h + 1) * HEAD_DIM] = 1.0
    q8 = ((sq * (1.0 / math.sqrt(HEAD_DIM)))[:, None, :] * hm[None]).astype(BF16)
    grid_spec = pltpu.PrefetchScalarGridSpec(
        num_scalar_prefetch=1, grid=(bd,),
        in_specs=[pl.BlockSpec((1, HROWS, SB_WIDTH), lambda b, pt: (b, 0, 0))] + _page_specs(2 * SB_WIDTH, 0)
        + [pl.BlockSpec((2 * PAGE, 2 * PAGE), lambda b, pt: (0, 0)),
           pl.BlockSpec((HROWS, SB_WIDTH), lambda b, pt: (0, 0))],
        out_specs=pl.BlockSpec((1, 1, SB_WIDTH), lambda b, pt: (b, 0, 0)))
    y = pl.pallas_call(
        _dec_sb_kernel, grid_spec=grid_spec,
        out_shape=jax.ShapeDtypeStruct((bd, 1, SB_WIDTH), F32),
        compiler_params=_cparams(("arbitrary",)),
        name="dec_sb",
    )(rows, q8, *([cache2d] * N_PAGES), _sb_tri_rows(), jnp.asarray(hm))
    return y.reshape(bd, SB_WIDTH)


def _dec_cmp_kernel(pt_ref, q_ref, *rest):
    pages = rest[:N_PAGES]
    w_ref, pe_ref, cmap_ref, perm_ref, o_ref, imp_ref = rest[N_PAGES:]
    width = 2 * NSA_KV_WIDTH
    cpp = PAGE // CMP_STRIDE
    ncp = N_PAGES * cpp
    perm = perm_ref[...]
    pp = [lax.dot_general(perm, pg[0].astype(BF16), NT_DIMS, preferred_element_type=F32) for pg in pages]
    xc = jnp.concatenate(
        [jnp.concatenate([x[s * cpp:(s + 1) * cpp] for x in pp], axis=0).astype(BF16) for s in range(CMP_STRIDE)],
        axis=1)
    aa = jnp.dot(jnp.concatenate([xc, pe_ref[...]], axis=0), w_ref[...], preferred_element_type=F32)
    a = aa[:ncp]
    bias = aa[ncp:ncp + 1, :width] + aa[ncp + 1:ncp + 2, width:]
    kcv = a[:, :width] + pltpu.roll(a[:, width:], ncp - 1, 0) + bias
    q8 = q_ref[0]
    s8 = lax.dot_general(q8, kcv[:, :NSA_KV_WIDTH].astype(BF16), NT_DIMS, preferred_element_type=F32)
    valid = lax.broadcasted_iota(jnp.int32, s8.shape, 1) < ncp - 1
    s8 = jnp.where(valid, s8, NEG)
    e = jnp.exp(s8 - jnp.max(s8, axis=-1, keepdims=True))
    p = jnp.where(valid, e / jnp.sum(e, axis=-1, keepdims=True), 0.0)
    o_ref[0] = jnp.dot(p.astype(BF16), kcv[:, NSA_KV_WIDTH:].astype(BF16), preferred_element_type=F32)
    row = lax.broadcasted_iota(jnp.int32, p.shape, 0)
    psum = jnp.where(row < NSA_REP, jnp.sum(p[:NSA_REP], axis=0, keepdims=True),
                     jnp.sum(p[NSA_REP:], axis=0, keepdims=True))
    imp_ref[0] = jnp.dot(_split_hi_lo(psum), cmap_ref[...], preferred_element_type=F32)


def _head_rows(x, bd):
    x = x.reshape(bd, NSA_KV_HEADS, NSA_REP, 1, HEAD_DIM)
    eye = jnp.eye(NSA_KV_HEADS, dtype=x.dtype).reshape(1, NSA_KV_HEADS, 1, NSA_KV_HEADS, 1)
    return (x * eye).reshape(bd, NSA_HEADS, NSA_KV_WIDTH)


def _dec_cmp(rows, qn, cache2d, wbig, pe8, cmap2):
    bd = qn.shape[0]
    q8 = _head_rows(qn, bd).astype(BF16)
    cpp = PAGE // CMP_STRIDE
    pm = np.zeros((PAGE, PAGE), np.float32)
    for s in range(CMP_STRIDE):
        for n in range(cpp):
            pm[s * cpp + n, CMP_STRIDE * n + s] = 1.0
    perm = jnp.asarray(pm, dtype=BF16)
    grid_spec = pltpu.PrefetchScalarGridSpec(
        num_scalar_prefetch=1, grid=(bd,),
        in_specs=[pl.BlockSpec((1, NSA_HEADS, NSA_KV_WIDTH), lambda b, pt: (b, 0, 0))]
        + _page_specs(2 * NSA_KV_WIDTH, 0)
        + [pl.BlockSpec(wbig.shape, lambda b, pt: (0, 0)), pl.BlockSpec(pe8.shape, lambda b, pt: (0, 0)),
           pl.BlockSpec(cmap2.shape, lambda b, pt: (0, 0)), pl.BlockSpec((PAGE, PAGE), lambda b, pt: (0, 0))],
        out_specs=[pl.BlockSpec((1, NSA_HEADS, NSA_KV_WIDTH), lambda b, pt: (b, 0, 0)),
                   pl.BlockSpec((1, NSA_HEADS, LANES), lambda b, pt: (b, 0, 0))])
    return pl.pallas_call(
        _dec_cmp_kernel, grid_spec=grid_spec,
        out_shape=[jax.ShapeDtypeStruct((bd, NSA_HEADS, NSA_KV_WIDTH), F32),
                   jax.ShapeDtypeStruct((bd, NSA_HEADS, LANES), F32)],
        compiler_params=_cparams(("arbitrary",)),
        name="dec_cmp",
    )(rows, q8, *([cache2d] * N_PAGES), wbig, pe8, cmap2, perm)


def _dec_topk_kernel(imp_ref, unsel_ref):
    unsel_ref[...] = _topk_unselected(imp_ref[...], PAST_LEN // SEL_LEN)


def _dec_topk(imp):
    return pl.pallas_call(
        _dec_topk_kernel,
        out_shape=jax.ShapeDtypeStruct(imp.shape, F32),
        name="dec_topk",
    )(imp)


def _dec_sel_kernel(pt_ref, q_ref, unsel_ref, ocmp_ref, gate_ref, new_ref, win_ref, nwin_ref, *rest):
    pages = rest[:N_PAGES]
    bige_ref, y_ref = rest[N_PAGES:]
    q8 = q_ref[0]
    q8f = q8.astype(F32)

    def new_key_score(krow):
        return jnp.sum(q8f * krow.astype(BF16).astype(F32), axis=-1, keepdims=True)

    def attend(s3, s_new, vals_t, v_new):
        m = jnp.maximum(jnp.max(jnp.max(s3, axis=0), axis=-1, keepdims=True), s_new)
        p3 = jnp.exp(s3 - m[None])
        p_new = jnp.exp(s_new - m)
        den = jnp.sum(jnp.sum(p3, axis=0), axis=-1, keepdims=True) + p_new
        o = p_new * v_new.astype(BF16).astype(F32)
        for i, vt in enumerate(vals_t):
            o = o + lax.dot_general(p3[i].astype(BF16), vt, NT_DIMS, preferred_element_type=F32)
        return o / den

    bias = jnp.dot(unsel_ref[0].astype(BF16), bige_ref[...], preferred_element_type=F32)
    s_sel = jnp.stack(
        [jnp.dot(q8, pg[0, 0:NSA_KV_WIDTH, :].astype(BF16), preferred_element_type=F32)
         + bias[:, p * PAGE:(p + 1) * PAGE] for p, pg in enumerate(pages)], axis=0)
    new = new_ref[0]
    o_sel = attend(s_sel, new_key_score(new[:, 2 * NSA_KV_WIDTH:3 * NSA_KV_WIDTH]),
                   [pg[0, NSA_KV_WIDTH:, :].astype(BF16) for pg in pages], new[:, 3 * NSA_KV_WIDTH:])

    nt = WINDOW // PAGE
    kpos0 = lax.broadcasted_iota(jnp.int32, (HROWS, PAGE), 1) == 0
    s_win = []
    for i in range(nt):
        s = jnp.dot(q8, win_ref[0, 0:NSA_KV_WIDTH, i * PAGE:(i + 1) * PAGE].astype(BF16),
                    preferred_element_type=F32)
        s_win.append(jnp.where(kpos0, NEG, s) if i == 0 else s)
    nwin = nwin_ref[0]
    o_win = attend(jnp.stack(s_win, axis=0), new_key_score(nwin[:, :NSA_KV_WIDTH]),
                   [win_ref[0, NSA_KV_WIDTH:, i * PAGE:(i + 1) * PAGE].astype(BF16) for i in range(nt)],
                   nwin[:, NSA_KV_WIDTH:])

    g = gate_ref[0]
    y8 = g[:, 0:LANES] * ocmp_ref[0] + g[:, LANES:2 * LANES] * o_sel + g[:, 2 * LANES:] * o_win
    for h in range(NSA_HEADS):
        c0 = (h // NSA_REP) * HEAD_DIM
        y_ref[0, :, h * HEAD_DIM:(h + 1) * HEAD_DIM] = y8[h:h + 1, c0:c0 + HEAD_DIM]


def _dec_sel(rows, qr, unsel, ocmp, gates, new_nsa, win_state, new_win, cache2d, bige):
    bd = qr.shape[0]
    q8 = _head_rows(qr, bd).astype(BF16)
    g3 = jnp.broadcast_to(gates[:, :3 * NSA_HEADS].reshape(bd, NSA_HEADS, 3, 1),
                          (bd, NSA_HEADS, 3, LANES)).reshape(bd, NSA_HEADS, 3 * LANES)
    per_b = lambda *shape: pl.BlockSpec((1,) + shape, lambda b, pt: (b,) + (0,) * len(shape))
    grid_spec = pltpu.PrefetchScalarGridSpec(
        num_scalar_prefetch=1, grid=(bd,),
        in_specs=[per_b(NSA_HEADS, NSA_KV_WIDTH), per_b(NSA_HEADS, LANES), per_b(NSA_HEADS, NSA_KV_WIDTH),
                  per_b(NSA_HEADS, 3 * LANES), per_b(1, 4 * NSA_KV_WIDTH), per_b(2 * NSA_KV_WIDTH, WINDOW),
                  per_b(1, 2 * NSA_KV_WIDTH)]
        + _page_specs(2 * NSA_KV_WIDTH, 1)
        + [pl.BlockSpec(bige.shape, lambda b, pt: (0, 0))],
        out_specs=per_b(1, NSA_WIDTH))
    y = pl.pallas_call(
        _dec_sel_kernel, grid_spec=grid_spec,
        out_shape=jax.ShapeDtypeStruct((bd, 1, NSA_WIDTH), F32),
        compiler_params=_cparams(("arbitrary",)),
        name="dec_sel_win",
    )(rows, q8, unsel, ocmp, g3, new_nsa.reshape(bd, 1, -1), win_state, new_win.reshape(bd, 1, -1),
      *([cache2d] * N_PAGES), bige)
    return y.reshape(bd, NSA_WIDTH)


def _dec_consts():
    cmap2 = _nsa_consts(PAST_LEN)[0]
    blk = np.arange(PAST_LEN) // SEL_LEN
    bige = jnp.asarray(np.where(np.arange(LANES)[:, None] == blk[None, :], -MASK_BIG, 0.0), dtype=BF16)
    return cmap2, bige


def _cplx_affine(e1, e2):
    a1r, a1i, b1r, b1i = e1
    a2r, a2i, b2r, b2i = e2
    return (a2r * a1r - a2i * a1i, a2r * a1i + a2i * a1r,
            a2r * b1r - a2i * b1i + b2r, a2r * b1i + a2i * b1r + b2i)


def _s5_branch(u, s0_re, s0_im, a_re, a_im, log_dt, b_re, b_im, c_re, c_im, d):
    bsz, t = u.shape[:2]
    dt = jnp.exp(log_dt)[:, None]
    mag = jnp.exp(a_re * dt)
    lb_re, lb_im = mag * jnp.cos(a_im * dt), mag * jnp.sin(a_im * dt)
    den = a_re * a_re + a_im * a_im
    co_re = ((lb_re - 1.0) * a_re + lb_im * a_im) / den
    co_im = (lb_im * a_re - (lb_re - 1.0) * a_im) / den
    bb_re = co_re[..., None] * b_re - co_im[..., None] * b_im
    bb_im = co_re[..., None] * b_im + co_im[..., None] * b_re
    ug = u.reshape(bsz, t, SSM_GROUPS, SSM_GROUP)
    bu_re = jnp.einsum('btgp,gnp->btgn', ug, bb_re)
    bu_im = jnp.einsum('btgp,gnp->btgn', ug, bb_im)
    bu_re = bu_re.at[:, 0].add(lb_re * s0_re - lb_im * s0_im)
    bu_im = bu_im.at[:, 0].add(lb_re * s0_im + lb_im * s0_re)
    la_re = jnp.broadcast_to(lb_re, bu_re.shape)
    la_im = jnp.broadcast_to(lb_im, bu_im.shape)
    _, _, s_re, s_im = lax.associative_scan(_cplx_affine, (la_re, la_im, bu_re, bu_im), axis=1)
    y = (jnp.einsum('btgn,gpn->btgp', s_re, c_re) - jnp.einsum('btgn,gpn->btgp', s_im, c_im) + d * ug)
    return y.reshape(bsz, t, SSM_WIDTH), s_re[:, -1], s_im[:, -1]


def _sb_core(q, k, v, qpos, kpos):
    z = jnp.einsum('bqhd,bshd->bhqs', q, k) / math.sqrt(HEAD_DIM)
    mask = kpos[None, :] < qpos[:, None]
    log_1m = jnp.where(mask, jax.nn.log_sigmoid(-z), 0.0)
    between = lax.cumsum(log_1m, axis=3, reverse=True) - log_1m
    w = jnp.where(mask, jnp.exp(jax.nn.log_sigmoid(z) + between), 0.0)
    return jnp.einsum('bhqs,bshd->bqhd', w, v)


def _compress(rows, pe, w):
    bsz, length = rows.shape[:2]
    r = CMP_LEN // CMP_STRIDE
    n_chunks = length // CMP_STRIDE
    nc = n_chunks - r + 1
    chunks = rows[:, :n_chunks * CMP_STRIDE].reshape(bsz, n_chunks, CMP_STRIDE, NSA_KV_HEADS, HEAD_DIM)
    wj = w.reshape(r, CMP_STRIDE, HEAD_DIM, HEAD_DIM)
    pj = pe.reshape(r, CMP_STRIDE, HEAD_DIM)
    out = 0.0
    for j in range(r):
        out = out + jnp.einsum('bnsgd,sde->bnge', chunks[:, j:j + nc], wj[j]) + jnp.einsum('sd,sde->e', pj[j], wj[j])
    cend = jnp.arange(nc) * CMP_STRIDE + CMP_LEN - 1
    return out, cend


def _cmp_to_sel(nc, ns):
    cs = np.arange(nc) * CMP_STRIDE
    ss = np.arange(ns) * SEL_LEN
    ov = np.clip(np.minimum(cs[:, None] + CMP_LEN, ss[None, :] + SEL_LEN) - np.maximum(cs[:, None], ss[None, :]), 0, None)
    return jnp.asarray(ov / CMP_LEN, dtype=F32)


def _pad_blocks(a):
    pad = (-a.shape[1]) % SEL_LEN
    return jnp.pad(a, ((0, 0), (0, pad), (0, 0), (0, 0)))


def _nsa_prepare(k_cmp, v_cmp, k_sel, v_sel, w_cmp, pe_cmp):
    kc, cend = _compress(k_cmp, pe_cmp[0], w_cmp[0])
    vc, _ = _compress(v_cmp, pe_cmp[1], w_cmp[1])
    ks, vs = _pad_blocks(k_sel), _pad_blocks(v_sel)
    cmap = _cmp_to_sel(kc.shape[1], ks.shape[1] // SEL_LEN)
    return kc, vc, cend, cmap, ks, vs


def _nsa_core(qr, qn, g, qpos, kc, vc, cend, cmap, ks, vs, kw, vw, kwpos):
    bsz, nq = qr.shape[:2]
    qr = qr.reshape(bsz, nq, NSA_KV_HEADS, NSA_REP, HEAD_DIM)
    qn = qn.reshape(bsz, nq, NSA_KV_HEADS, NSA_REP, HEAD_DIM)
    mc = cend[None, :] <= qpos[:, None]
    sc = jnp.einsum('bqgrd,bcgd->bgrqc', qn, kc)
    pc = jax.nn.softmax(jnp.where(mc, sc, NEG), axis=-1) * mc
    o_cmp = jnp.einsum('bgrqc,bcgd->bqgrd', pc, vc)
    ns = cmap.shape[1]
    topk = min(SEL_TOPK, ns)
    imp = jnp.einsum('bgrqc,cn->bgqn', pc, cmap)
    blk = jnp.arange(ns)[None, :]
    cur = (qpos // SEL_LEN)[:, None]
    imp = jnp.where(blk == cur, FORCE, jnp.where(blk < cur, imp, -1.0))
    _, idx = lax.top_k(imp, topk)
    ksb = ks.reshape(bsz, ns, SEL_LEN, NSA_KV_HEADS, HEAD_DIM).transpose(0, 3, 1, 2, 4)
    vsb = vs.reshape(bsz, ns, SEL_LEN, NSA_KV_HEADS, HEAD_DIM).transpose(0, 3, 1, 2, 4)
    bi = jnp.arange(bsz)[:, None, None, None]
    gi = jnp.arange(NSA_KV_HEADS)[None, :, None, None]
    kg, vg = ksb[bi, gi, idx], vsb[bi, gi, idx]
    spos = idx[..., None] * SEL_LEN + jnp.arange(SEL_LEN)
    ms = spos <= qpos[None, None, :, None, None]
    ss = jnp.einsum('bqgrd,bgqkld->bgrqkl', qr, kg)
    ss = jnp.where(ms[:, :, None], ss, NEG).reshape(bsz, NSA_KV_HEADS, NSA_REP, nq, topk * SEL_LEN)
    ps = jax.nn.softmax(ss, axis=-1).reshape(bsz, NSA_KV_HEADS, NSA_REP, nq, topk, SEL_LEN)
    o_sel = jnp.einsum('bgrqkl,bgqkld->bqgrd', ps, vg)
    mw = (kwpos[None, :] <= qpos[:, None]) & (kwpos[None, :] > qpos[:, None] - WINDOW) & (kwpos[None, :] >= 0)
    sw = jnp.einsum('bqgrd,bwgd->bgrqw', qr, kw)
    pw = jax.nn.softmax(jnp.where(mw, sw, NEG), axis=-1)
    o_win = jnp.einsum('bgrqw,bwgd->bqgrd', pw, vw)
    g = g.reshape(bsz, nq, NSA_KV_HEADS, NSA_REP, 3)
    o = g[..., 0:1] * o_cmp + g[..., 1:2] * o_sel + g[..., 2:3] * o_win
    return o.reshape(bsz, nq, NSA_WIDTH)


def _rope_tables(pos):
    half = HEAD_DIM // 2
    freqs = ROPE_THETA ** (-jnp.arange(half, dtype=F32) / half)
    ang = pos.astype(F32)[:, None] * freqs
    cos, sin = jnp.cos(ang), jnp.sin(ang)
    cos2 = jnp.concatenate([cos, cos], axis=-1)
    sin2 = jnp.concatenate([-sin, sin], axis=-1)
    return jnp.tile(cos2, (1, LANES // HEAD_DIM)), jnp.tile(sin2, (1, LANES // HEAD_DIM))


def _pack_w_in(w):
    pad = jnp.zeros((D_MODEL, LANES - 3 * NSA_HEADS), w.dtype)
    return jnp.concatenate([w[:, :2304], w[:, 2304:2328], pad, w[:, 2328:]], axis=1).astype(BF16)


def kernel(x_prompt, x_sample, cache_sb, cache_nsa, state_win, state_ssm, page_table, c_prompt, c_sample,
           w_ada, b_ada, norm_mix, norm_ffn, w_in, ssm_a_re, ssm_a_im, ssm_log_dt, ssm_b_re, ssm_b_im,
           ssm_c_re, ssm_c_im, ssm_d, w_glu, b_glu, nsa_w_cmp, nsa_pe_cmp, w_branch, w_out,
           w_grp, b_grp, w_rt, b_rt, w_e_gate, w_e_up, w_e_down, final_norm):
    bp, t = x_prompt.shape[:2]
    bd, s = x_sample.shape[:2]
    np_tok = bp * t
    win_len = state_win.shape[2]
    tm_p = 256
    tiles_pb = t // tm_p
    tm_f = 1024
    tiles_fb = t // tm_f

    xp = x_prompt.reshape(np_tok, D_MODEL)
    xs = x_sample.reshape(bd * s, D_MODEL)
    cos_p, sin_p = _rope_tables(jnp.arange(t))
    cos_s, sin_s = _rope_tables(jnp.full((bd,), PAST_LEN))
    pos_s = PAST_LEN + jnp.arange(s)
    nsa_consts = _nsa_consts(t)
    dec_cmap2, dec_bige = _dec_consts()
    n_pool = cache_sb.shape[1]
    cache_sb2d = _pages_t(cache_sb)
    cache_nsa2d = _pages_t(cache_nsa)
    win_t = state_win.transpose(0, 1, 3, 4, 5, 2).reshape(DEPTH, bd, 2 * NSA_KV_WIDTH, win_len)

    sb_p, sb_s, nsa_p, nsa_s, win_p, win_s, ssm_p, ssm_s = [], [], [], [], [], [], [], []
    for l in range(DEPTH):
        w_in_l = _pack_w_in(w_in[l])
        wb_l = w_branch[l].astype(BF16)
        wo_l = w_out[l].astype(BF16)
        wr_l = jnp.concatenate([w_grp[l], w_rt[l], jnp.zeros((D_MODEL, LANES - 20), F32)], axis=1)
        br_l = jnp.concatenate([b_grp[l], b_rt[l], jnp.zeros((LANES - 20,), F32)])[None, :]
        def by_group(w):
            return w.astype(BF16).reshape(MOE_GROUPS, EXPERTS_PER_GROUP, D_MODEL, EXPERT_FF)

        wg_l, wu_l = by_group(w_e_gate[l]), by_group(w_e_up[l])
        wd_l = w_e_down[l].astype(BF16).reshape(MOE_GROUPS, EXPERTS_PER_GROUP * EXPERT_FF, D_MODEL)
        gn_mix = norm_mix[l][None, :]
        gn_ffn = norm_ffn[l][None, :]
        ssm_l = (ssm_a_re[l], ssm_a_im[l], ssm_log_dt[l], ssm_b_re[l], ssm_b_im[l], ssm_c_re[l], ssm_c_im[l],
                 ssm_d[l])
        wglu_l = w_glu[l].astype(BF16)
        bglu_l = b_glu[l][None, :]

        mp = jnp.split(jax.nn.silu(c_prompt) @ w_ada[l] + b_ada[l], 6, axis=-1)
        mp = [m[:, None, :] for m in mp]
        u, sq, skv, qn, qr, nsa, win, ng, g = _inproj(
            xp, mp[0], mp[1], gn_mix, cos_p, sin_p, w_in_l, tm_p,
            lambda i: i // tiles_pb, lambda i: i % tiles_pb)
        y_a, sf = _s5_prompt(u.reshape(bp, t, SSM_WIDTH), _s5_mats(*ssm_l[:7]), ssm_d[l])
        y_b = _sb_prompt(sq, skv, bp, t)
        wbig, pe8 = _cmp_weights(nsa_w_cmp[l], nsa_pe_cmp[l])
        y_c = _nsa_prompt(qn, qr, nsa, win, ng, wbig, pe8, nsa_consts, bp, t)
        xp = _merge(xp, mp[2], y_a.reshape(np_tok, SSM_WIDTH), y_b, y_c, g, wglu_l, bglu_l, wb_l, wo_l,
                    tm_p, lambda i: i // tiles_pb)
        xp = _ffn(xp, mp[3], mp[4], mp[5], gn_ffn, wr_l, br_l, wg_l, wu_l, wd_l, tm_f, lambda i: i // tiles_fb)
        sb_p.append(skv.reshape(bp, t, 2, SB_HEADS, HEAD_DIM))
        nsa_p.append(nsa.reshape(bp, t, 4, NSA_KV_HEADS, HEAD_DIM))
        win_p.append(win.reshape(bp, t, 2, NSA_KV_HEADS, HEAD_DIM)[:, t - min(WINDOW, t):])
        ssm_p.append(sf)

        ms = jnp.split(jax.nn.silu(c_sample) @ w_ada[l] + b_ada[l], 6, axis=-1)
        ms = [m[None, :, :] for m in ms]
        u, sq, skv, qn, qr, nsa, win, ng, g = _inproj(
            xs, ms[0], ms[1], gn_mix, cos_s, sin_s, w_in_l, bd, lambda i: 0, lambda i: 0)
        y_a, sr, si = _s5_branch(u.reshape(bd, s, SSM_WIDTH), state_ssm[l][:, 0], state_ssm[l][:, 1], *ssm_l)
        new_sb = skv.reshape(bd, s, 2, SB_HEADS, HEAD_DIM)
        rows_l = page_table + l * n_pool
        y_b = _dec_sb(rows_l, sq, cache_sb2d)
        new_nsa = nsa.reshape(bd, s, 4, NSA_KV_HEADS, HEAD_DIM)
        ocmp, imp = _dec_cmp(rows_l, qn, cache_nsa2d, wbig, pe8, dec_cmap2)
        unsel = _dec_topk(imp.reshape(bd * NSA_HEADS, LANES)).reshape(bd, NSA_HEADS, LANES)
        new_win = win.reshape(bd, s, 2, NSA_KV_HEADS, HEAD_DIM)
        winc = jnp.concatenate([state_win[l], new_win], axis=1)
        y_c = _dec_sel(rows_l, qr, unsel, ocmp, ng, nsa, win_t[l], win, cache_nsa2d, dec_bige)
        xs = _merge(xs, ms[2], y_a.reshape(bd * s, SSM_WIDTH), y_b, y_c, g, wglu_l, bglu_l, wb_l, wo_l,
                    bd, lambda i: 0)
        xs = _ffn(xs, ms[3], ms[4], ms[5], gn_ffn, wr_l, br_l, wg_l, wu_l, wd_l, bd, lambda i: 0)
        sb_s.append(new_sb)
        nsa_s.append(new_nsa)
        win_s.append(winc[:, s:])
        ssm_s.append(jnp.stack([sr, si], axis=1))

    fn = final_norm[None, :]
    y_prompt = _final_norm(xp, fn, tm_f).reshape(bp, t, D_MODEL)
    y_sample = _final_norm(xs, fn, bd).reshape(bd, s, D_MODEL)
    return (y_prompt, y_sample,
            jnp.stack(sb_p), jnp.stack(sb_s),
            jnp.stack(nsa_p), jnp.stack(nsa_s),
            jnp.stack(win_p), jnp.stack(win_s),
            jnp.stack(ssm_p), jnp.stack(ssm_s))
```

```python
import functools
import math

import jax
import jax.numpy as jnp
import numpy as np
from jax import lax
from jax.experimental import pallas as pl
from jax.experimental.pallas import tpu as pltpu

F32 = jnp.float32
BF16 = jnp.bfloat16

D_MODEL = 1024
DEPTH = 4
PAST_LEN = 2048
HEAD_DIM = 64
SSM_WIDTH = 256
SSM_GROUP = 16
SSM_GROUPS = 16
SSM_STATE = 64
SB_HEADS = 4
SB_WIDTH = 256
NSA_HEADS = 8
NSA_KV_HEADS = 2
NSA_REP = 4
NSA_WIDTH = 512
NSA_KV_WIDTH = 128
CMP_LEN = 32
CMP_STRIDE = 16
SEL_LEN = 64
SEL_TOPK = 16
WINDOW = 512
N_BRANCH = 3
Q_BLOCK = 128
ROPE_THETA = 10000.0
MOE_GROUPS = 4
EXPERTS_PER_GROUP = 4
N_EXPERTS = 16
EXPERT_FF = 256
EPS = 1e-6
NEG = -1e30
FORCE = 1e9

LANES = 128
VMEM_LIMIT = 56 * 1024 * 1024

_C_U = 0
_C_SQ = 256
_C_SKV = 512
_C_NQ = 1024
_C_NKV = 1536
_C_NG = 2304
_C_MG = 2432
_C_END = 5504


def _cparams(sem):
    return pltpu.CompilerParams(dimension_semantics=sem, vmem_limit_bytes=VMEM_LIMIT)


def _rope_slab(v, cos, sin_signed):
    lane = lax.broadcasted_iota(jnp.int32, v.shape, 1)
    first = (lane % HEAD_DIM) < (HEAD_DIM // 2)
    swapped = jnp.where(first, pltpu.roll(v, LANES - HEAD_DIM // 2, 1), pltpu.roll(v, HEAD_DIM // 2, 1))
    return v * cos + swapped * sin_signed


def _inproj_kernel(x_ref, shift_ref, scale_ref, gn_ref, cos_ref, sin_ref, w_ref,
                   u_ref, sq_ref, skv_ref, qn_ref, qr_ref, nsa_ref, win_ref, ng_ref, g_ref):
    x = x_ref[...]
    ms = jnp.mean(x * x, axis=-1, keepdims=True)
    h = x * lax.rsqrt(ms + EPS) * gn_ref[...]
    h = h * (1.0 + scale_ref[0]) + shift_ref[0]
    hb = h.astype(BF16)

    def mm(lo, hi):
        return jnp.dot(hb, w_ref[:, lo:hi], preferred_element_type=F32)

    cos = cos_ref[...]
    sin = sin_ref[...]
    u_ref[...] = mm(_C_U, _C_SQ)
    sq_ref[...] = mm(_C_SQ, _C_SKV)
    skv_ref[...] = mm(_C_SKV, _C_NQ)
    q = mm(_C_NQ, _C_NKV) * (1.0 / math.sqrt(HEAD_DIM))
    qn_ref[...] = q
    for s in range(NSA_WIDTH // LANES):
        qr_ref[:, s * LANES:(s + 1) * LANES] = _rope_slab(q[:, s * LANES:(s + 1) * LANES], cos, sin)
    kv = mm(_C_NKV, _C_NG)
    nsa_ref[:, 0:256] = kv[:, 0:256]
    nsa_ref[:, 256:384] = _rope_slab(kv[:, 256:384], cos, sin)
    nsa_ref[:, 384:512] = kv[:, 384:512]
    win_ref[:, 0:128] = _rope_slab(kv[:, 512:640], cos, sin)
    win_ref[:, 128:256] = kv[:, 640:768]
    ng_ref[...] = jax.nn.sigmoid(mm(_C_NG, _C_MG))
    g_ref[...] = jax.nn.sigmoid(mm(_C_MG, _C_END)).astype(g_ref.dtype)


def _inproj(x, shift, scale, gn, cos, sin, w, tm, mod_map, pos_map):
    n = x.shape[0]
    mrows = shift.shape[1]
    row = lambda width: pl.BlockSpec((tm, width), lambda i: (i, 0))
    widths = (256, 256, 512, 512, 512, 512, 256, 128, 3072)
    return pl.pallas_call(
        _inproj_kernel,
        grid=(n // tm,),
        in_specs=[
            row(D_MODEL),
            pl.BlockSpec((1, mrows, D_MODEL), lambda i: (mod_map(i), 0, 0)),
            pl.BlockSpec((1, mrows, D_MODEL), lambda i: (mod_map(i), 0, 0)),
            pl.BlockSpec((1, D_MODEL), lambda i: (0, 0)),
            pl.BlockSpec((tm, LANES), lambda i: (pos_map(i), 0)),
            pl.BlockSpec((tm, LANES), lambda i: (pos_map(i), 0)),
            pl.BlockSpec((D_MODEL, _C_END), lambda i: (0, 0)),
        ],
        out_specs=[row(wd) for wd in widths],
        out_shape=[jax.ShapeDtypeStruct((n, wd), BF16 if k == len(widths) - 1 else F32)
                   for k, wd in enumerate(widths)],
        compiler_params=_cparams(("parallel",)),
        name="inproj",
    )(x, shift, scale, gn, cos, sin, w)


def _merge_kernel(x_ref, gate_ref, ya_ref, yb_ref, yc_ref, g_ref, wglu_ref, bglu_ref, wb_ref, wo_ref, o_ref):
    def mm(a, w):
        return jnp.dot(a.astype(BF16), w, preferred_element_type=F32)

    ya = jax.nn.gelu(ya_ref[...])
    ya = ya * jax.nn.sigmoid(mm(ya, wglu_ref[...]) + bglu_ref[...])
    br_a = mm(ya, wb_ref[0:SSM_WIDTH, :])
    br_b = mm(yb_ref[...], wb_ref[SSM_WIDTH:SSM_WIDTH + SB_WIDTH, :])
    br_c = mm(yc_ref[...], wb_ref[SSM_WIDTH + SB_WIDTH:, :])
    merged = (g_ref[:, 0:D_MODEL] * br_a + g_ref[:, D_MODEL:2 * D_MODEL] * br_b
              + g_ref[:, 2 * D_MODEL:] * br_c)
    out = mm(merged, wo_ref[...])
    o_ref[...] = x_ref[...] + gate_ref[0] * out


def _merge(x, gate, ya, yb, yc, g, wglu, bglu, wb, wo, tm, mod_map):
    n = x.shape[0]
    mrows = gate.shape[1]
    row = lambda width: pl.BlockSpec((tm, width), lambda i: (i, 0))
    return pl.pallas_call(
        _merge_kernel,
        grid=(n // tm,),
        in_specs=[
            row(D_MODEL),
            pl.BlockSpec((1, mrows, D_MODEL), lambda i: (mod_map(i), 0, 0)),
            row(SSM_WIDTH), row(SB_WIDTH), row(NSA_WIDTH), row(N_BRANCH * D_MODEL),
            pl.BlockSpec((SSM_WIDTH, SSM_WIDTH), lambda i: (0, 0)),
            pl.BlockSpec((1, SSM_WIDTH), lambda i: (0, 0)),
            pl.BlockSpec((D_MODEL, D_MODEL), lambda i: (0, 0)),
            pl.BlockSpec((D_MODEL, D_MODEL), lambda i: (0, 0)),
        ],
        out_specs=row(D_MODEL),
        out_shape=jax.ShapeDtypeStruct((n, D_MODEL), F32),
        compiler_params=_cparams(("parallel",)),
        name="merge",
    )(x, gate, ya, yb, yc, g, wglu, bglu, wb, wo)


def _route(logits):
    lane = lax.broadcasted_iota(jnp.int32, logits.shape, 1)
    big = jnp.int32(1 << 20)
    is_g = lane < MOE_GROUPS
    gl = jnp.where(is_g, logits, NEG)
    gmax = jnp.max(gl, axis=-1, keepdims=True)
    p_sel = 1.0 / jnp.sum(jnp.where(is_g, jnp.exp(gl - gmax), 0.0), axis=-1, keepdims=True)
    g_sel = jnp.min(jnp.where(is_g & (gl == gmax), lane, big), axis=-1, keepdims=True)
    e_idx = lane - MOE_GROUPS
    in_grp = (e_idx >= g_sel * EXPERTS_PER_GROUP) & (e_idx < (g_sel + 1) * EXPERTS_PER_GROUP)
    el = jnp.where(in_grp, logits, NEG)
    m1 = jnp.max(el, axis=-1, keepdims=True)
    i1 = jnp.min(jnp.where(in_grp & (el == m1), lane, big), axis=-1, keepdims=True)
    el2 = jnp.where(lane == i1, NEG, el)
    m2 = jnp.max(el2, axis=-1, keepdims=True)
    i2 = jnp.min(jnp.where(in_grp & (lane != i1) & (el2 == m2), lane, big), axis=-1, keepdims=True)
    e2 = jnp.exp(m2 - m1)
    w1 = p_sel / (1.0 + e2)
    w2 = p_sel * e2 / (1.0 + e2)
    return jnp.where(lane == i1, w1, jnp.where(lane == i2, w2, 0.0))


def _ffn_kernel(x_ref, shift_ref, scale_ref, gate_ref, gn_ref, wr_ref, br_ref, wg_ref, wu_ref, wd_ref,
                o_ref, h_scr, comb_scr, acc_scr):
    e = pl.program_id(1)

    @pl.when(e == 0)
    def _():
        x = x_ref[...]
        ms = jnp.mean(x * x, axis=-1, keepdims=True)
        h = x * lax.rsqrt(ms + EPS) * gn_ref[...]
        h = h * (1.0 + scale_ref[0]) + shift_ref[0]
        logits = jnp.dot(h, wr_ref[...], preferred_element_type=F32,
                         precision=lax.Precision.HIGHEST) + br_ref[...]
        comb_scr[...] = _route(logits)
        h_scr[...] = h.astype(BF16)
        acc_scr[...] = jnp.zeros_like(acc_scr)

    hb = h_scr[...]
    a = jnp.concatenate([jnp.dot(hb, wg_ref[0, j], preferred_element_type=F32)
                         for j in range(EXPERTS_PER_GROUP)], axis=1)
    b = jnp.concatenate([jnp.dot(hb, wu_ref[0, j], preferred_element_type=F32)
                         for j in range(EXPERTS_PER_GROUP)], axis=1)
    comb = comb_scr[...]
    lane = lax.broadcasted_iota(jnp.int32, comb.shape, 1)
    tm = comb.shape[0]
    c = jnp.concatenate(
        [jnp.broadcast_to(jnp.sum(jnp.where(lane == MOE_GROUPS + e * EXPERTS_PER_GROUP + j, comb, 0.0),
                                  axis=-1, keepdims=True), (tm, EXPERT_FF))
         for j in range(EXPERTS_PER_GROUP)], axis=1)
    hid = (a * jax.nn.sigmoid(a)) * b * c
    acc_scr[...] += jnp.dot(hid.astype(BF16), wd_ref[0], preferred_element_type=F32)

    @pl.when(e == MOE_GROUPS - 1)
    def _():
        o_ref[...] = x_ref[...] + gate_ref[0] * acc_scr[...]


def _ffn(x, shift, scale, gate, gn, wr, br, wg, wu, wd, tm, mod_map):
    n = x.shape[0]
    mrows = shift.shape[1]
    mod = pl.BlockSpec((1, mrows, D_MODEL), lambda i, e: (mod_map(i), 0, 0))
    gff = EXPERTS_PER_GROUP * EXPERT_FF
    wspec = pl.BlockSpec((1, EXPERTS_PER_GROUP, D_MODEL, EXPERT_FF), lambda i, e: (e, 0, 0, 0))
    return pl.pallas_call(
        _ffn_kernel,
        grid=(n // tm, MOE_GROUPS),
        in_specs=[
            pl.BlockSpec((tm, D_MODEL), lambda i, e: (i, 0)),
            mod, mod, mod,
            pl.BlockSpec((1, D_MODEL), lambda i, e: (0, 0)),
            pl.BlockSpec((D_MODEL, LANES), lambda i, e: (0, 0)),
            pl.BlockSpec((1, LANES), lambda i, e: (0, 0)),
            wspec, wspec,
            pl.BlockSpec((1, gff, D_MODEL), lambda i, e: (e, 0, 0)),
        ],
        out_specs=pl.BlockSpec((tm, D_MODEL), lambda i, e: (i, 0)),
        out_shape=jax.ShapeDtypeStruct((n, D_MODEL), F32),
        scratch_shapes=[pltpu.VMEM((tm, D_MODEL), BF16), pltpu.VMEM((tm, LANES), F32),
                        pltpu.VMEM((tm, D_MODEL), F32)],
        compiler_params=_cparams(("parallel", "arbitrary")),
        name="ffn_moe",
    )(x, shift, scale, gate, gn, wr, br, wg, wu, wd)


def _final_norm_kernel(x_ref, g_ref, o_ref):
    x = x_ref[...]
    o_ref[...] = x * lax.rsqrt(jnp.mean(x * x, axis=-1, keepdims=True) + EPS) * g_ref[...]


def _final_norm(x, g, tm):
    n = x.shape[0]
    return pl.pallas_call(
        _final_norm_kernel,
        grid=(n // tm,),
        in_specs=[pl.BlockSpec((tm, D_MODEL), lambda i: (i, 0)), pl.BlockSpec((1, D_MODEL), lambda i: (0, 0))],
        out_specs=pl.BlockSpec((tm, D_MODEL), lambda i: (i, 0)),
        out_shape=jax.ShapeDtypeStruct((n, D_MODEL), F32),
        compiler_params=_cparams(("parallel",)),
        name="final_norm",
    )(x, g)


S5_CHUNK = 64
S5_LW = S5_CHUNK * SSM_GROUP


def _s5_mats(a_re, a_im, log_dt, b_re, b_im, c_re, c_im):
    L = S5_CHUNK
    hp = lax.Precision.HIGHEST
    dt = jnp.exp(log_dt)[:, None]
    k = jnp.arange(L + 1, dtype=F32)[:, None, None]
    mag = jnp.exp(a_re * dt * k)
    ang = a_im * dt * k
    lk_re, lk_im = mag * jnp.cos(ang), mag * jnp.sin(ang)
    lb_re, lb_im = lk_re[1], lk_im[1]
    den = a_re * a_re + a_im * a_im
    co_re = ((lb_re - 1.0) * a_re + lb_im * a_im) / den
    co_im = (lb_im * a_re - (lb_re - 1.0) * a_im) / den
    bb_re = co_re[..., None] * b_re - co_im[..., None] * b_im
    bb_im = co_re[..., None] * b_im + co_im[..., None] * b_re
    cl_re = c_re[None] * lk_re[:, :, None, :] - c_im[None] * lk_im[:, :, None, :]
    cl_im = c_re[None] * lk_im[:, :, None, :] + c_im[None] * lk_re[:, :, None, :]
    kk = (jnp.einsum('kgpn,gnq->kgpq', cl_re[:L], bb_re, precision=hp)
          - jnp.einsum('kgpn,gnq->kgpq', cl_im[:L], bb_im, precision=hp))
    kq = kk.transpose(1, 3, 0, 2).reshape(SSM_GROUPS, SSM_GROUP, S5_LW).astype(BF16)
    kp = jnp.concatenate([jnp.zeros_like(kq), kq], axis=-1)
    tmat = jnp.stack([kp[:, :, S5_LW - SSM_GROUP * i:2 * S5_LW - SSM_GROUP * i] for i in range(L)], axis=1)
    tmat = tmat.reshape(SSM_GROUPS, S5_LW, S5_LW)
    rev = L - 1 - jnp.arange(L)
    be_re = lk_re[rev][:, :, :, None] * bb_re[None] - lk_im[rev][:, :, :, None] * bb_im[None]
    be_im = lk_re[rev][:, :, :, None] * bb_im[None] + lk_im[rev][:, :, :, None] * bb_re[None]
    bmat = jnp.concatenate([be_re.transpose(1, 0, 3, 2).reshape(SSM_GROUPS, S5_LW, SSM_STATE),
                            be_im.transpose(1, 0, 3, 2).reshape(SSM_GROUPS, S5_LW, SSM_STATE)], axis=-1)
    cm_re = cl_re[1:].transpose(1, 3, 0, 2).reshape(SSM_GROUPS, SSM_STATE, S5_LW)
    cm_im = cl_im[1:].transpose(1, 3, 0, 2).reshape(SSM_GROUPS, SSM_STATE, S5_LW)
    cmat = jnp.concatenate([cm_re, -cm_im], axis=1)
    lbl = jnp.stack([lk_re[L], lk_im[L]], axis=1)
    return tmat.astype(BF16), bmat.astype(BF16), cmat.astype(BF16), lbl


def _s5_kernel(x_ref, t_ref, b_ref, c_ref, lbl_ref, d_ref, y_ref, sf_ref, er_scr, ei_scr, pr_scr, pi_scr):
    nb, _, nch, _ = x_ref.shape
    a_re = lbl_ref[0, 0:1, :]
    a_im = lbl_ref[0, 1:2, :]
    for b in range(nb):
        x = x_ref[b, 0]
        xb = x.astype(BF16)
        e = jnp.dot(xb, b_ref[0], preferred_element_type=F32)
        er_scr[...] = e[:, :SSM_STATE]
        ei_scr[...] = e[:, SSM_STATE:]

        def step(c, carry):
            s_re, s_im = carry
            pr_scr[pl.ds(c, 1), :] = s_re
            pi_scr[pl.ds(c, 1), :] = s_im
            n_re = a_re * s_re - a_im * s_im + er_scr[pl.ds(c, 1), :]
            n_im = a_re * s_im + a_im * s_re + ei_scr[pl.ds(c, 1), :]
            return n_re, n_im

        zero = jnp.zeros((1, SSM_STATE), F32)
        s_re, s_im = lax.fori_loop(0, nch, step, (zero, zero))
        sprev = jnp.concatenate([pr_scr[...], pi_scr[...]], axis=1).astype(BF16)
        y = (jnp.dot(xb, t_ref[0], preferred_element_type=F32)
             + jnp.dot(sprev, c_ref[0], preferred_element_type=F32) + d_ref[0] * x)
        y_ref[b, 0] = y
        sf_ref[b, 0] = jnp.concatenate([s_re, s_im], axis=1)


def _s5_prompt(u, mats, d):
    tmat, bmat, cmat, lbl = mats
    bsz, t = u.shape[:2]
    nch = t // S5_CHUNK
    x = u.reshape(bsz, nch, S5_CHUNK, SSM_GROUPS, SSM_GROUP).transpose(0, 3, 1, 2, 4).reshape(
        bsz, SSM_GROUPS, nch, S5_LW)
    dt = jnp.tile(d, (1, S5_CHUNK))[:, None, :]
    y, sf = pl.pallas_call(
        _s5_kernel,
        grid=(SSM_GROUPS,),
        in_specs=[
            pl.BlockSpec((bsz, 1, nch, S5_LW), lambda g: (0, g, 0, 0)),
            pl.BlockSpec((1, S5_LW, S5_LW), lambda g: (g, 0, 0)),
            pl.BlockSpec((1, S5_LW, 2 * SSM_STATE), lambda g: (g, 0, 0)),
            pl.BlockSpec((1, 2 * SSM_STATE, S5_LW), lambda g: (g, 0, 0)),
            pl.BlockSpec((1, 2, SSM_STATE), lambda g: (g, 0, 0)),
            pl.BlockSpec((1, 1, S5_LW), lambda g: (g, 0, 0)),
        ],
        out_specs=[pl.BlockSpec((bsz, 1, nch, S5_LW), lambda g: (0, g, 0, 0)),
                   pl.BlockSpec((bsz, 1, 1, 2 * SSM_STATE), lambda g: (0, g, 0, 0))],
        out_shape=[jax.ShapeDtypeStruct((bsz, SSM_GROUPS, nch, S5_LW), F32),
                   jax.ShapeDtypeStruct((bsz, SSM_GROUPS, 1, 2 * SSM_STATE), F32)],
        scratch_shapes=[pltpu.VMEM((nch, SSM_STATE), F32)] * 4,
        compiler_params=_cparams(("arbitrary",)),
        name="s5_prompt",
    )(x, tmat, bmat, cmat, lbl, dt)
    y = y.reshape(bsz, SSM_GROUPS, nch, S5_CHUNK, SSM_GROUP).transpose(0, 2, 3, 1, 4).reshape(bsz, t, SSM_WIDTH)
    sf = sf.reshape(bsz, SSM_GROUPS, 2, SSM_STATE).transpose(0, 2, 1, 3)
    return y, sf


SB_TK = 128
SB_TQ = 256
SB_PER = SB_TQ // SB_TK


def _sb_tri():
    j = np.arange(SB_TK)[:, None]
    s = np.arange(SB_TK)[None, :]
    return jnp.asarray(-(s > j).astype(np.float32), dtype=BF16)


def _sb_kernel(qt_ref, k_ref, vt_ref, tri_ref, o_ref, acc_scr, c_scr):
    qi = pl.program_id(1)
    tri = tri_ref[...]
    kidx = lax.broadcasted_iota(jnp.int32, (SB_TK, SB_TQ), 0)
    qidx = lax.broadcasted_iota(jnp.int32, (SB_TK, SB_TQ), 1)

    def softplus_tile(h, j, mask):
        kt = k_ref[0, h, pl.ds(pl.multiple_of(j * SB_TK, SB_TK), SB_TK), :]
        z = jnp.dot(kt, qt_ref[0, h], preferred_element_type=F32)
        sp = jnp.maximum(z, 0.0) + jnp.log(1.0 + jnp.exp(-jnp.abs(z)))
        if mask is not None:
            sp = jnp.where(mask, sp, 0.0)
        return z, sp

    def weights(z, sp, c, mask):
        r = jnp.dot(tri, sp.astype(BF16), preferred_element_type=F32)
        w = jnp.exp(z - sp + r - c)
        if mask is not None:
            w = jnp.where(mask, w, 0.0)
        return w.astype(BF16)

    def two_tiles(j_hi, masks):
        heads = range(SB_HEADS)
        zs1 = [softplus_tile(h, j_hi, masks[0]) for h in heads]
        zs0 = [softplus_tile(h, j_hi - 1, masks[1]) for h in heads]
        cs = [c_scr[h] for h in heads]
        t1 = [jnp.sum(zs1[h][1], axis=0, keepdims=True) for h in heads]
        w1 = [weights(zs1[h][0], zs1[h][1], cs[h], masks[0]) for h in heads]
        w0 = [weights(zs0[h][0], zs0[h][1], cs[h] + t1[h], masks[1]) for h in heads]
        for h in heads:
            acc_scr[h] += (jnp.dot(vt_ref[0, h, j_hi], w1[h], preferred_element_type=F32)
                           + jnp.dot(vt_ref[0, h, j_hi - 1], w0[h], preferred_element_type=F32))
            c_scr[h] = cs[h] + t1[h] + jnp.sum(zs0[h][1], axis=0, keepdims=True)

    acc_scr[...] = jnp.zeros_like(acc_scr)
    c_scr[...] = jnp.zeros_like(c_scr)
    two_tiles(SB_PER * qi + 1, (kidx + SB_TK < qidx, kidx < qidx))

    def body(p, carry):
        two_tiles(SB_PER * (qi - 1 - p) + 1, (None, None))
        return carry

    lax.fori_loop(0, qi, body, 0)
    o_ref[0] = acc_scr[...]


def _sb_prompt(sq, skv, bsz, t):
    assert SB_PER == 2
    nt = t // SB_TK
    nq = t // SB_TQ
    qt = (sq * (1.0 / math.sqrt(HEAD_DIM))).reshape(bsz, t, SB_HEADS, HEAD_DIM).transpose(0, 2, 3, 1).astype(BF16)
    k = skv[:, :SB_WIDTH].reshape(bsz, t, SB_HEADS, HEAD_DIM).transpose(0, 2, 1, 3).astype(BF16)
    vt = skv[:, SB_WIDTH:].reshape(bsz, nt, SB_TK, SB_HEADS, HEAD_DIM).transpose(0, 3, 1, 4, 2).astype(BF16)
    yt = pl.pallas_call(
        _sb_kernel,
        grid=(bsz, nq),
        in_specs=[
            pl.BlockSpec((1, SB_HEADS, HEAD_DIM, SB_TQ), lambda b, i: (b, 0, 0, i)),
            pl.BlockSpec((1, SB_HEADS, t, HEAD_DIM), lambda b, i: (b, 0, 0, 0)),
            pl.BlockSpec((1, SB_HEADS, nt, HEAD_DIM, SB_TK), lambda b, i: (b, 0, 0, 0, 0)),
            pl.BlockSpec((SB_TK, SB_TK), lambda b, i: (0, 0)),
        ],
        out_specs=pl.BlockSpec((1, SB_HEADS, HEAD_DIM, SB_TQ), lambda b, i: (b, 0, 0, i)),
        out_shape=jax.ShapeDtypeStruct((bsz, SB_HEADS, HEAD_DIM, t), F32),
        scratch_shapes=[pltpu.VMEM((SB_HEADS, HEAD_DIM, SB_TQ), F32), pltpu.VMEM((SB_HEADS, 1, SB_TQ), F32)],
        compiler_params=_cparams(("parallel", "arbitrary")),
        name="sb_prompt",
    )(qt, k, vt, _sb_tri())
    return yt.transpose(0, 3, 1, 2).reshape(bsz * t, SB_WIDTH)


NSA_TQ = 128
NSA_TQC = 512
NSA_KC = 1024
MASK_BIG = 2.0 ** 100
CMP_CHUNK_W = CMP_STRIDE * 2 * NSA_KV_WIDTH


def _cmp_weights(w_cmp, pe_cmp):
    r = CMP_LEN // CMP_STRIDE
    wj = w_cmp.reshape(2, r, CMP_STRIDE, HEAD_DIM, HEAD_DIM)
    eye = jnp.eye(2 * NSA_KV_HEADS, dtype=F32).reshape(2, NSA_KV_HEADS, 2, NSA_KV_HEADS)
    wb = jnp.einsum('kjsde,kgmh->jskgdmhe', wj, eye)
    wbig = wb.reshape(r, CMP_CHUNK_W, 2 * NSA_KV_WIDTH).transpose(1, 0, 2).reshape(CMP_CHUNK_W, r * 2 * NSA_KV_WIDTH)
    pj = pe_cmp.reshape(2, r, CMP_STRIDE, HEAD_DIM)
    pe = jnp.broadcast_to(pj.transpose(1, 2, 0, 3)[:, :, :, None, :],
                          (r, CMP_STRIDE, 2, NSA_KV_HEADS, HEAD_DIM)).reshape(r, CMP_CHUNK_W)
    pe8 = jnp.concatenate([pe, jnp.zeros((16 - r, CMP_CHUNK_W), F32)], axis=0)
    return wbig.astype(BF16), pe8.astype(BF16)


def _compress_kernel(x_ref, w_ref, pe_ref, o_ref):
    w = w_ref[...]
    width = 2 * NSA_KV_WIDTH
    a = jnp.dot(x_ref[0].astype(BF16), w, preferred_element_type=F32)
    pb = jnp.dot(pe_ref[...], w, preferred_element_type=F32)
    bias = pb[0:1, :width] + pb[1:2, width:]
    nch = a.shape[0]
    a1 = pltpu.roll(a[:, width:], nch - 1, 0)
    o_ref[0] = a[:, :width] + a1 + bias


def _nsa_compress(xc, wbig, pe8):
    bsz, nch, _ = xc.shape
    return pl.pallas_call(
        _compress_kernel,
        grid=(bsz,),
        in_specs=[pl.BlockSpec((1, nch, CMP_CHUNK_W), lambda b: (b, 0, 0)),
                  pl.BlockSpec(wbig.shape, lambda b: (0, 0)),
                  pl.BlockSpec(pe8.shape, lambda b: (0, 0))],
        out_specs=pl.BlockSpec((1, nch, 2 * NSA_KV_WIDTH), lambda b: (b, 0, 0)),
        out_shape=jax.ShapeDtypeStruct((bsz, nch, 2 * NSA_KV_WIDTH), F32),
        compiler_params=_cparams(("parallel",)),
        name="nsa_compress",
    )(xc, wbig, pe8)


def _split_hi_lo(x):
    hi = x.astype(BF16)
    lo = (x - hi.astype(F32)).astype(BF16)
    return jnp.concatenate([hi, lo], axis=1)


def _topk_unselected(imp, cur, axis=1):
    blk = lax.broadcasted_iota(jnp.int32, imp.shape, axis)
    imp = jnp.where(blk == cur, FORCE, jnp.where(blk < cur, imp, -1.0))
    unsel = jnp.ones(imp.shape, F32)
    for _ in range(SEL_TOPK):
        m = jnp.max(imp, axis=axis, keepdims=True)
        idx = jnp.min(jnp.where(imp == m, blk, LANES), axis=axis, keepdims=True)
        hit = blk == idx
        unsel = jnp.where(hit, 0.0, unsel)
        imp = jnp.where(hit, -3.0e38, imp)
    return jnp.where(blk <= cur, unsel, 1.0)


def _nsa_cmp_kernel(qt_ref, kc_ref, vct_ref, cmap_ref, ocmp_ref, unsel_ref):
    tq = NSA_TQC
    ncp = kc_ref.shape[2]
    q0 = pl.program_id(2) * tq
    qpos = q0 + lax.broadcasted_iota(jnp.int32, (1, tq), 1)
    cidx = lax.broadcasted_iota(jnp.int32, (ncp, tq), 0)
    mc = (cidx * CMP_STRIDE + (CMP_LEN - 1) <= qpos) & (cidx < ncp - 1)
    kc = kc_ref[0, 0]
    vct = vct_ref[0, 0]
    psum = jnp.zeros((ncp, tq), F32)
    outs = []
    for r in range(NSA_REP):
        s = jnp.where(mc, jnp.dot(kc, qt_ref[0, 0, 0, :, r * tq:(r + 1) * tq], preferred_element_type=F32), NEG)
        e = jnp.exp(s - jnp.max(s, axis=0, keepdims=True))
        p = jnp.where(mc, e / jnp.sum(e, axis=0, keepdims=True), 0.0)
        outs.append(jnp.dot(vct, p.astype(BF16), preferred_element_type=F32))
        psum = psum + p
    for i in range(tq // NSA_TQ):
        ocmp_ref[0, 0, i] = jnp.concatenate([o[:, i * NSA_TQ:(i + 1) * NSA_TQ] for o in outs], axis=1)
    ph = psum.astype(BF16)
    pl_ = (psum - ph.astype(F32)).astype(BF16)
    imp = jnp.dot(cmap_ref[...], jnp.concatenate([ph, pl_], axis=0), preferred_element_type=F32)
    unsel = _topk_unselected(imp, qpos // SEL_LEN, axis=0)
    for i in range(tq // NSA_TQ):
        unsel_ref[0, i] = unsel[:, i * NSA_TQ:(i + 1) * NSA_TQ]


def _nsa_cmp(qnt, kc, vct, cmap2t, bsz, t):
    nqt = t // NSA_TQC
    per = NSA_TQC // NSA_TQ
    ncp = kc.shape[2]
    return pl.pallas_call(
        _nsa_cmp_kernel,
        grid=(bsz, NSA_KV_HEADS, nqt),
        in_specs=[
            pl.BlockSpec((1, 1, 1, HEAD_DIM, NSA_REP * NSA_TQC), lambda b, g, i: (b, g, i, 0, 0)),
            pl.BlockSpec((1, 1, ncp, HEAD_DIM), lambda b, g, i: (b, g, 0, 0)),
            pl.BlockSpec((1, 1, HEAD_DIM, ncp), lambda b, g, i: (b, g, 0, 0)),
            pl.BlockSpec(cmap2t.shape, lambda b, g, i: (0, 0)),
        ],
        out_specs=[pl.BlockSpec((1, 1, per, HEAD_DIM, NSA_COLS), lambda b, g, i: (b, g, i, 0, 0)),
                   pl.BlockSpec((1, per, LANES, NSA_TQ), lambda b, g, i: (g, b * nqt + i, 0, 0))],
        out_shape=[jax.ShapeDtypeStruct((bsz, NSA_KV_HEADS, t // NSA_TQ, HEAD_DIM, NSA_COLS), F32),
                   jax.ShapeDtypeStruct((NSA_KV_HEADS, bsz * t // NSA_TQ, LANES, NSA_TQ), F32)],
        compiler_params=_cparams(("parallel", "parallel", "arbitrary")),
        name="nsa_cmp_select",
    )(qnt, kc, vct, cmap2t)


NSA_COLS = NSA_REP * NSA_TQ
NSA_KAUG = HEAD_DIM + LANES


def _nsa_selt_kernel(qt_ref, unsel_ref, ocmp_ref, gate_ref, ksa_ref, vst_ref, vsd_ref, kw_ref, vwt_ref, gexp_ref,
                     y_ref, acc_scr):
    tq = NSA_TQ
    qi = pl.program_id(2)
    qt = qt_ref[0, 0, 0]
    unsel = unsel_ref[0, 0]
    krow = lax.broadcasted_iota(jnp.int32, (tq, tq), 0)
    qcol = lax.broadcasted_iota(jnp.int32, (tq, tq), 1)

    def cols4(x):
        return jnp.concatenate([x] * NSA_REP, axis=1)

    u0 = unsel_ref[0, 0, pl.ds(2 * qi, 1), :]
    u1 = unsel_ref[0, 0, pl.ds(2 * qi + 1, 1), :]
    ud = jnp.where(krow < SEL_LEN, u0, u1)
    bias_d = jnp.where((ud < 0.5) & (krow <= qcol), 0.0, -MASK_BIG)
    kd = ksa_ref[0, 0, pl.ds(pl.multiple_of(qi * tq, tq), tq), 0:HEAD_DIM]
    s = jnp.dot(kd, qt, preferred_element_type=F32) + cols4(bias_d)
    m0 = jnp.max(s, axis=0, keepdims=True)
    p = jnp.exp(s - m0)
    l0 = jnp.sum(p, axis=0, keepdims=True)
    acc_scr[...] = jnp.dot(vsd_ref[0, 0, qi], p.astype(BF16), preferred_element_type=F32)

    unsel_past = jnp.where(krow >= 2 * qi, 1.0, unsel).astype(BF16)
    rhs = jnp.concatenate([qt, cols4(unsel_past)], axis=0)

    def chunk(c, carry):
        m_old, l_old = carry
        ka = ksa_ref[0, 0, pl.ds(pl.multiple_of(c * NSA_KC, NSA_KC), NSA_KC), :]
        s = jnp.dot(ka, rhs, preferred_element_type=F32)
        m_new = jnp.maximum(m_old, jnp.max(s, axis=0, keepdims=True))
        alpha = jnp.exp(m_old - m_new)
        p = jnp.exp(s - m_new)
        acc_scr[...] = acc_scr[...] * alpha + jnp.dot(vst_ref[0, 0, c], p.astype(BF16),
                                                      preferred_element_type=F32)
        return m_new, l_old * alpha + jnp.sum(p, axis=0, keepdims=True)

    _, l_sel = lax.fori_loop(0, (qi * tq + NSA_KC - 1) // NSA_KC, chunk, (m0, l0))
    o_sel_t = acc_scr[...] / l_sel

    ntw = (WINDOW + tq) // tq
    t0 = jnp.maximum(qi - WINDOW // tq, 0)
    kw = kw_ref[0, 0, pl.ds(pl.multiple_of(t0 * tq, tq), ntw * tq), :]
    kpos = t0 * tq + lax.broadcasted_iota(jnp.int32, (ntw * tq, tq), 0)
    qpw = qi * tq + lax.broadcasted_iota(jnp.int32, (ntw * tq, tq), 1)
    bias_w = jnp.where((kpos <= qpw) & (kpos > qpw - WINDOW), 0.0, NEG)
    sw = jnp.dot(kw, qt, preferred_element_type=F32) + cols4(bias_w)
    pw = jnp.exp(sw - jnp.max(sw, axis=0, keepdims=True))
    lw = jnp.sum(pw, axis=0, keepdims=True)
    pwb = pw.astype(BF16)
    o_win_t = jnp.zeros((HEAD_DIM, NSA_COLS), F32)
    for i in range(ntw):
        o_win_t = o_win_t + jnp.dot(vwt_ref[0, 0, t0 + i], pwb[i * tq:(i + 1) * tq], preferred_element_type=F32)
    o_win_t = o_win_t / lw

    o_cmp_t = ocmp_ref[0, 0, 0]

    def cols(x, r):
        return x[:, r * tq:(r + 1) * tq]

    both = [jnp.concatenate([cols(o_sel_t, r), cols(o_win_t, r)], axis=0).T for r in range(NSA_REP)]
    cmp2 = [jnp.concatenate([cols(o_cmp_t, r), cols(o_cmp_t, r + 1)], axis=0).T for r in range(0, NSA_REP, 2)]
    o_sel = jnp.concatenate([x[:, :HEAD_DIM] for x in both], axis=1)
    o_win = jnp.concatenate([x[:, HEAD_DIM:] for x in both], axis=1)
    o_cmp = jnp.concatenate(cmp2, axis=1)
    ge = jnp.dot(_split_hi_lo(gate_ref[...]), gexp_ref[0], preferred_element_type=F32)
    w = NSA_REP * HEAD_DIM
    y_ref[...] = ge[:, 0:w] * o_cmp + ge[:, w:2 * w] * o_sel + ge[:, 2 * w:3 * w] * o_win


def _nsa_selt(qt, unsel_t, ocmp, gates, ksa, vst, vsd, kw, vwt, gexp, bsz, t):
    nq = t // NSA_TQ
    w = NSA_REP * HEAD_DIM
    full = lambda a: pl.BlockSpec((1, 1) + a.shape[2:], lambda b, g, i: (b, g) + (0,) * (a.ndim - 2))
    return pl.pallas_call(
        _nsa_selt_kernel,
        grid=(bsz, NSA_KV_HEADS, nq),
        in_specs=[
            pl.BlockSpec((1, 1, 1, HEAD_DIM, NSA_COLS), lambda b, g, i: (b, g, i, 0, 0)),
            pl.BlockSpec((1, 1, LANES, NSA_TQ), lambda b, g, i: (g, b * nq + i, 0, 0)),
            pl.BlockSpec((1, 1, 1, HEAD_DIM, NSA_COLS), lambda b, g, i: (b, g, i, 0, 0)),
            pl.BlockSpec((NSA_TQ, LANES), lambda b, g, i: (b * nq + i, 0)),
            full(ksa), full(vst), full(vsd), full(kw), full(vwt),
            pl.BlockSpec((1,) + gexp.shape[1:], lambda b, g, i: (g, 0, 0)),
        ],
        out_specs=pl.BlockSpec((NSA_TQ, w), lambda b, g, i: (b * nq + i, g)),
        out_shape=jax.ShapeDtypeStruct((bsz * t, NSA_WIDTH), F32),
        scratch_shapes=[pltpu.VMEM((HEAD_DIM, NSA_COLS), F32)],
        compiler_params=_cparams(("parallel", "parallel", "arbitrary")),
        name="nsa_sel_win",
    )(qt, unsel_t, ocmp, gates, ksa, vst, vsd, kw, vwt, gexp)


def _nsa_consts(t):
    ncp = t // CMP_STRIDE
    ns = t // SEL_LEN
    cs = np.arange(ncp) * CMP_STRIDE
    ss = np.arange(LANES) * SEL_LEN
    ov = np.clip(np.minimum(cs[:, None] + CMP_LEN, ss[None, :] + SEL_LEN) - np.maximum(cs[:, None], ss[None, :]), 0, None)
    cmap = (ov / CMP_LEN).astype(np.float32)
    cmap[ncp - 1:, :] = 0.0
    cmap[:, ns:] = 0.0
    cmap2 = jnp.asarray(np.concatenate([cmap, cmap], axis=0), dtype=BF16)
    key_blk = (np.arange(t) // SEL_LEN)[:, None]
    bige = jnp.asarray(np.where(np.arange(LANES)[None, :] == key_blk, -MASK_BIG, 0.0), dtype=BF16)
    ge = np.zeros((NSA_KV_HEADS, LANES, 3 * NSA_REP * HEAD_DIM), np.float32)
    for g in range(NSA_KV_HEADS):
        for r in range(NSA_REP):
            for j in range(3):
                c0 = j * NSA_REP * HEAD_DIM + r * HEAD_DIM
                ge[g, (g * NSA_REP + r) * 3 + j, c0:c0 + HEAD_DIM] = 1.0
    gexp = jnp.asarray(np.concatenate([ge, ge], axis=1), dtype=BF16)
    return cmap2, bige, gexp, cmap2.T


def _nsa_prompt(qn, qr, nsa, win, gates, wbig, pe8, consts, bsz, t):
    _, bige, gexp, cmap2t = consts
    ncp = t // CMP_STRIDE
    xc = nsa[:, :2 * NSA_KV_WIDTH].reshape(bsz, ncp, CMP_CHUNK_W)
    kcv = _nsa_compress(xc, wbig, pe8).reshape(bsz, ncp, 2, NSA_KV_HEADS, HEAD_DIM)
    kc = kcv[:, :, 0].transpose(0, 2, 1, 3).astype(BF16)
    vct = kcv[:, :, 1].transpose(0, 2, 3, 1).astype(BF16)
    nqt = t // NSA_TQC
    qnt = qn.reshape(bsz, nqt, NSA_TQC, NSA_KV_HEADS, NSA_REP, HEAD_DIM).transpose(0, 3, 1, 5, 4, 2)
    qnt = qnt.reshape(bsz, NSA_KV_HEADS, nqt, HEAD_DIM, NSA_REP * NSA_TQC).astype(BF16)
    ocmp, unsel_t = _nsa_cmp(qnt, kc, vct, cmap2t, bsz, t)
    nq = t // NSA_TQ
    qt = qr.reshape(bsz, nq, NSA_TQ, NSA_KV_HEADS, NSA_REP, HEAD_DIM).transpose(0, 3, 1, 5, 4, 2)
    qt = qt.reshape(bsz, NSA_KV_HEADS, nq, HEAD_DIM, NSA_COLS).astype(BF16)

    def rows_major(x):
        return x.reshape(bsz, t, NSA_KV_HEADS, HEAD_DIM).transpose(0, 2, 1, 3).astype(BF16)

    def tiles_t(x, tk):
        return x.reshape(bsz, t // tk, tk, NSA_KV_HEADS, HEAD_DIM).transpose(0, 3, 1, 4, 2).astype(BF16)

    ksa = jnp.concatenate([rows_major(nsa[:, 256:384]),
                           jnp.broadcast_to(bige, (bsz, NSA_KV_HEADS, t, LANES))], axis=-1)
    vst = tiles_t(nsa[:, 384:512], NSA_KC)
    vsd = tiles_t(nsa[:, 384:512], NSA_TQ)
    kw = rows_major(win[:, :128])
    vwt = tiles_t(win[:, 128:], NSA_TQ)
    return _nsa_selt(qt, unsel_t, ocmp, gates, ksa, vst, vsd, kw, vwt, gexp, bsz, t)


PAGE = 128
N_PAGES = PAST_LEN // PAGE
HROWS = 8
NT_DIMS = (((1,), (1,)), ((), ()))


def _sb_tri_rows():
    s = np.arange(PAGE)[:, None]
    j = np.arange(2 * PAGE)[None, :]
    u = -((s > j) | (j >= PAGE)).astype(np.float32)
    return jnp.asarray(np.concatenate([u, u], axis=0), dtype=BF16)


def _pages_t(cache):
    dp, npool = cache.shape[:2]
    return cache.transpose(0, 1, 3, 4, 5, 2).reshape(dp * npool, -1, PAGE)


def _page_specs(feat, blk):
    return [pl.BlockSpec((1, feat, PAGE), lambda b, pt, p=p: (pt[b, p], blk, 0)) for p in range(N_PAGES)]


def _dec_sb_kernel(pt_ref, q_ref, *rest):
    pages = rest[:N_PAGES]
    tri_ref, dmask_ref, y_ref = rest[N_PAGES:]
    q8 = q_ref[0]
    z = jnp.concatenate(
        [jnp.dot(q8, pg[0, 0:SB_WIDTH, :].astype(BF16), preferred_element_type=F32) for pg in pages],
        axis=0)
    sp = jnp.maximum(z, 0.0) + jnp.log(1.0 + jnp.exp(-jnp.abs(z)))
    rt = jnp.dot(_split_hi_lo(sp), tri_ref[...], preferred_element_type=F32)
    cs = [None] * N_PAGES
    c = jnp.zeros((HROWS, PAGE), F32)
    for p in range(N_PAGES - 1, -1, -1):
        cs[p] = c
        c = c + rt[p * HROWS:(p + 1) * HROWS, PAGE:]
    w = jnp.exp(z - sp + rt[:, :PAGE] + jnp.concatenate(cs, axis=0)).astype(BF16)
    y8 = jnp.zeros((HROWS, SB_WIDTH), F32)
    for p, pg in enumerate(pages):
        y8 = y8 + lax.dot_general(w[p * HROWS:(p + 1) * HROWS], pg[0, SB_WIDTH:, :].astype(BF16), NT_DIMS,
                                  preferred_element_type=F32)
    y_ref[0] = jnp.sum(y8 * dmask_ref[...], axis=0, keepdims=True)


def _dec_sb(rows, sq, cache2d):
    bd = sq.shape[0]
    hm = np.zeros((HROWS, SB_WIDTH), np.float32)
    for h in range(SB_HEADS):
        hm[h, h * HEAD_DIM:(h + 1) * HEAD_DIM] = 1.0
    q8 = ((sq * (1.0 / math.sqrt(HEAD_DIM)))[:, None, :] * hm[None]).astype(BF16)
    grid_spec = pltpu.PrefetchScalarGridSpec(
        num_scalar_prefetch=1, grid=(bd,),
        in_specs=[pl.BlockSpec((1, HROWS, SB_WIDTH), lambda b, pt: (b, 0, 0))] + _page_specs(2 * SB_WIDTH, 0)
        + [pl.BlockSpec((2 * PAGE, 2 * PAGE), lambda b, pt: (0, 0)),
           pl.BlockSpec((HROWS, SB_WIDTH), lambda b, pt: (0, 0))],
        out_specs=pl.BlockSpec((1, 1, SB_WIDTH), lambda b, pt: (b, 0, 0)))
    y = pl.pallas_call(
        _dec_sb_kernel, grid_spec=grid_spec,
        out_shape=jax.ShapeDtypeStruct((bd, 1, SB_WIDTH), F32),
        compiler_params=_cparams(("arbitrary",)),
        name="dec_sb",
    )(rows, q8, *([cache2d] * N_PAGES), _sb_tri_rows(), jnp.asarray(hm))
    return y.reshape(bd, SB_WIDTH)


def _dec_cmp_kernel(pt_ref, q_ref, *rest):
    pages = rest[:N_PAGES]
    w_ref, pe_ref, cmap_ref, perm_ref, o_ref, imp_ref = rest[N_PAGES:]
    width = 2 * NSA_KV_WIDTH
    cpp = PAGE // CMP_STRIDE
    ncp = N_PAGES * cpp
    perm = perm_ref[...]
    pp = [lax.dot_general(perm, pg[0].astype(BF16), NT_DIMS, preferred_element_type=F32) for pg in pages]
    xc = jnp.concatenate(
        [jnp.concatenate([x[s * cpp:(s + 1) * cpp] for x in pp], axis=0).astype(BF16) for s in range(CMP_STRIDE)],
        axis=1)
    aa = jnp.dot(jnp.concatenate([xc, pe_ref[...]], axis=0), w_ref[...], preferred_element_type=F32)
    a = aa[:ncp]
    bias = aa[ncp:ncp + 1, :width] + aa[ncp + 1:ncp + 2, width:]
    kcv = a[:, :width] + pltpu.roll(a[:, width:], ncp - 1, 0) + bias
    q8 = q_ref[0]
    s8 = lax.dot_general(q8, kcv[:, :NSA_KV_WIDTH].astype(BF16), NT_DIMS, preferred_element_type=F32)
    valid = lax.broadcasted_iota(jnp.int32, s8.shape, 1) < ncp - 1
    s8 = jnp.where(valid, s8, NEG)
    e = jnp.exp(s8 - jnp.max(s8, axis=-1, keepdims=True))
    p = jnp.where(valid, e / jnp.sum(e, axis=-1, keepdims=True), 0.0)
    o_ref[0] = jnp.dot(p.astype(BF16), kcv[:, NSA_KV_WIDTH:].astype(BF16), preferred_element_type=F32)
    row = lax.broadcasted_iota(jnp.int32, p.shape, 0)
    psum = jnp.where(row < NSA_REP, jnp.sum(p[:NSA_REP], axis=0, keepdims=True),
                     jnp.sum(p[NSA_REP:], axis=0, keepdims=True))
    imp_ref[0] = jnp.dot(_split_hi_lo(psum), cmap_ref[...], preferred_element_type=F32)


def _head_rows(x, bd):
    x = x.reshape(bd, NSA_KV_HEADS, NSA_REP, 1, HEAD_DIM)
    eye = jnp.eye(NSA_KV_HEADS, dtype=x.dtype).reshape(1, NSA_KV_HEADS, 1, NSA_KV_HEADS, 1)
    return (x * eye).reshape(bd, NSA_HEADS, NSA_KV_WIDTH)


def _dec_cmp(rows, qn, cache2d, wbig, pe8, cmap2):
    bd = qn.shape[0]
    q8 = _head_rows(qn, bd).astype(BF16)
    cpp = PAGE // CMP_STRIDE
    pm = np.zeros((PAGE, PAGE), np.float32)
    for s in range(CMP_STRIDE):
        for n in range(cpp):
            pm[s * cpp + n, CMP_STRIDE * n + s] = 1.0
    perm = jnp.asarray(pm, dtype=BF16)
    grid_spec = pltpu.PrefetchScalarGridSpec(
        num_scalar_prefetch=1, grid=(bd,),
        in_specs=[pl.BlockSpec((1, NSA_HEADS, NSA_KV_WIDTH), lambda b, pt: (b, 0, 0))]
        + _page_specs(2 * NSA_KV_WIDTH, 0)
        + [pl.BlockSpec(wbig.shape, lambda b, pt: (0, 0)), pl.BlockSpec(pe8.shape, lambda b, pt: (0, 0)),
           pl.BlockSpec(cmap2.shape, lambda b, pt: (0, 0)), pl.BlockSpec((PAGE, PAGE), lambda b, pt: (0, 0))],
        out_specs=[pl.BlockSpec((1, NSA_HEADS, NSA_KV_WIDTH), lambda b, pt: (b, 0, 0)),
                   pl.BlockSpec((1, NSA_HEADS, LANES), lambda b, pt: (b, 0, 0))])
    return pl.pallas_call(
        _dec_cmp_kernel, grid_spec=grid_spec,
        out_shape=[jax.ShapeDtypeStruct((bd, NSA_HEADS, NSA_KV_WIDTH), F32),
                   jax.ShapeDtypeStruct((bd, NSA_HEADS, LANES), F32)],
        compiler_params=_cparams(("arbitrary",)),
        name="dec_cmp",
    )(rows, q8, *([cache2d] * N_PAGES), wbig, pe8, cmap2, perm)


def _dec_topk_kernel(imp_ref, unsel_ref):
    unsel_ref[...] = _topk_unselected(imp_ref[...], PAST_LEN // SEL_LEN)


def _dec_topk(imp):
    return pl.pallas_call(
        _dec_topk_kernel,
        out_shape=jax.ShapeDtypeStruct(imp.shape, F32),
        name="dec_topk",
    )(imp)


def _dec_sel_kernel(pt_ref, q_ref, unsel_ref, ocmp_ref, gate_ref, new_ref, win_ref, nwin_ref, *rest):
    pages = rest[:N_PAGES]
    bige_ref, y_ref = rest[N_PAGES:]
    q8 = q_ref[0]
    q8f = q8.astype(F32)

    def new_key_score(krow):
        return jnp.sum(q8f * krow.astype(BF16).astype(F32), axis=-1, keepdims=True)

    def attend(s3, s_new, vals_t, v_new):
        m = jnp.maximum(jnp.max(jnp.max(s3, axis=0), axis=-1, keepdims=True), s_new)
        p3 = jnp.exp(s3 - m[None])
        p_new = jnp.exp(s_new - m)
        den = jnp.sum(jnp.sum(p3, axis=0), axis=-1, keepdims=True) + p_new
        o = p_new * v_new.astype(BF16).astype(F32)
        for i, vt in enumerate(vals_t):
            o = o + lax.dot_general(p3[i].astype(BF16), vt, NT_DIMS, preferred_element_type=F32)
        return o / den

    bias = jnp.dot(unsel_ref[0].astype(BF16), bige_ref[...], preferred_element_type=F32)
    s_sel = jnp.stack(
        [jnp.dot(q8, pg[0, 0:NSA_KV_WIDTH, :].astype(BF16), preferred_element_type=F32)
         + bias[:, p * PAGE:(p + 1) * PAGE] for p, pg in enumerate(pages)], axis=0)
    new = new_ref[0]
    o_sel = attend(s_sel, new_key_score(new[:, 2 * NSA_KV_WIDTH:3 * NSA_KV_WIDTH]),
                   [pg[0, NSA_KV_WIDTH:, :].astype(BF16) for pg in pages], new[:, 3 * NSA_KV_WIDTH:])

    nt = WINDOW // PAGE
    kpos0 = lax.broadcasted_iota(jnp.int32, (HROWS, PAGE), 1) == 0
    s_win = []
    for i in range(nt):
        s = jnp.dot(q8, win_ref[0, 0:NSA_KV_WIDTH, i * PAGE:(i + 1) * PAGE].astype(BF16),
                    preferred_element_type=F32)
        s_win.append(jnp.where(kpos0, NEG, s) if i == 0 else s)
    nwin = nwin_ref[0]
    o_win = attend(jnp.stack(s_win, axis=0), new_key_score(nwin[:, :NSA_KV_WIDTH]),
                   [win_ref[0, NSA_KV_WIDTH:, i * PAGE:(i + 1) * PAGE].astype(BF16) for i in range(nt)],
                   nwin[:, NSA_KV_WIDTH:])

    g = gate_ref[0]
    y8 = g[:, 0:LANES] * ocmp_ref[0] + g[:, LANES:2 * LANES] * o_sel + g[:, 2 * LANES:] * o_win
    for h in range(NSA_HEADS):
        c0 = (h // NSA_REP) * HEAD_DIM
        y_ref[0, :, h * HEAD_DIM:(h + 1) * HEAD_DIM] = y8[h:h + 1, c0:c0 + HEAD_DIM]


def _dec_sel(rows, qr, unsel, ocmp, gates, new_nsa, win_state, new_win, cache2d, bige):
    bd = qr.shape[0]
    q8 = _head_rows(qr, bd).astype(BF16)
    g3 = jnp.broadcast_to(gates[:, :3 * NSA_HEADS].reshape(bd, NSA_HEADS, 3, 1),
                          (bd, NSA_HEADS, 3, LANES)).reshape(bd, NSA_HEADS, 3 * LANES)
    per_b = lambda *shape: pl.BlockSpec((1,) + shape, lambda b, pt: (b,) + (0,) * len(shape))
    grid_spec = pltpu.PrefetchScalarGridSpec(
        num_scalar_prefetch=1, grid=(bd,),
        in_specs=[per_b(NSA_HEADS, NSA_KV_WIDTH), per_b(NSA_HEADS, LANES), per_b(NSA_HEADS, NSA_KV_WIDTH),
                  per_b(NSA_HEADS, 3 * LANES), per_b(1, 4 * NSA_KV_WIDTH), per_b(2 * NSA_KV_WIDTH, WINDOW),
                  per_b(1, 2 * NSA_KV_WIDTH)]
        + _page_specs(2 * NSA_KV_WIDTH, 1)
        + [pl.BlockSpec(bige.shape, lambda b, pt: (0, 0))],
        out_specs=per_b(1, NSA_WIDTH))
    y = pl.pallas_call(
        _dec_sel_kernel, grid_spec=grid_spec,
        out_shape=jax.ShapeDtypeStruct((bd, 1, NSA_WIDTH), F32),
        compiler_params=_cparams(("arbitrary",)),
        name="dec_sel_win",
    )(rows, q8, unsel, ocmp, g3, new_nsa.reshape(bd, 1, -1), win_state, new_win.reshape(bd, 1, -1),
      *([cache2d] * N_PAGES), bige)
    return y.reshape(bd, NSA_WIDTH)


def _dec_consts():
    cmap2 = _nsa_consts(PAST_LEN)[0]
    blk = np.arange(PAST_LEN) // SEL_LEN
    bige = jnp.asarray(np.where(np.arange(LANES)[:, None] == blk[None, :], -MASK_BIG, 0.0), dtype=BF16)
    return cmap2, bige


S5_FLAT = SSM_GROUPS * SSM_STATE


def _s5_step_mats(a_re, a_im, log_dt, b_re, b_im, c_re, c_im):
    dt = jnp.exp(log_dt)[:, None]
    mag = jnp.exp(a_re * dt)
    lb_re, lb_im = mag * jnp.cos(a_im * dt), mag * jnp.sin(a_im * dt)
    den = a_re * a_re + a_im * a_im
    co_re = ((lb_re - 1.0) * a_re + lb_im * a_im) / den
    co_im = (lb_im * a_re - (lb_re - 1.0) * a_im) / den
    bb_re = co_re[..., None] * b_re - co_im[..., None] * b_im
    bb_im = co_re[..., None] * b_im + co_im[..., None] * b_re
    eye = jnp.eye(SSM_GROUPS, dtype=F32)
    bm = jnp.concatenate([jnp.einsum('gnp,gh->gphn', bb_re, eye).reshape(SSM_WIDTH, S5_FLAT),
                          jnp.einsum('gnp,gh->gphn', bb_im, eye).reshape(SSM_WIDTH, S5_FLAT)], axis=1)
    cm = jnp.concatenate([jnp.einsum('gpn,gh->gnhp', c_re, eye).reshape(S5_FLAT, SSM_WIDTH),
                          jnp.einsum('gpn,gh->gnhp', -c_im, eye).reshape(S5_FLAT, SSM_WIDTH)], axis=0)
    lb = jnp.concatenate([lb_re.reshape(1, S5_FLAT), lb_im.reshape(1, S5_FLAT)], axis=1)
    return lb, bm.astype(BF16), cm.astype(BF16)


def _s5_step_kernel(u_ref, s0_ref, lb_ref, bm_ref, cm_ref, d_ref, y_ref, s_ref):
    u = u_ref[...]
    bu = jnp.dot(u.astype(BF16), bm_ref[...], preferred_element_type=F32)
    s0r, s0i = s0_ref[:, :S5_FLAT], s0_ref[:, S5_FLAT:]
    lr, li = lb_ref[:, :S5_FLAT], lb_ref[:, S5_FLAT:]
    sr = lr * s0r - li * s0i + bu[:, :S5_FLAT]
    si = lr * s0i + li * s0r + bu[:, S5_FLAT:]
    s_ref[:, :S5_FLAT] = sr
    s_ref[:, S5_FLAT:] = si
    y_ref[...] = (jnp.dot(jnp.concatenate([sr, si], axis=1).astype(BF16), cm_ref[...], preferred_element_type=F32)
                  + d_ref[...] * u)


def _s5_step(u, s0, mats, d):
    lb, bm, cm = mats
    bd = u.shape[0]
    y, s = pl.pallas_call(
        _s5_step_kernel,
        out_shape=[jax.ShapeDtypeStruct((bd, SSM_WIDTH), F32), jax.ShapeDtypeStruct((bd, 2 * S5_FLAT), F32)],
        compiler_params=pltpu.CompilerParams(vmem_limit_bytes=VMEM_LIMIT),
        name="s5_step",
    )(u, s0.reshape(bd, 2 * S5_FLAT), lb, bm, cm, d.reshape(1, SSM_WIDTH))
    return y, s.reshape(bd, 2, SSM_GROUPS, SSM_STATE)


ADA_TN = 1536


def _ada_kernel(c_ref, w_ref, b_ref, o_ref):
    c = c_ref[...]
    h = (c * jax.nn.sigmoid(c)).astype(BF16)
    o_ref[0] = jnp.dot(h, w_ref[0].astype(BF16), preferred_element_type=F32) + b_ref[0]


def _ada(c, w_ada, b_ada):
    rows = c.shape[0]
    width = w_ada.shape[2]
    return pl.pallas_call(
        _ada_kernel,
        grid=(DEPTH, width // ADA_TN),
        in_specs=[pl.BlockSpec((rows, D_MODEL), lambda l, j: (0, 0)),
                  pl.BlockSpec((1, D_MODEL, ADA_TN), lambda l, j: (l, 0, j)),
                  pl.BlockSpec((1, 1, ADA_TN), lambda l, j: (l, 0, j))],
        out_specs=pl.BlockSpec((1, rows, ADA_TN), lambda l, j: (l, 0, j)),
        out_shape=jax.ShapeDtypeStruct((DEPTH, rows, width), F32),
        compiler_params=_cparams(("parallel", "parallel")),
        name="ada_mod",
    )(c, w_ada, b_ada.reshape(DEPTH, 1, width))


def _rope_tables(pos):
    half = HEAD_DIM // 2
    freqs = ROPE_THETA ** (-jnp.arange(half, dtype=F32) / half)
    ang = pos.astype(F32)[:, None] * freqs
    cos, sin = jnp.cos(ang), jnp.sin(ang)
    cos2 = jnp.concatenate([cos, cos], axis=-1)
    sin2 = jnp.concatenate([-sin, sin], axis=-1)
    return jnp.tile(cos2, (1, LANES // HEAD_DIM)), jnp.tile(sin2, (1, LANES // HEAD_DIM))


def _pack_w_in(w):
    pad = jnp.zeros((D_MODEL, LANES - 3 * NSA_HEADS), w.dtype)
    return jnp.concatenate([w[:, :2304], w[:, 2304:2328], pad, w[:, 2328:]], axis=1).astype(BF16)


def kernel(x_prompt, x_sample, cache_sb, cache_nsa, state_win, state_ssm, page_table, c_prompt, c_sample,
           w_ada, b_ada, norm_mix, norm_ffn, w_in, ssm_a_re, ssm_a_im, ssm_log_dt, ssm_b_re, ssm_b_im,
           ssm_c_re, ssm_c_im, ssm_d, w_glu, b_glu, nsa_w_cmp, nsa_pe_cmp, w_branch, w_out,
           w_grp, b_grp, w_rt, b_rt, w_e_gate, w_e_up, w_e_down, final_norm):
    bp, t = x_prompt.shape[:2]
    bd, s = x_sample.shape[:2]
    np_tok = bp * t
    win_len = state_win.shape[2]
    tm_p = 256
    tiles_pb = t // tm_p
    tm_f = 1024
    tiles_fb = t // tm_f

    xp = x_prompt.reshape(np_tok, D_MODEL)
    xs = x_sample.reshape(bd * s, D_MODEL)
    cos_p, sin_p = _rope_tables(jnp.arange(t))
    cos_s, sin_s = _rope_tables(jnp.full((bd,), PAST_LEN))
    c_rows = bp + bd
    c_pad = (-c_rows) % 16
    mods_all = _ada(jnp.concatenate([c_prompt, c_sample, jnp.zeros((c_pad, D_MODEL), F32)], axis=0), w_ada, b_ada)
    nsa_consts = _nsa_consts(t)
    dec_cmap2, dec_bige = _dec_consts()
    n_pool = cache_sb.shape[1]
    cache_sb2d = _pages_t(cache_sb)
    cache_nsa2d = _pages_t(cache_nsa)
    win_t = state_win.transpose(0, 1, 3, 4, 5, 2).reshape(DEPTH, bd, 2 * NSA_KV_WIDTH, win_len)

    sb_p, sb_s, nsa_p, nsa_s, win_p, win_s, ssm_p, ssm_s = [], [], [], [], [], [], [], []
    for l in range(DEPTH):
        w_in_l = _pack_w_in(w_in[l])
        wb_l = w_branch[l].astype(BF16)
        wo_l = w_out[l].astype(BF16)
        wr_l = jnp.concatenate([w_grp[l], w_rt[l], jnp.zeros((D_MODEL, LANES - 20), F32)], axis=1)
        br_l = jnp.concatenate([b_grp[l], b_rt[l], jnp.zeros((LANES - 20,), F32)])[None, :]
        def by_group(w):
            return w.astype(BF16).reshape(MOE_GROUPS, EXPERTS_PER_GROUP, D_MODEL, EXPERT_FF)

        wg_l, wu_l = by_group(w_e_gate[l]), by_group(w_e_up[l])
        wd_l = w_e_down[l].astype(BF16).reshape(MOE_GROUPS, EXPERTS_PER_GROUP * EXPERT_FF, D_MODEL)
        gn_mix = norm_mix[l][None, :]
        gn_ffn = norm_ffn[l][None, :]
        ssm_l = (ssm_a_re[l], ssm_a_im[l], ssm_log_dt[l], ssm_b_re[l], ssm_b_im[l], ssm_c_re[l], ssm_c_im[l],
                 ssm_d[l])
        wglu_l = w_glu[l].astype(BF16)
        bglu_l = b_glu[l][None, :]

        mp = [m[:, None, :] for m in jnp.split(mods_all[l, :bp], 6, axis=-1)]
        u, sq, skv, qn, qr, nsa, win, ng, g = _inproj(
            xp, mp[0], mp[1], gn_mix, cos_p, sin_p, w_in_l, tm_p,
            lambda i: i // tiles_pb, lambda i: i % tiles_pb)
        y_a, sf = _s5_prompt(u.reshape(bp, t, SSM_WIDTH), _s5_mats(*ssm_l[:7]), ssm_d[l])
        y_b = _sb_prompt(sq, skv, bp, t)
        wbig, pe8 = _cmp_weights(nsa_w_cmp[l], nsa_pe_cmp[l])
        y_c = _nsa_prompt(qn, qr, nsa, win, ng, wbig, pe8, nsa_consts, bp, t)
        xp = _merge(xp, mp[2], y_a.reshape(np_tok, SSM_WIDTH), y_b, y_c, g, wglu_l, bglu_l, wb_l, wo_l,
                    tm_p, lambda i: i // tiles_pb)
        xp = _ffn(xp, mp[3], mp[4], mp[5], gn_ffn, wr_l, br_l, wg_l, wu_l, wd_l, tm_f, lambda i: i // tiles_fb)
        sb_p.append(skv.reshape(bp, t, 2, SB_HEADS, HEAD_DIM))
        nsa_p.append(nsa.reshape(bp, t, 4, NSA_KV_HEADS, HEAD_DIM))
        win_p.append(win.reshape(bp, t, 2, NSA_KV_HEADS, HEAD_DIM)[:, t - min(WINDOW, t):])
        ssm_p.append(sf)

        ms = [m[None, :, :] for m in jnp.split(mods_all[l, bp:c_rows], 6, axis=-1)]
        u, sq, skv, qn, qr, nsa, win, ng, g = _inproj(
            xs, ms[0], ms[1], gn_mix, cos_s, sin_s, w_in_l, bd, lambda i: 0, lambda i: 0)
        y_a, s_new = _s5_step(u, state_ssm[l], _s5_step_mats(*ssm_l[:7]), ssm_d[l])
        new_sb = skv.reshape(bd, s, 2, SB_HEADS, HEAD_DIM)
        rows_l = page_table + l * n_pool
        y_b = _dec_sb(rows_l, sq, cache_sb2d)
        new_nsa = nsa.reshape(bd, s, 4, NSA_KV_HEADS, HEAD_DIM)
        ocmp, imp = _dec_cmp(rows_l, qn, cache_nsa2d, wbig, pe8, dec_cmap2)
        unsel = _dec_topk(imp.reshape(bd * NSA_HEADS, LANES)).reshape(bd, NSA_HEADS, LANES)
        new_win = win.reshape(bd, s, 2, NSA_KV_HEADS, HEAD_DIM)
        winc = jnp.concatenate([state_win[l], new_win], axis=1)
        y_c = _dec_sel(rows_l, qr, unsel, ocmp, ng, nsa, win_t[l], win, cache_nsa2d, dec_bige)
        xs = _merge(xs, ms[2], y_a.reshape(bd * s, SSM_WIDTH), y_b, y_c, g, wglu_l, bglu_l, wb_l, wo_l,
                    bd, lambda i: 0)
        xs = _ffn(xs, ms[3], ms[4], ms[5], gn_ffn, wr_l, br_l, wg_l, wu_l, wd_l, bd, lambda i: 0)
        sb_s.append(new_sb)
        nsa_s.append(new_nsa)
        win_s.append(winc[:, s:])
        ssm_s.append(s_new)

    fn = final_norm[None, :]
    y_prompt = _final_norm(xp, fn, tm_f).reshape(bp, t, D_MODEL)
    y_sample = _final_norm(xs, fn, bd).reshape(bd, s, D_MODEL)
    return (y_prompt, y_sample,
            jnp.stack(sb_p), jnp.stack(sb_s),
            jnp.stack(nsa_p), jnp.stack(nsa_s),
            jnp.stack(win_p), jnp.stack(win_s),
            jnp.stack(ssm_p), jnp.stack(ssm_s))
```
